```python
import jax
import jax.numpy as jnp
from jax import lax
import numpy as np

D_MODEL = 1024
BATCH = 1
SEQ = 16384
DEPTH = 2

GRID_W = 64
CTX_LEN = 256
CHUNK = 128
EPS = 1e-6
ROPE_THETA = 10000.0
N_MOD = 9

A_WIDTH = D_MODEL // 2
A_GROUPS = 8
A_GROUP_DIM = A_WIDTH // A_GROUPS
ATT_HEADS = 8
ATT_KV_HEADS = 2
ATT_GROUP = ATT_HEADS // ATT_KV_HEADS
ATT_HEAD_DIM = (D_MODEL // 2) // ATT_HEADS
ATT_Q = ATT_HEADS * ATT_HEAD_DIM
ATT_KV = ATT_KV_HEADS * ATT_HEAD_DIM
EVEN_IN = 2 * A_WIDTH + ATT_Q + 2 * ATT_KV
EVEN_CAT = A_WIDTH + ATT_Q

ML_HEADS = 8
ML_QK_DIM = (D_MODEL // 2) // ML_HEADS
ML_V_DIM = D_MODEL // ML_HEADS
ML_QK = ML_HEADS * ML_QK_DIM
ML_V = ML_HEADS * ML_V_DIM
CONV_WIDTH = 3
ODD_IN = 2 * ML_QK + 2 * ML_V + 4 * ML_HEADS

FFN_HIDDEN = ((8 * D_MODEL // 3 + 127) // 128) * 128
N_EVEN = (DEPTH + 1) // 2
N_ODD = DEPTH // 2

kernel_name = "hybrid_gmlp_gqa_mlstm_dit_trunk"


def rmsnorm(x, g):
    xf = x.astype(jnp.float32)
    y = xf * lax.rsqrt(jnp.mean(xf * xf, axis=-1, keepdims=True) + EPS)
    return (y * g.astype(jnp.float32)).astype(x.dtype)


def modulate(z, g, shift, scale):
    return rmsnorm(z, g) * (1 + scale) + shift


def swiglu(z, w13, w2):
    gate, up = jnp.split(z @ w13, 2, axis=-1)
    return (jax.nn.silu(gate) * up) @ w2


def axial_rope_angles(n_tokens):
    rows = n_tokens // GRID_W
    axis_dim = ATT_HEAD_DIM // 2
    inv_freq = ROPE_THETA ** (-jnp.arange(0, axis_dim, 2, dtype=jnp.float32) / axis_dim)
    row = jnp.broadcast_to(jnp.arange(rows, dtype=jnp.float32)[:, None], (rows, GRID_W)).reshape(-1)
    col = jnp.broadcast_to(jnp.arange(GRID_W, dtype=jnp.float32)[None, :], (rows, GRID_W)).reshape(-1)
    return row[:, None] * inv_freq, col[:, None] * inv_freq


def rotate(x, ang):
    x1, x2 = jnp.split(x, 2, axis=-1)
    cos = jnp.cos(ang)[None, :, None, :].astype(x.dtype)
    sin = jnp.sin(ang)[None, :, None, :].astype(x.dtype)
    return jnp.concatenate([x1 * cos - x2 * sin, x2 * cos + x1 * sin], axis=-1)


def axial_rope(x, ang_row, ang_col):
    xr, xc = jnp.split(x, 2, axis=-1)
    return jnp.concatenate([rotate(xr, ang_row), rotate(xc, ang_col)], axis=-1)


def chunk_spatial_gate(u, v, norm_g, w_s, b_s):
    b, t, _ = v.shape
    vn = rmsnorm(v.reshape(b, t, A_GROUPS, A_GROUP_DIM), norm_g.reshape(A_GROUPS, A_GROUP_DIM))
    vn = vn.reshape(b, t // CHUNK, CHUNK, A_GROUPS, A_GROUP_DIM)
    s = jnp.einsum('gts,bnsgd->bntgd', w_s, vn) + b_s.T[None, None, :, :, None]
    return u * s.reshape(b, t, A_WIDTH)


def gqa_attend(q, k, v):
    b, nq = q.shape[0], q.shape[1]
    qg = q.reshape(b, nq, ATT_KV_HEADS, ATT_GROUP, ATT_HEAD_DIM)
    s = jnp.einsum('bqkgd,bskd->bkgqs', qg, k).astype(jnp.float32) * (ATT_HEAD_DIM ** -0.5)
    p = jax.nn.softmax(s, axis=-1).astype(v.dtype)
    return jnp.einsum('bkgqs,bskd->bqkgd', p, v).reshape(b, nq, ATT_Q)


def gqa_blocks(q, k, v):
    b, t = q.shape[0], q.shape[1]
    qb = jnp.swapaxes(q.reshape(b, t // CHUNK, CHUNK, ATT_HEADS, ATT_HEAD_DIM), 0, 1)
    out = lax.map(lambda qq: gqa_attend(qq, k, v), qb)
    return jnp.swapaxes(out, 0, 1).reshape(b, t, ATT_Q)


def gmlp_gqa_mixer(z_lat, z_ctx, w_in, gate_norm_g, sp_w, sp_b, q_norm_g, k_norm_g, w_out,
                   ang_row, ang_col, ctx_out):
    def project(z):
        p = z @ w_in
        b, t, _ = p.shape
        u = jax.nn.gelu(p[..., :A_WIDTH])
        v = jax.nn.gelu(p[..., A_WIDTH:2 * A_WIDTH])
        o = 2 * A_WIDTH
        q = rmsnorm(p[..., o:o + ATT_Q].reshape(b, t, ATT_HEADS, ATT_HEAD_DIM), q_norm_g)
        k = rmsnorm(p[..., o + ATT_Q:o + ATT_Q + ATT_KV].reshape(b, t, ATT_KV_HEADS, ATT_HEAD_DIM), k_norm_g)
        va = p[..., o + ATT_Q + ATT_KV:].reshape(b, t, ATT_KV_HEADS, ATT_HEAD_DIM)
        return u, v, q, k, va

    u_l, v_l, q_l, k_l, a_l = project(z_lat)
    u_c, v_c, q_c, k_c, a_c = project(z_ctx)
    q_l = axial_rope(q_l, ang_row, ang_col)
    k_l = axial_rope(k_l, ang_row, ang_col)
    k_all = jnp.concatenate([k_c, k_l], axis=1)
    a_all = jnp.concatenate([a_c, a_l], axis=1)
    y_lat = jnp.concatenate([chunk_spatial_gate(u_l, v_l, gate_norm_g, sp_w, sp_b),
                             gqa_blocks(q_l, k_all, a_all)], axis=-1) @ w_out
    if not ctx_out:
        return y_lat, None
    y_ctx = jnp.concatenate([chunk_spatial_gate(u_c, v_c, gate_norm_g, sp_w, sp_b),
                             gqa_attend(q_c, k_c, a_c)], axis=-1) @ w_out
    return y_lat, y_ctx


def short_conv(x, w):
    pad = CONV_WIDTH // 2
    t = x.shape[1]
    xp = jnp.pad(x, ((0, 0), (pad, pad), (0, 0)))
    y = xp[:, 0:t] * w[0]
    for j in range(1, CONV_WIDTH):
        y = y + xp[:, j:j + t] * w[j]
    return y


def to_chunks(t):
    b, n = t.shape[0], t.shape[1] // CHUNK
    t = t.reshape((b, n, CHUNK) + t.shape[2:])
    return jnp.moveaxis(t, 3, 1)


def from_chunks(t):
    t = jnp.moveaxis(t, 1, 3)
    return t.reshape((t.shape[0], t.shape[1] * t.shape[2]) + t.shape[3:])


def mlstm_chunk_states(k, v, log_i, log_f, state0):
    b_cum = jnp.cumsum(log_f, axis=-1)
    b_last = b_cum[..., -1]
    w = b_last[..., None] - b_cum + log_i
    m_loc = jnp.max(w, axis=-1)
    e = jnp.exp(w - m_loc[..., None])
    c_loc = jnp.einsum('bhnl,bhnlv,bhnlk->bhnvk', e, v, k)
    n_loc = jnp.einsum('bhnl,bhnlk->bhnk', e, k)

    def step(carry, inp):
        c_prev, n_prev, m_prev = carry
        cl, nl, ml, bl = inp
        m_new = jnp.maximum(bl + m_prev, ml)
        a = jnp.exp(bl + m_prev - m_new)
        g = jnp.exp(ml - m_new)
        c_new = a[..., None, None] * c_prev + g[..., None, None] * cl
        n_new = a[..., None] * n_prev + g[..., None] * nl
        return (c_new, n_new, m_new), (c_prev, n_prev, m_prev)

    xs = (jnp.moveaxis(c_loc, 2, 0), jnp.moveaxis(n_loc, 2, 0),
          jnp.moveaxis(m_loc, 2, 0), jnp.moveaxis(b_last, 2, 0))
    final, (c_in, n_in, m_in) = lax.scan(step, state0, xs)
    return jnp.moveaxis(c_in, 0, 2), jnp.moveaxis(n_in, 0, 2), jnp.moveaxis(m_in, 0, 2), final


def mlstm_chunk_outputs(q, k, v, log_i, log_f, c_in, n_in, m_in):
    b_cum = jnp.cumsum(log_f, axis=-1)
    ln = q.shape[-2]
    lower = jnp.tril(jnp.ones((ln, ln), dtype=bool))
    dmat = b_cum[..., :, None] - b_cum[..., None, :] + log_i[..., None, :]
    dmat = jnp.where(lower, dmat, -jnp.inf)
    inter = b_cum + m_in[..., None]
    m = jnp.maximum(inter, jnp.max(dmat, axis=-1))
    wts = jnp.exp(dmat - m[..., None])
    a = jnp.exp(inter - m)
    s = jnp.einsum('bhnjd,bhnsd->bhnjs', q, k) * wts
    num = jnp.einsum('bhnjs,bhnsv->bhnjv', s, v) + a[..., None] * jnp.einsum('bhnvk,bhnjk->bhnjv', c_in, q)
    den = jnp.sum(s, axis=-1) + a * jnp.einsum('bhnk,bhnjk->bhnj', n_in, q)
    return num / jnp.maximum(jnp.abs(den), jnp.exp(-m))[..., None]


def mlstm_mixer(z_lat, z_ctx, w_in, conv_w, gate_b, out_norm_g, w_out, ctx_out):
    def project(z):
        p = z @ w_in
        b, t, _ = p.shape
        qk = jax.nn.silu(short_conv(p[..., :2 * ML_QK], conv_w)).astype(jnp.float32)
        q = qk[..., :ML_QK].reshape(b, t, ML_HEADS, ML_QK_DIM) * (ML_QK_DIM ** -0.5)
        k = qk[..., ML_QK:].reshape(b, t, ML_HEADS, ML_QK_DIM)
        v = p[..., 2 * ML_QK:2 * ML_QK + ML_V].reshape(b, t, ML_HEADS, ML_V_DIM).astype(jnp.float32)
        o = p[..., 2 * ML_QK + ML_V:2 * ML_QK + 2 * ML_V]
        g = (p[..., 2 * ML_QK + 2 * ML_V:] + gate_b).astype(jnp.float32).reshape(b, t, 4, ML_HEADS)
        return q, k, v, o, g

    lat = project(z_lat)
    ctx = project(z_ctx)
    bsz = z_lat.shape[0]
    zero = (jnp.zeros((bsz, ML_HEADS, ML_V_DIM, ML_QK_DIM), jnp.float32),
            jnp.zeros((bsz, ML_HEADS, ML_QK_DIM), jnp.float32),
            jnp.zeros((bsz, ML_HEADS), jnp.float32))

    def run_direction(d):
        def orient(t):
            return jnp.flip(t, axis=1) if d == 1 else t

        def chunked(side):
            q, k, v, _, g = side
            log_i = g[:, :, 2 * d]
            log_f = jax.nn.log_sigmoid(g[:, :, 2 * d + 1])
            return tuple(to_chunks(orient(t)) for t in (q, k, v, log_i, log_f))

        qc, kc, vc, ic, fc = chunked(ctx)
        ql, kl, vl, il, fl = chunked(lat)
        cc_in, nc_in, mc_in, ctx_final = mlstm_chunk_states(kc, vc, ic, fc, zero)
        cl_in, nl_in, ml_in, _ = mlstm_chunk_states(kl, vl, il, fl, ctx_final)
        h_lat = orient(from_chunks(mlstm_chunk_outputs(ql, kl, vl, il, fl, cl_in, nl_in, ml_in)))
        h_ctx = orient(from_chunks(mlstm_chunk_outputs(qc, kc, vc, ic, fc, cc_in, nc_in, mc_in))) if ctx_out else None
        return h_lat, h_ctx

    hf_lat, hf_ctx = run_direction(0)
    hb_lat, hb_ctx = run_direction(1)

    def readout(h, o):
        b, t = h.shape[0], h.shape[1]
        hn = rmsnorm(h, out_norm_g.reshape(ML_HEADS, ML_V_DIM)).reshape(b, t, ML_V).astype(o.dtype)
        return (hn * jax.nn.sigmoid(o)) @ w_out

    y_lat = readout(hf_lat + hb_lat, lat[3])
    y_ctx = readout(hf_ctx + hb_ctx, ctx[3]) if ctx_out else None
    return y_lat, y_ctx


def setup_inputs(seed: int = 0) -> dict:
    key = jax.random.key(seed)
    ks = jax.random.split(key, 24)

    def nrm(k, shape, scale=1.0):
        return jax.random.normal(k, shape, jnp.float32) * scale

    d, f = D_MODEL, FFN_HIDDEN
    gi = nrm(ks[18], (N_ODD, 2, ML_HEADS), 0.1)
    gf = jnp.linspace(3.0, 6.0, ML_HEADS, dtype=jnp.float32) + nrm(ks[19], (N_ODD, 2, ML_HEADS), 0.1)
    ml_gate_b = jnp.stack([gi[:, 0], gf[:, 0], gi[:, 1], gf[:, 1]], axis=1).reshape(N_ODD, 4 * ML_HEADS)
    return {
        "x": nrm(ks[0], (BATCH, SEQ, d)),
        "c": nrm(ks[1], (BATCH, d)),
        "ctx": nrm(ks[2], (BATCH, CTX_LEN, d)),
        "c_ctx": nrm(ks[3], (d,)),
        "mod_w": nrm(ks[4], (DEPTH, d, N_MOD * d), 0.5 * d ** -0.5),
        "mod_b": nrm(ks[5], (DEPTH, N_MOD * d), 0.1),
        "norm_g": 1.0 + nrm(ks[6], (DEPTH, 3, d), 0.1),
        "ffn_w13": nrm(ks[7], (DEPTH, 2, d, 2 * f), d ** -0.5),
        "ffn_w2": nrm(ks[8], (DEPTH, 2, f, d), f ** -0.5),
        "ab_w_in": nrm(ks[9], (N_EVEN, d, EVEN_IN), d ** -0.5),
        "ab_gate_norm_g": 1.0 + nrm(ks[10], (N_EVEN, A_WIDTH), 0.1),
        "ab_spatial_w": nrm(ks[11], (N_EVEN, A_GROUPS, CHUNK, CHUNK), CHUNK ** -0.5),
        "ab_spatial_b": 1.0 + nrm(ks[12], (N_EVEN, A_GROUPS, CHUNK), 0.1),
        "ab_q_norm_g": 1.0 + nrm(ks[13], (N_EVEN, ATT_HEAD_DIM), 0.1),
        "ab_k_norm_g": 1.0 + nrm(ks[14], (N_EVEN, ATT_HEAD_DIM), 0.1),
        "ab_w_out": nrm(ks[15], (N_EVEN, EVEN_CAT, d), EVEN_CAT ** -0.5),
        "ml_w_in": nrm(ks[16], (N_ODD, d, ODD_IN), d ** -0.5),
        "ml_conv_w": nrm(ks[17], (N_ODD, CONV_WIDTH, 2 * ML_QK), CONV_WIDTH ** -0.5),
        "ml_gate_b": ml_gate_b,
        "ml_out_norm_g": 1.0 + nrm(ks[20], (N_ODD, ML_V), 0.1),
        "ml_w_out": nrm(ks[21], (N_ODD, ML_V, d), ML_V ** -0.5),
        "final_norm_g": 1.0 + nrm(ks[22], (d,), 0.1),
    }


def reference(x, c, ctx, c_ctx, mod_w, mod_b, norm_g, ffn_w13, ffn_w2,
              ab_w_in, ab_gate_norm_g, ab_spatial_w, ab_spatial_b, ab_q_norm_g, ab_k_norm_g, ab_w_out,
              ml_w_in, ml_conv_w, ml_gate_b, ml_out_norm_g, ml_w_out, final_norm_g):
    b, t, d = x.shape
    ang_row, ang_col = axial_rope_angles(t)
    s_lat = jax.nn.silu(c)
    s_ctx = jax.nn.silu(c_ctx)[None, :]
    h, hc = x, ctx
    for l in range(DEPTH):
        keep_ctx = l < DEPTH - 1
        m_l = (s_lat @ mod_w[l] + mod_b[l]).reshape(b, N_MOD, 1, d)
        m_c = (s_ctx @ mod_w[l] + mod_b[l]).reshape(1, N_MOD, 1, d)
        h = h + 0.5 * m_l[:, 2] * swiglu(modulate(h, norm_g[l, 0], m_l[:, 0], m_l[:, 1]), ffn_w13[l, 0], ffn_w2[l, 0])
        hc = hc + 0.5 * m_c[:, 2] * swiglu(modulate(hc, norm_g[l, 0], m_c[:, 0], m_c[:, 1]), ffn_w13[l, 0], ffn_w2[l, 0])
        z_l = modulate(h, norm_g[l, 1], m_l[:, 3], m_l[:, 4])
        z_c = modulate(hc, norm_g[l, 1], m_c[:, 3], m_c[:, 4])
        if l % 2 == 0:
            e = l // 2
            y_l, y_c = gmlp_gqa_mixer(z_l, z_c, ab_w_in[e], ab_gate_norm_g[e], ab_spatial_w[e], ab_spatial_b[e],
                                      ab_q_norm_g[e], ab_k_norm_g[e], ab_w_out[e], ang_row, ang_col, keep_ctx)
        else:
            o = l // 2
            y_l, y_c = mlstm_mixer(z_l, z_c, ml_w_in[o], ml_conv_w[o], ml_gate_b[o], ml_out_norm_g[o],
                                   ml_w_out[o], keep_ctx)
        h = h + m_l[:, 5] * y_l
        h = h + 0.5 * m_l[:, 8] * swiglu(modulate(h, norm_g[l, 2], m_l[:, 6], m_l[:, 7]), ffn_w13[l, 1], ffn_w2[l, 1])
        if keep_ctx:
            hc = hc + m_c[:, 5] * y_c
            hc = hc + 0.5 * m_c[:, 8] * swiglu(modulate(hc, norm_g[l, 2], m_c[:, 6], m_c[:, 7]), ffn_w13[l, 1], ffn_w2[l, 1])
    return rmsnorm(h, final_norm_g)
```

```python
import functools

import jax
import jax.numpy as jnp
from jax import lax
from jax.experimental import pallas as pl
from jax.experimental.pallas import tpu as pltpu

F32 = jnp.float32
BF16 = jnp.bfloat16

LANES = 128
SUBLANES = 8
VMEM_LIMIT = 56 * 1024 * 1024

GRID_W = 64
CHUNK = 128
EPS = 1e-6
ROPE_THETA = 10000.0
N_MOD = 9
A_GROUPS = 8
A_GROUP_DIM = 64
ATT_HEADS = 8
ATT_KV_HEADS = 2
ATT_HEAD_DIM = 64
ML_HEADS = 8
ML_QK_DIM = 64
ML_V_DIM = 128
CONV_WIDTH = 3

TM = 256
ATT_TK = 1024


def _cparams(*sem):
    return pltpu.CompilerParams(dimension_semantics=sem, vmem_limit_bytes=VMEM_LIMIT)


def _full(shape):
    n = len(shape)
    return pl.BlockSpec(shape, lambda *_: (0,) * n)


def _rms_mod(x, g, shift, scale):
    ms = jnp.mean(x * x, axis=-1, keepdims=True)
    y = x * lax.rsqrt(ms + EPS) * g
    return y * (1 + scale) + shift


def _silu(x):
    return x * jax.nn.sigmoid(x)


def _split3(x):
    x1 = x.astype(BF16)
    r = x - x1.astype(F32)
    x2 = r.astype(BF16)
    x3 = (r - x2.astype(F32)).astype(BF16)
    return x1, x2, x3


def _group_mean_sq(x, bmat):
    sq = x * x
    hi = sq.astype(BF16)
    lo = (sq - hi.astype(F32)).astype(BF16)
    return (jnp.dot(hi, bmat, preferred_element_type=F32)
            + jnp.dot(lo, bmat, preferred_element_type=F32))


def _mod_kernel(c_ref, w_ref, b_ref, o_ref):
    s = _silu(c_ref[...])
    o_ref[0] = jnp.dot(s, w_ref[0], preferred_element_type=F32,
                       precision=lax.Precision.HIGHEST) + b_ref[0]


def _mods(c, c_ctx, mod_w, mod_b):
    depth, d, nd = mod_w.shape
    cvec = jnp.zeros((SUBLANES, d), F32).at[0].set(c[0]).at[1].set(c_ctx)
    out = pl.pallas_call(
        _mod_kernel,
        grid=(depth, nd // d),
        in_specs=[
            pl.BlockSpec((SUBLANES, d), lambda l, j: (0, 0)),
            pl.BlockSpec((1, d, d), lambda l, j: (l, 0, j)),
            pl.BlockSpec((1, 1, d), lambda l, j: (l, 0, j)),
        ],
        out_specs=pl.BlockSpec((1, SUBLANES, d), lambda l, j: (l, 0, j)),
        out_shape=jax.ShapeDtypeStruct((depth, SUBLANES, nd), F32),
        compiler_params=_cparams("arbitrary", "arbitrary"),
        name="modulation",
    )(cvec, mod_w, mod_b.reshape(depth, 1, nd))
    return out[:, :2].reshape(depth, 2, N_MOD, d)


def _ffn_kernel(h_ref, mod_ref, g_ref, w13_ref, w2_ref, *rest, mod_base, final):
    o_ref = rest[-1]
    x = h_ref[...]
    m = mod_ref[0]
    z = _rms_mod(x, g_ref[...], m[mod_base:mod_base + 1], m[mod_base + 1:mod_base + 2]).astype(BF16)
    f = w2_ref.shape[0]
    gu = jnp.dot(z, w13_ref[...], preferred_element_type=F32)
    a = (_silu(gu[:, :f]) * gu[:, f:]).astype(BF16)
    y = jnp.dot(a, w2_ref[...], preferred_element_type=F32)
    out = x + (0.5 * m[mod_base + 2:mod_base + 3]) * y
    if final:
        ms = jnp.mean(out * out, axis=-1, keepdims=True)
        out = out * lax.rsqrt(ms + EPS) * rest[0][...]
    o_ref[...] = out


def _ffn(h, mod, g, w13, w2, n_lat_t, mod_base, final_g=None):
    rows, d = h.shape
    f = w2.shape[0]
    final = final_g is not None
    in_specs = [
        pl.BlockSpec((TM, d), lambda i: (i, 0)),
        pl.BlockSpec((1, N_MOD, d), lambda i: (i // n_lat_t, 0, 0)),
        _full((1, d)),
        _full((d, 2 * f)),
        _full((f, d)),
    ]
    args = [h, mod, g.reshape(1, d), w13.astype(BF16), w2.astype(BF16)]
    if final:
        in_specs.append(_full((1, d)))
        args.append(final_g.reshape(1, d))
    return pl.pallas_call(
        functools.partial(_ffn_kernel, mod_base=mod_base, final=final),
        grid=(rows // TM,),
        in_specs=in_specs,
        out_specs=pl.BlockSpec((TM, d), lambda i: (i, 0)),
        out_shape=jax.ShapeDtypeStruct((rows, d), F32),
        compiler_params=_cparams("arbitrary"),
        name="ffn",
    )(*args)


def _even_proj_kernel(h_ref, mod_ref, g_ref, win_ref, gng_ref, qg_ref, kg_ref, spw_ref, spb_ref,
                      cos_ref, sin_ref, b512_ref, b128_ref,
                      gated_ref, q_ref, kd0_ref, kd1_ref, va_ref, vb_ref):
    x = h_ref[...]
    m = mod_ref[0]
    z = _rms_mod(x, g_ref[...], m[3:4], m[4:5]).astype(BF16)
    p = jnp.dot(z, win_ref[...], preferred_element_type=F32)
    aw = A_GROUPS * A_GROUP_DIM
    qw = ATT_HEADS * ATT_HEAD_DIM
    u = jax.nn.gelu(p[:, :aw])
    v = jax.nn.gelu(p[:, aw:2 * aw])
    b512 = b512_ref[...]
    vn = (v * lax.rsqrt(_group_mean_sq(v, b512) + EPS) * gng_ref[...]).astype(BF16)

    lane = lax.broadcasted_iota(jnp.int32, (CHUNK, LANES), 1)
    lo = lane < A_GROUP_DIM
    for c in range(TM // CHUNK):
        r0 = c * CHUNK
        for j in range(aw // LANES):
            vt = vn[r0:r0 + CHUNK, j * LANES:(j + 1) * LANES]
            s0 = jnp.dot(spw_ref[2 * j], vt, preferred_element_type=F32)
            s1 = jnp.dot(spw_ref[2 * j + 1], vt, preferred_element_type=F32)
            s = jnp.where(lo, s0, s1) + spb_ref[:, j * LANES:(j + 1) * LANES]
            gated_ref[r0:r0 + CHUNK, j * LANES:(j + 1) * LANES] = (
                u[r0:r0 + CHUNK, j * LANES:(j + 1) * LANES] * s).astype(BF16)

    o = 2 * aw
    pq = p[:, o:o + qw]
    pk = p[:, o + qw:o + qw + LANES]
    va = p[:, o + qw + LANES:o + qw + 2 * LANES]
    qn = pq * lax.rsqrt(_group_mean_sq(pq, b512) + EPS) * qg_ref[...]
    kn = pk * lax.rsqrt(_group_mean_sq(pk, b128_ref[...]) + EPS) * kg_ref[...]

    cos = cos_ref[...]
    sin = sin_ref[...]
    lane_t = lax.broadcasted_iota(jnp.int32, (TM, LANES), 1)
    first_half = (lane_t % 32) < 16
    lo_t = lane_t < ATT_HEAD_DIM

    def rope(t):
        partner = jnp.where(first_half, pltpu.roll(t, LANES - 16, 1), pltpu.roll(t, 16, 1))
        return t * cos + partner * sin

    scale = ATT_HEAD_DIM ** -0.5
    for j in range(qw // LANES):
        q_ref[:, j * LANES:(j + 1) * LANES] = (rope(qn[:, j * LANES:(j + 1) * LANES]) * scale).astype(BF16)
    kr = rope(kn)
    kswap = pltpu.roll(kr, ATT_HEAD_DIM, 1)
    kd0_ref[...] = jnp.where(lo_t, kr, kswap).astype(BF16)
    kd1_ref[...] = jnp.where(lo_t, kswap, kr).astype(BF16)
    va_ref[...] = va.astype(BF16)
    vb_ref[...] = pltpu.roll(va, ATT_HEAD_DIM, 1).astype(BF16)


def _group_matrix(width, gs):
    r = jnp.arange(width) // gs
    return jnp.where(r[:, None] == r[None, :], 1.0 / gs, 0.0).astype(BF16)


def _rope_tables(t_lat, rows):
    axis_dim = ATT_HEAD_DIM // 2
    inv_freq = ROPE_THETA ** (-jnp.arange(0, axis_dim, 2, dtype=F32) / axis_dim)
    t = jnp.arange(rows)
    is_lat = (t < t_lat)[:, None]
    row = (t // GRID_W).astype(F32)[:, None]
    col = (t % GRID_W).astype(F32)[:, None]
    lane = jnp.arange(LANES)
    freq = inv_freq[lane % (axis_dim // 2)][None, :]
    use_row = ((lane % ATT_HEAD_DIM) < axis_dim)[None, :]
    ang = jnp.where(is_lat, jnp.where(use_row, row, col) * freq, 0.0)
    sign = jnp.where((lane % axis_dim) < axis_dim // 2, -1.0, 1.0)[None, :]
    return jnp.cos(ang), jnp.sin(ang) * sign


def _even_proj(h, mod, g, w_in, gate_norm_g, sp_w, sp_b, q_norm_g, k_norm_g, t_lat, n_lat_t):
    rows, d = h.shape
    aw = A_GROUPS * A_GROUP_DIM
    qw = ATT_HEADS * ATT_HEAD_DIM
    cos, sin = _rope_tables(t_lat, rows)
    spb = jnp.repeat(sp_b.T, A_GROUP_DIM, axis=1)
    row_tile = lambda w: pl.BlockSpec((TM, w), lambda i: (i, 0))
    out_shape = [jax.ShapeDtypeStruct((rows, aw), BF16), jax.ShapeDtypeStruct((rows, qw), BF16)]
    out_shape += [jax.ShapeDtypeStruct((rows, LANES), BF16)] * 4
    return pl.pallas_call(
        _even_proj_kernel,
        grid=(rows // TM,),
        in_specs=[
            row_tile(d),
            pl.BlockSpec((1, N_MOD, d), lambda i: (i // n_lat_t, 0, 0)),
            _full((1, d)),
            _full(w_in.shape),
            _full((1, aw)),
            _full((1, qw)),
            _full((1, LANES)),
            _full(sp_w.shape),
            _full((CHUNK, aw)),
            row_tile(LANES),
            row_tile(LANES),
            _full((aw, aw)),
            _full((LANES, LANES)),
        ],
        out_specs=[row_tile(aw), row_tile(qw)] + [row_tile(LANES)] * 4,
        out_shape=out_shape,
        compiler_params=_cparams("arbitrary"),
        name="even_proj",
    )(h, mod, g.reshape(1, d), w_in.astype(BF16), gate_norm_g.reshape(1, aw),
      jnp.tile(q_norm_g, ATT_HEADS).reshape(1, qw), jnp.tile(k_norm_g, ATT_KV_HEADS).reshape(1, LANES),
      sp_w.astype(BF16), spb, cos, sin, _group_matrix(aw, A_GROUP_DIM), _group_matrix(LANES, ATT_HEAD_DIM))


def _attn_kernel(q_ref, kd0_ref, kd1_ref, va_ref, vb_ref, o_ref, qs_ref, m_ref, l_ref, acc_ref,
                 *, n_lat_t, t_lat, ctx_len):
    i = pl.program_id(0)
    tq = q_ref.shape[0]
    lane = lax.broadcasted_iota(jnp.int32, (tq, LANES), 1)
    lo = lane < ATT_HEAD_DIM
    n_full = jnp.where(i < n_lat_t, t_lat // ATT_TK, 0)

    for g in range(ATT_KV_HEADS):
        kd_ref = (kd0_ref, kd1_ref)[g]
        v_lo_ref = (va_ref, vb_ref)[g]
        v_hi_ref = (vb_ref, va_ref)[g]
        t0 = q_ref[:, (2 * g) * LANES:(2 * g + 1) * LANES]
        t1 = q_ref[:, (2 * g + 1) * LANES:(2 * g + 2) * LANES]
        zero = jnp.zeros_like(t0)
        qs_ref[0 * tq:1 * tq] = jnp.where(lo, t0, zero)
        qs_ref[1 * tq:2 * tq] = jnp.where(lo, t1, zero)
        qs_ref[2 * tq:3 * tq] = jnp.where(lo, zero, t0)
        qs_ref[3 * tq:4 * tq] = jnp.where(lo, zero, t1)
        m_ref[...] = jnp.full(m_ref.shape, -jnp.inf, F32)
        l_ref[...] = jnp.zeros(l_ref.shape, F32)
        acc_ref[...] = jnp.zeros(acc_ref.shape, F32)

        def chunk(start, size):
            kc = kd_ref[pl.ds(start, size), :]
            s = lax.dot_general(qs_ref[...], kc, (((1,), (1,)), ((), ())),
                                preferred_element_type=F32)
            m_old = m_ref[...]
            m_new = jnp.maximum(m_old, jnp.max(s, axis=-1, keepdims=True))
            alpha = jnp.exp(m_old - m_new)
            p = jnp.exp(s - m_new)
            l_ref[...] = alpha * l_ref[...] + jnp.sum(p, axis=-1, keepdims=True)
            pb = p.astype(BF16)
            pv_lo = jnp.dot(pb[:2 * tq], v_lo_ref[pl.ds(start, size), :], preferred_element_type=F32)
            pv_hi = jnp.dot(pb[2 * tq:], v_hi_ref[pl.ds(start, size), :], preferred_element_type=F32)
            acc_ref[:2 * tq] = alpha[:2 * tq] * acc_ref[:2 * tq] + pv_lo
            acc_ref[2 * tq:] = alpha[2 * tq:] * acc_ref[2 * tq:] + pv_hi
            m_ref[...] = m_new

        def body(c, carry):
            chunk(pl.multiple_of(c * ATT_TK, ATT_TK), ATT_TK)
            return carry

        lax.fori_loop(0, n_full, body, 0)
        chunk(t_lat, ctx_len)

        o = acc_ref[...] * (1.0 / l_ref[...])
        o_ref[:, (2 * g) * LANES:(2 * g + 1) * LANES] = jnp.where(lo, o[0 * tq:1 * tq], o[2 * tq:3 * tq]).astype(BF16)
        o_ref[:, (2 * g + 1) * LANES:(2 * g + 2) * LANES] = jnp.where(lo, o[1 * tq:2 * tq], o[3 * tq:4 * tq]).astype(BF16)


def _attention(q, kd0, kd1, va, vb, t_lat, n_lat_t):
    rows, qw = q.shape
    ctx_len = rows - t_lat
    group = ATT_HEADS // ATT_KV_HEADS
    return pl.pallas_call(
        functools.partial(_attn_kernel, n_lat_t=n_lat_t, t_lat=t_lat, ctx_len=ctx_len),
        grid=(rows // TM,),
        in_specs=[pl.BlockSpec((TM, qw), lambda i: (i, 0))] + [_full((rows, LANES))] * 4,
        out_specs=pl.BlockSpec((TM, qw), lambda i: (i, 0)),
        out_shape=jax.ShapeDtypeStruct((rows, qw), BF16),
        scratch_shapes=[
            pltpu.VMEM((group * TM, LANES), BF16),
            pltpu.VMEM((group * TM, 1), F32),
            pltpu.VMEM((group * TM, 1), F32),
            pltpu.VMEM((group * TM, LANES), F32),
        ],
        compiler_params=_cparams("arbitrary"),
        name="gqa_attention",
    )(q, kd0, kd1, va, vb)


def _out_even_kernel(h_ref, mod_ref, a_ref, b_ref, w_ref, o_ref):
    aw = a_ref.shape[1]
    y = (jnp.dot(a_ref[...], w_ref[:aw], preferred_element_type=F32)
         + jnp.dot(b_ref[...], w_ref[aw:], preferred_element_type=F32))
    o_ref[...] = h_ref[...] + mod_ref[0][5:6] * y


def _out_even(h, mod, gated, attn, w_out, n_tiles, n_lat_t):
    d = h.shape[1]
    row_tile = lambda w: pl.BlockSpec((TM, w), lambda i: (i, 0))
    return pl.pallas_call(
        _out_even_kernel,
        grid=(n_tiles,),
        in_specs=[row_tile(d), pl.BlockSpec((1, N_MOD, d), lambda i: (i // n_lat_t, 0, 0)),
                  row_tile(gated.shape[1]), row_tile(attn.shape[1]), _full(w_out.shape)],
        out_specs=row_tile(d),
        out_shape=jax.ShapeDtypeStruct((n_tiles * TM, d), F32),
        compiler_params=_cparams("arbitrary"),
        name="even_out",
    )(h, mod, gated, attn, w_out.astype(BF16))


def _odd_proj_kernel(h_ref, mod_ref, g_ref, win_ref, gb_ref, pqk_ref, v_ref, o_ref, gates_ref):
    x = h_ref[...]
    m = mod_ref[0]
    z = _rms_mod(x, g_ref[...], m[3:4], m[4:5]).astype(BF16)
    p = jnp.dot(z, win_ref[...], preferred_element_type=F32)
    qk = 2 * ML_HEADS * ML_QK_DIM
    vw = ML_HEADS * ML_V_DIM
    pqk_ref[...] = p[:, :qk]
    v_ref[...] = p[:, qk:qk + vw].astype(BF16)
    o_ref[...] = p[:, qk + vw:qk + 2 * vw]
    gts = p[:, qk + 2 * vw:] + gb_ref[...]
    lane = lax.broadcasted_iota(jnp.int32, gts.shape, 1)
    is_f = (lane < 4 * ML_HEADS) & ((lane // ML_HEADS) % 2 == 1)
    log_sig = jnp.minimum(gts, 0.0) - jnp.log1p(jnp.exp(-jnp.abs(gts)))
    gates_ref[...] = jnp.where(is_f, log_sig, gts)


def _odd_proj(h, mod, g, w_in, gate_b, n_lat_t):
    rows, d = h.shape
    qk = 2 * ML_HEADS * ML_QK_DIM
    vw = ML_HEADS * ML_V_DIM
    n_gate = 4 * ML_HEADS
    w_pad = jnp.pad(w_in, ((0, 0), (0, LANES - n_gate))).astype(BF16)
    gb = jnp.pad(gate_b, (0, LANES - n_gate)).reshape(1, LANES)
    row_tile = lambda w: pl.BlockSpec((TM, w), lambda i: (i, 0))
    return pl.pallas_call(
        _odd_proj_kernel,
        grid=(rows // TM,),
        in_specs=[row_tile(d), pl.BlockSpec((1, N_MOD, d), lambda i: (i // n_lat_t, 0, 0)),
                  _full((1, d)), _full(w_pad.shape), _full((1, LANES))],
        out_specs=[row_tile(qk), row_tile(vw), row_tile(vw), row_tile(LANES)],
        out_shape=[jax.ShapeDtypeStruct((rows, qk), F32), jax.ShapeDtypeStruct((rows, vw), BF16),
                   jax.ShapeDtypeStruct((rows, vw), F32), jax.ShapeDtypeStruct((rows, LANES), F32)],
        compiler_params=_cparams("arbitrary"),
        name="odd_proj",
    )(h, mod, g.reshape(1, d), w_pad, gb)


def _conv_kernel(x_ref, prev_ref, next_ref, w_ref, q_ref, k_ref, *, n_lat_t):
    i = pl.program_id(0)
    nt = pl.num_programs(0)
    x = x_ref[...]
    tm = x.shape[0]
    first = (i == 0) | (i == n_lat_t)
    last = (i == n_lat_t - 1) | (i == nt - 1)
    prev_row = jnp.where(first, 0.0, prev_ref[SUBLANES - 1:SUBLANES, :])
    next_row = jnp.where(last, 0.0, next_ref[0:1, :])
    row = lax.broadcasted_iota(jnp.int32, x.shape, 0)
    xm = jnp.where(row == 0, prev_row, pltpu.roll(x, 1, 0))
    xp = jnp.where(row == tm - 1, next_row, pltpu.roll(x, tm - 1, 0))
    y = xm * w_ref[0:1, :] + x * w_ref[1:2, :] + xp * w_ref[2:3, :]
    y = _silu(y)
    qk = ML_HEADS * ML_QK_DIM
    q_ref[...] = (y[:, :qk] * (ML_QK_DIM ** -0.5)).astype(BF16)
    k_ref[...] = y[:, qk:].astype(BF16)


def _conv(pqk, conv_w, n_lat_t):
    rows, w = pqk.shape
    per = TM // SUBLANES
    last_blk = rows // SUBLANES - 1
    return pl.pallas_call(
        functools.partial(_conv_kernel, n_lat_t=n_lat_t),
        grid=(rows // TM,),
        in_specs=[
            pl.BlockSpec((TM, w), lambda i: (i, 0)),
            pl.BlockSpec((SUBLANES, w), lambda i: (jnp.maximum(i * per - 1, 0), 0)),
            pl.BlockSpec((SUBLANES, w), lambda i: (jnp.minimum((i + 1) * per, last_blk), 0)),
            _full(conv_w.shape),
        ],
        out_specs=[pl.BlockSpec((TM, w // 2), lambda i: (i, 0))] * 2,
        out_shape=[jax.ShapeDtypeStruct((rows, w // 2), BF16)] * 2,
        compiler_params=_cparams("arbitrary"),
        name="short_conv",
    )(pqk, pqk, pqk, conv_w)


def _mlstm_kernel(q_ref, k_ref, v_ref, g_ref, o_ref, ct_ref, n_ref, m_ref, *, d):
    i = pl.program_id(0)
    ln = CHUNK

    @pl.when(i == 0)
    def _():
        ct_ref[...] = jnp.zeros(ct_ref.shape, F32)
        n_ref[...] = jnp.zeros(n_ref.shape, F32)
        m_ref[...] = jnp.zeros(m_ref.shape, F32)

    r = lax.broadcasted_iota(jnp.int32, (ln, ln), 0)
    c = lax.broadcasted_iota(jnp.int32, (ln, ln), 1)
    tri = (c <= r) if d == 0 else (c >= r)
    tri_b = jnp.where(tri, 1.0, 0.0).astype(BF16)
    lane_lo = c < ML_QK_DIM
    row_lo = r < ML_QK_DIM

    g = g_ref[...]
    g_t = g.T
    nt_dims = (((1,), (1,)), ((), ()))
    bc_col = sum(jnp.dot(tri_b, piece, preferred_element_type=F32) for piece in _split3(g))
    bc_row = sum(lax.dot_general(piece, tri_b, nt_dims, preferred_element_type=F32) for piece in _split3(g_t))
    last = ln - 1 if d == 0 else 0

    for pair in range(ML_HEADS // 2):
        q_pair = q_ref[:, pair * LANES:(pair + 1) * LANES]
        k_pair = k_ref[:, pair * LANES:(pair + 1) * LANES]
        k_f = k_pair.astype(F32)
        k_t = k_f.T
        for half in range(2):
            h = 2 * pair + half
            il = 2 * ML_HEADS * d + h
            fl = il + ML_HEADS
            keep_lane = lane_lo if half == 0 else ~lane_lo
            keep_row = row_lo if half == 0 else ~row_lo
            bcol = bc_col[:, fl:fl + 1]
            brow = bc_row[fl:fl + 1, :]
            irow = g_t[il:il + 1, :]
            icol = g[:, il:il + 1]
            btot = bc_col[last:last + 1, fl:fl + 1]
            m_prev = m_ref[h:h + 1, 0:1]

            dmat = jnp.where(tri, bcol - brow + irow, -jnp.inf)
            inter = bcol + m_prev
            mrow = jnp.maximum(inter, jnp.max(dmat, axis=-1, keepdims=True))
            wts = jnp.exp(dmat - mrow)
            a = jnp.exp(inter - mrow)

            qm = jnp.where(keep_lane, q_pair, jnp.zeros_like(q_pair))
            s = lax.dot_general(qm, k_pair, nt_dims, preferred_element_type=F32) * wts
            vh = v_ref[:, h * LANES:(h + 1) * LANES]
            ct = ct_ref[h]
            n_row = n_ref[h]
            num = (jnp.dot(s.astype(BF16), vh, preferred_element_type=F32)
                   + a * jnp.dot(qm, ct.astype(BF16), preferred_element_type=F32))
            den = (jnp.sum(s, axis=-1, keepdims=True)
                   + a * jnp.sum(qm.astype(F32) * n_row, axis=-1, keepdims=True))
            o_ref[:, h * LANES:(h + 1) * LANES] = num / jnp.maximum(jnp.abs(den), jnp.exp(-mrow))

            wv = btot - bcol + icol
            m_loc = jnp.max(wv, axis=0, keepdims=True)
            e = jnp.exp(wv - m_loc)
            ev = (e * vh.astype(F32)).astype(BF16)
            k_tm = jnp.where(keep_row, k_t, 0.0).astype(BF16)
            c_loc = jnp.dot(k_tm, ev, preferred_element_type=F32)
            n_loc = jnp.sum(jnp.where(keep_lane, e * k_f, 0.0), axis=0, keepdims=True)
            m_new = jnp.maximum(btot + m_prev, m_loc)
            decay = jnp.exp(btot + m_prev - m_new)
            gain = jnp.exp(m_loc - m_new)
            ct_ref[h] = decay * ct + gain * c_loc
            n_ref[h] = decay * n_row + gain * n_loc
            m_ref[h:h + 1, :] = jnp.broadcast_to(m_new, (1, LANES))


def _mlstm_direction(q, k, v, gates, d, n_lat_ch):
    rows = q.shape[0]
    nch = rows // CHUNK
    if d == 0:
        order = lambda i: (i + n_lat_ch) % nch
    else:
        order = lambda i: nch - 1 - i
    blk = lambda w: pl.BlockSpec((CHUNK, w), lambda i: (order(i), 0))
    vw = v.shape[1]
    return pl.pallas_call(
        functools.partial(_mlstm_kernel, d=d),
        grid=(nch,),
        in_specs=[blk(q.shape[1]), blk(k.shape[1]), blk(vw), blk(LANES)],
        out_specs=blk(vw),
        out_shape=jax.ShapeDtypeStruct((rows, vw), F32),
        scratch_shapes=[
            pltpu.VMEM((ML_HEADS, LANES, ML_V_DIM), F32),
            pltpu.VMEM((ML_HEADS, 1, LANES), F32),
            pltpu.VMEM((ML_HEADS, LANES), F32),
        ],
        compiler_params=_cparams("arbitrary"),
        name="mlstm_fwd" if d == 0 else "mlstm_bwd",
    )(q, k, v, gates)


def _out_odd_kernel(h_ref, mod_ref, hf_ref, hb_ref, og_ref, ng_ref, w_ref, o_ref):
    hs = hf_ref[...] + hb_ref[...]
    parts = []
    for hh in range(ML_HEADS):
        blk = hs[:, hh * ML_V_DIM:(hh + 1) * ML_V_DIM]
        ms = jnp.mean(blk * blk, axis=-1, keepdims=True)
        parts.append(blk * lax.rsqrt(ms + EPS))
    hn = jnp.concatenate(parts, axis=1) * ng_ref[...]
    gated = (hn * jax.nn.sigmoid(og_ref[...])).astype(BF16)
    y = jnp.dot(gated, w_ref[...], preferred_element_type=F32)
    o_ref[...] = h_ref[...] + mod_ref[0][5:6] * y


def _out_odd(h, mod, hf, hb, og, out_norm_g, w_out, n_tiles, n_lat_t):
    d = h.shape[1]
    vw = hf.shape[1]
    row_tile = lambda w: pl.BlockSpec((TM, w), lambda i: (i, 0))
    return pl.pallas_call(
        _out_odd_kernel,
        grid=(n_tiles,),
        in_specs=[row_tile(d), pl.BlockSpec((1, N_MOD, d), lambda i: (i // n_lat_t, 0, 0)),
                  row_tile(vw), row_tile(vw), row_tile(vw), _full((1, vw)), _full(w_out.shape)],
        out_specs=row_tile(d),
        out_shape=jax.ShapeDtypeStruct((n_tiles * TM, d), F32),
        compiler_params=_cparams("arbitrary"),
        name="odd_out",
    )(h, mod, hf, hb, og, out_norm_g.reshape(1, vw), w_out.astype(BF16))


def kernel(x, c, ctx, c_ctx, mod_w, mod_b, norm_g, ffn_w13, ffn_w2, ab_w_in, ab_gate_norm_g, ab_spatial_w, ab_spatial_b, ab_q_norm_g, ab_k_norm_g, ab_w_out, ml_w_in, ml_conv_w, ml_gate_b, ml_out_norm_g, ml_w_out, final_norm_g):
    b, t_lat, d = x.shape
    ctx_len = ctx.shape[1]
    depth = mod_w.shape[0]
    assert b == 1 and t_lat % TM == 0 and ctx_len % TM == 0 and t_lat % ATT_TK == 0
    n_lat_t = t_lat // TM
    n_all_t = (t_lat + ctx_len) // TM

    mods = _mods(c, c_ctx, mod_w, mod_b)
    h = jnp.concatenate([x[0], ctx[0]], axis=0)
    for l in range(depth):
        keep_ctx = l < depth - 1
        mod = mods[l]
        h = _ffn(h, mod, norm_g[l, 0], ffn_w13[l, 0], ffn_w2[l, 0], n_lat_t, 0)
        n_out_t = n_all_t if keep_ctx else n_lat_t
        if l % 2 == 0:
            e = l // 2
            gated, q, kd0, kd1, va, vb = _even_proj(
                h, mod, norm_g[l, 1], ab_w_in[e], ab_gate_norm_g[e], ab_spatial_w[e], ab_spatial_b[e],
                ab_q_norm_g[e], ab_k_norm_g[e], t_lat, n_lat_t)
            attn = _attention(q, kd0, kd1, va, vb, t_lat, n_lat_t)
            h = _out_even(h, mod, gated, attn, ab_w_out[e], n_out_t, n_lat_t)
        else:
            o = l // 2
            pqk, v, og, gates = _odd_proj(h, mod, norm_g[l, 1], ml_w_in[o], ml_gate_b[o], n_lat_t)
            q, k = _conv(pqk, ml_conv_w[o], n_lat_t)
            hf = _mlstm_direction(q, k, v, gates, 0, t_lat // CHUNK)
            hb = _mlstm_direction(q, k, v, gates, 1, t_lat // CHUNK)
            h = _out_odd(h, mod, hf, hb, og, ml_out_norm_g[o], ml_w_out[o], n_out_t, n_lat_t)
        final_g = final_norm_g if l == depth - 1 else None
        h = _ffn(h, mod, norm_g[l, 2], ffn_w13[l, 1], ffn_w2[l, 1], n_lat_t, 6, final_g)
    return h[:t_lat][None]
```

```python
import functools

import jax
import jax.numpy as jnp
from jax import lax
from jax.experimental import pallas as pl
from jax.experimental.pallas import tpu as pltpu

F32 = jnp.float32
BF16 = jnp.bfloat16

LANES = 128
SUBLANES = 8
VMEM_LIMIT = 56 * 1024 * 1024

GRID_W = 64
CHUNK = 128
EPS = 1e-6
ROPE_THETA = 10000.0
LOG2_E = 1.4426950408889634
N_MOD = 9
A_GROUPS = 8
A_GROUP_DIM = 64
ATT_HEADS = 8
ATT_KV_HEADS = 2
ATT_HEAD_DIM = 64
ML_HEADS = 8
ML_QK_DIM = 64
ML_V_DIM = 128
CONV_WIDTH = 3

TM = 256
ATT_TK = 1024
ATT_LOOKAHEAD = 3
V_ROWS = ATT_HEAD_DIM + 16


def _cparams(*sem):
    return pltpu.CompilerParams(dimension_semantics=sem, vmem_limit_bytes=VMEM_LIMIT)


def _full(shape):
    n = len(shape)
    return pl.BlockSpec(shape, lambda *_: (0,) * n)


def _rms_mod(x, g, shift, scale):
    ms = jnp.mean(x * x, axis=-1, keepdims=True)
    y = x * lax.rsqrt(ms + EPS) * g
    return y * (1 + scale) + shift


def _silu(x):
    return x * jax.nn.sigmoid(x)


def _split3(x):
    x1 = x.astype(BF16)
    r = x - x1.astype(F32)
    x2 = r.astype(BF16)
    x3 = (r - x2.astype(F32)).astype(BF16)
    return x1, x2, x3


def _group_mean_sq(x, bmat):
    sq = x * x
    hi = sq.astype(BF16)
    lo = (sq - hi.astype(F32)).astype(BF16)
    return (jnp.dot(hi, bmat, preferred_element_type=F32)
            + jnp.dot(lo, bmat, preferred_element_type=F32))


def _mod_kernel(c_ref, w_ref, b_ref, o_ref):
    s = _silu(c_ref[...])
    o_ref[0] = jnp.dot(s, w_ref[0], preferred_element_type=F32,
                       precision=lax.Precision.HIGHEST) + b_ref[0]


def _mods(c, c_ctx, mod_w, mod_b):
    depth, d, nd = mod_w.shape
    cvec = jnp.zeros((SUBLANES, d), F32).at[0].set(c[0]).at[1].set(c_ctx)
    out = pl.pallas_call(
        _mod_kernel,
        grid=(depth, nd // d),
        in_specs=[
            pl.BlockSpec((SUBLANES, d), lambda l, j: (0, 0)),
            pl.BlockSpec((1, d, d), lambda l, j: (l, 0, j)),
            pl.BlockSpec((1, 1, d), lambda l, j: (l, 0, j)),
        ],
        out_specs=pl.BlockSpec((1, SUBLANES, d), lambda l, j: (l, 0, j)),
        out_shape=jax.ShapeDtypeStruct((depth, SUBLANES, nd), F32),
        compiler_params=_cparams("arbitrary", "arbitrary"),
        name="modulation",
    )(cvec, mod_w, mod_b.reshape(depth, 1, nd))
    return out[:, :2].reshape(depth, 2, N_MOD, d)


def _ffn_kernel(h_ref, mod_ref, g_ref, w13_ref, w2_ref, *rest, mod_base, final):
    o_ref = rest[-1]
    x = h_ref[...]
    m = mod_ref[0]
    z = _rms_mod(x, g_ref[...], m[mod_base:mod_base + 1], m[mod_base + 1:mod_base + 2]).astype(BF16)
    f = w2_ref.shape[0]
    gu = jnp.dot(z, w13_ref[...], preferred_element_type=F32)
    a = (_silu(gu[:, :f]) * gu[:, f:]).astype(BF16)
    y = jnp.dot(a, w2_ref[...], preferred_element_type=F32)
    out = x + (0.5 * m[mod_base + 2:mod_base + 3]) * y
    if final:
        ms = jnp.mean(out * out, axis=-1, keepdims=True)
        out = out * lax.rsqrt(ms + EPS) * rest[0][...]
    o_ref[...] = out


def _ffn(h, mod, g, w13, w2, n_lat_t, mod_base, final_g=None):
    rows, d = h.shape
    f = w2.shape[0]
    final = final_g is not None
    in_specs = [
        pl.BlockSpec((TM, d), lambda i: (i, 0)),
        pl.BlockSpec((1, N_MOD, d), lambda i: (i // n_lat_t, 0, 0)),
        _full((1, d)),
        _full((d, 2 * f)),
        _full((f, d)),
    ]
    args = [h, mod, g.reshape(1, d), w13.astype(BF16), w2.astype(BF16)]
    if final:
        in_specs.append(_full((1, d)))
        args.append(final_g.reshape(1, d))
    return pl.pallas_call(
        functools.partial(_ffn_kernel, mod_base=mod_base, final=final),
        grid=(rows // TM,),
        in_specs=in_specs,
        out_specs=pl.BlockSpec((TM, d), lambda i: (i, 0)),
        out_shape=jax.ShapeDtypeStruct((rows, d), F32),
        compiler_params=_cparams("arbitrary"),
        name="ffn",
    )(*args)


def _even_proj_kernel(h_ref, mod_ref, g_ref, win_ref, gng_ref, qg_ref, kg_ref, spw_ref, spb_ref,
                      cos_ref, sin_ref, b512_ref, b128_ref,
                      gated_ref, q_ref, kd0_ref, kd1_ref, vt_ref):
    x = h_ref[...]
    m = mod_ref[0]
    z = _rms_mod(x, g_ref[...], m[3:4], m[4:5]).astype(BF16)
    p = jnp.dot(z, win_ref[...], preferred_element_type=F32)
    aw = A_GROUPS * A_GROUP_DIM
    qw = ATT_HEADS * ATT_HEAD_DIM
    u = jax.nn.gelu(p[:, :aw])
    v = jax.nn.gelu(p[:, aw:2 * aw])
    b512 = b512_ref[...]
    vn = (v * lax.rsqrt(_group_mean_sq(v, b512) + EPS) * gng_ref[...]).astype(BF16)

    lane = lax.broadcasted_iota(jnp.int32, (CHUNK, LANES), 1)
    lo = lane < A_GROUP_DIM
    for c in range(TM // CHUNK):
        r0 = c * CHUNK
        for j in range(aw // LANES):
            vt = vn[r0:r0 + CHUNK, j * LANES:(j + 1) * LANES]
            s0 = jnp.dot(spw_ref[2 * j], vt, preferred_element_type=F32)
            s1 = jnp.dot(spw_ref[2 * j + 1], vt, preferred_element_type=F32)
            s = jnp.where(lo, s0, s1) + spb_ref[:, j * LANES:(j + 1) * LANES]
            gated_ref[r0:r0 + CHUNK, j * LANES:(j + 1) * LANES] = (
                u[r0:r0 + CHUNK, j * LANES:(j + 1) * LANES] * s).astype(BF16)

    o = 2 * aw
    pq = p[:, o:o + qw]
    pk = p[:, o + qw:o + qw + LANES]
    va = p[:, o + qw + LANES:o + qw + 2 * LANES]
    qn = pq * lax.rsqrt(_group_mean_sq(pq, b512) + EPS) * qg_ref[...]
    kn = pk * lax.rsqrt(_group_mean_sq(pk, b128_ref[...]) + EPS) * kg_ref[...]

    cos = cos_ref[...]
    sin = sin_ref[...]
    lane_t = lax.broadcasted_iota(jnp.int32, (TM, LANES), 1)
    first_half = (lane_t % 32) < 16
    lo_t = lane_t < ATT_HEAD_DIM

    def rope(t):
        partner = jnp.where(first_half, pltpu.roll(t, LANES - 16, 1), pltpu.roll(t, 16, 1))
        return t * cos + partner * sin

    scale = ATT_HEAD_DIM ** -0.5 * LOG2_E
    for j in range(qw // LANES):
        q_ref[:, j * LANES:(j + 1) * LANES] = (rope(qn[:, j * LANES:(j + 1) * LANES]) * scale).astype(BF16)
    kr = rope(kn)
    kswap = pltpu.roll(kr, ATT_HEAD_DIM, 1)
    kd0_ref[...] = jnp.where(lo_t, kr, kswap).astype(BF16)
    kd1_ref[...] = jnp.where(lo_t, kswap, kr).astype(BF16)
    va_t = va.T
    sub = lax.broadcasted_iota(jnp.int32, (V_ROWS - ATT_HEAD_DIM, TM), 0)
    ones_rows = jnp.where(sub == 0, 1.0, 0.0).astype(BF16)
    for g in range(ATT_KV_HEADS):
        vt_ref[g, 0, :ATT_HEAD_DIM, :] = va_t[g * ATT_HEAD_DIM:(g + 1) * ATT_HEAD_DIM].astype(BF16)
        vt_ref[g, 0, ATT_HEAD_DIM:, :] = ones_rows


def _group_matrix(width, gs):
    r = jnp.arange(width) // gs
    return jnp.where(r[:, None] == r[None, :], 1.0 / gs, 0.0).astype(BF16)


def _rope_tables(t_lat, rows):
    axis_dim = ATT_HEAD_DIM // 2
    inv_freq = ROPE_THETA ** (-jnp.arange(0, axis_dim, 2, dtype=F32) / axis_dim)
    t = jnp.arange(rows)
    is_lat = (t < t_lat)[:, None]
    row = (t // GRID_W).astype(F32)[:, None]
    col = (t % GRID_W).astype(F32)[:, None]
    lane = jnp.arange(LANES)
    freq = inv_freq[lane % (axis_dim // 2)][None, :]
    use_row = ((lane % ATT_HEAD_DIM) < axis_dim)[None, :]
    ang = jnp.where(is_lat, jnp.where(use_row, row, col) * freq, 0.0)
    sign = jnp.where((lane % axis_dim) < axis_dim // 2, -1.0, 1.0)[None, :]
    return jnp.cos(ang), jnp.sin(ang) * sign


def _even_proj(h, mod, g, w_in, gate_norm_g, sp_w, sp_b, q_norm_g, k_norm_g, t_lat, n_lat_t):
    rows, d = h.shape
    aw = A_GROUPS * A_GROUP_DIM
    qw = ATT_HEADS * ATT_HEAD_DIM
    cos, sin = _rope_tables(t_lat, rows)
    spb = jnp.repeat(sp_b.T, A_GROUP_DIM, axis=1)
    row_tile = lambda w: pl.BlockSpec((TM, w), lambda i: (i, 0))
    out_shape = [jax.ShapeDtypeStruct((rows, aw), BF16), jax.ShapeDtypeStruct((rows, qw), BF16)]
    out_shape += [jax.ShapeDtypeStruct((rows, LANES), BF16)] * 2
    out_shape += [jax.ShapeDtypeStruct((ATT_KV_HEADS, rows // TM, V_ROWS, TM), BF16)]
    vt_spec = pl.BlockSpec((ATT_KV_HEADS, 1, V_ROWS, TM), lambda i: (0, i, 0, 0))
    return pl.pallas_call(
        _even_proj_kernel,
        grid=(rows // TM,),
        in_specs=[
            row_tile(d),
            pl.BlockSpec((1, N_MOD, d), lambda i: (i // n_lat_t, 0, 0)),
            _full((1, d)),
            _full(w_in.shape),
            _full((1, aw)),
            _full((1, qw)),
            _full((1, LANES)),
            _full(sp_w.shape),
            _full((CHUNK, aw)),
            row_tile(LANES),
            row_tile(LANES),
            _full((aw, aw)),
            _full((LANES, LANES)),
        ],
        out_specs=[row_tile(aw), row_tile(qw), row_tile(LANES), row_tile(LANES), vt_spec],
        out_shape=out_shape,
        compiler_params=_cparams("arbitrary"),
        name="even_proj",
    )(h, mod, g.reshape(1, d), w_in.astype(BF16), gate_norm_g.reshape(1, aw),
      jnp.tile(q_norm_g, ATT_HEADS).reshape(1, qw), jnp.tile(k_norm_g, ATT_KV_HEADS).reshape(1, LANES),
      sp_w.astype(BF16), spb, cos, sin, _group_matrix(aw, A_GROUP_DIM), _group_matrix(LANES, ATT_HEAD_DIM))


def _attn_kernel(q_ref, kd0_ref, kd1_ref, vt_ref, o_ref, qs_ref, m_ref, acc_ref, *, n_lat_t, t_lat, ctx_len):
    i = pl.program_id(0)
    tq = q_ref.shape[0]
    lane = lax.broadcasted_iota(jnp.int32, (tq, LANES), 1)
    lo = lane < ATT_HEAD_DIM
    for h in range(ATT_HEADS):
        t = q_ref[:, (h // 2) * LANES:(h // 2 + 1) * LANES]
        qs_ref[h * tq:(h + 1) * tq] = jnp.where(lo if h % 2 == 0 else ~lo, t, jnp.zeros_like(t))

    kd_refs = (kd0_ref, kd1_ref)
    n_streams = ATT_HEADS // 2
    blocks_per_chunk = ATT_TK // TM
    nt_dims = (((1,), (1,)), ((), ()))

    def group_of(s):
        return s // (n_streams // ATT_KV_HEADS)

    def scores(s, k_start, n_blk):
        kc = kd_refs[group_of(s)][pl.ds(k_start, n_blk * TM), :]
        qs = qs_ref[2 * s * tq:(2 * s + 2) * tq]
        return lax.dot_general(kc, qs, nt_dims, preferred_element_type=F32)

    def softmax_pv(s, st, n_blk, blk0):
        m_old = m_ref[s]
        m_new = jnp.maximum(m_old, jnp.max(st, axis=0, keepdims=True))
        alpha = jnp.exp2(m_old - m_new)
        pt = jnp.exp2(st - m_new).astype(BF16)
        pv = None
        for j in range(n_blk):
            part = jnp.dot(vt_ref[group_of(s), blk0 + j], pt[j * TM:(j + 1) * TM], preferred_element_type=F32)
            pv = part if pv is None else pv + part
        m_ref[s] = m_new
        acc_ref[s] = alpha * acc_ref[s] + pv

    def step(k_start, n_blk, blk0):
        pending = [scores(s, k_start, n_blk) for s in range(min(ATT_LOOKAHEAD, n_streams))]
        for s in range(n_streams):
            st = pending.pop(0)
            if s + ATT_LOOKAHEAD < n_streams:
                pending.append(scores(s + ATT_LOOKAHEAD, k_start, n_blk))
            softmax_pv(s, st, n_blk, blk0)

    def body(c, carry):
        step(pl.multiple_of(c * ATT_TK, ATT_TK), blocks_per_chunk, c * blocks_per_chunk)
        return carry

    m_ref[...] = jnp.full(m_ref.shape, -jnp.inf, F32)
    acc_ref[...] = jnp.zeros(acc_ref.shape, F32)
    n_full = jnp.where(i < n_lat_t, t_lat // ATT_TK, 0)
    lax.fori_loop(0, n_full, body, 0)
    step(t_lat, ctx_len // TM, t_lat // TM)

    for s in range(n_streams):
        acc = acc_ref[s]
        o_t = acc[:ATT_HEAD_DIM] * (1.0 / acc[ATT_HEAD_DIM:ATT_HEAD_DIM + 1])
        pair = jnp.concatenate([o_t[:, :tq], o_t[:, tq:]], axis=0)
        o_ref[:, s * LANES:(s + 1) * LANES] = pair.T.astype(BF16)


def _attention(q, kd0, kd1, vt, t_lat, n_lat_t):
    rows, qw = q.shape
    ctx_len = rows - t_lat
    return pl.pallas_call(
        functools.partial(_attn_kernel, n_lat_t=n_lat_t, t_lat=t_lat, ctx_len=ctx_len),
        grid=(rows // TM,),
        in_specs=[pl.BlockSpec((TM, qw), lambda i: (i, 0)), _full((rows, LANES)), _full((rows, LANES)),
                  _full(vt.shape)],
        out_specs=pl.BlockSpec((TM, qw), lambda i: (i, 0)),
        out_shape=jax.ShapeDtypeStruct((rows, qw), BF16),
        scratch_shapes=[
            pltpu.VMEM((ATT_HEADS * TM, LANES), BF16),
            pltpu.VMEM((ATT_HEADS // 2, 1, 2 * TM), F32),
            pltpu.VMEM((ATT_HEADS // 2, V_ROWS, 2 * TM), F32),
        ],
        compiler_params=_cparams("arbitrary"),
        name="gqa_attention",
    )(q, kd0, kd1, vt)


def _out_even_kernel(h_ref, mod_ref, a_ref, b_ref, w_ref, o_ref):
    aw = a_ref.shape[1]
    y = (jnp.dot(a_ref[...], w_ref[:aw], preferred_element_type=F32)
         + jnp.dot(b_ref[...], w_ref[aw:], preferred_element_type=F32))
    o_ref[...] = h_ref[...] + mod_ref[0][5:6] * y


def _out_even(h, mod, gated, attn, w_out, n_tiles, n_lat_t):
    d = h.shape[1]
    row_tile = lambda w: pl.BlockSpec((TM, w), lambda i: (i, 0))
    return pl.pallas_call(
        _out_even_kernel,
        grid=(n_tiles,),
        in_specs=[row_tile(d), pl.BlockSpec((1, N_MOD, d), lambda i: (i // n_lat_t, 0, 0)),
                  row_tile(gated.shape[1]), row_tile(attn.shape[1]), _full(w_out.shape)],
        out_specs=row_tile(d),
        out_shape=jax.ShapeDtypeStruct((n_tiles * TM, d), F32),
        compiler_params=_cparams("arbitrary"),
        name="even_out",
    )(h, mod, gated, attn, w_out.astype(BF16))


def _odd_proj_kernel(h_ref, mod_ref, g_ref, win_ref, gb_ref, pqk_ref, v_ref, o_ref, gates_ref):
    x = h_ref[...]
    m = mod_ref[0]
    z = _rms_mod(x, g_ref[...], m[3:4], m[4:5]).astype(BF16)
    p = jnp.dot(z, win_ref[...], preferred_element_type=F32)
    qk = 2 * ML_HEADS * ML_QK_DIM
    vw = ML_HEADS * ML_V_DIM
    pqk_ref[...] = p[:, :qk]
    v_ref[...] = p[:, qk:qk + vw].astype(BF16)
    o_ref[...] = p[:, qk + vw:qk + 2 * vw]
    gts = p[:, qk + 2 * vw:] + gb_ref[...]
    lane = lax.broadcasted_iota(jnp.int32, gts.shape, 1)
    is_f = (lane < 4 * ML_HEADS) & ((lane // ML_HEADS) % 2 == 1)
    log_sig = jnp.minimum(gts, 0.0) - jnp.log1p(jnp.exp(-jnp.abs(gts)))
    gates_ref[...] = jnp.where(is_f, log_sig, gts)


def _odd_proj(h, mod, g, w_in, gate_b, n_lat_t):
    rows, d = h.shape
    qk = 2 * ML_HEADS * ML_QK_DIM
    vw = ML_HEADS * ML_V_DIM
    n_gate = 4 * ML_HEADS
    w_pad = jnp.pad(w_in, ((0, 0), (0, LANES - n_gate))).astype(BF16)
    gb = jnp.pad(gate_b, (0, LANES - n_gate)).reshape(1, LANES)
    row_tile = lambda w: pl.BlockSpec((TM, w), lambda i: (i, 0))
    return pl.pallas_call(
        _odd_proj_kernel,
        grid=(rows // TM,),
        in_specs=[row_tile(d), pl.BlockSpec((1, N_MOD, d), lambda i: (i // n_lat_t, 0, 0)),
                  _full((1, d)), _full(w_pad.shape), _full((1, LANES))],
        out_specs=[row_tile(qk), row_tile(vw), row_tile(vw), row_tile(LANES)],
        out_shape=[jax.ShapeDtypeStruct((rows, qk), F32), jax.ShapeDtypeStruct((rows, vw), BF16),
                   jax.ShapeDtypeStruct((rows, vw), F32), jax.ShapeDtypeStruct((rows, LANES), F32)],
        compiler_params=_cparams("arbitrary"),
        name="odd_proj",
    )(h, mod, g.reshape(1, d), w_pad, gb)


def _conv_kernel(x_ref, prev_ref, next_ref, w_ref, q_ref, k_ref, *, n_lat_t):
    i = pl.program_id(0)
    nt = pl.num_programs(0)
    x = x_ref[...]
    tm = x.shape[0]
    first = (i == 0) | (i == n_lat_t)
    last = (i == n_lat_t - 1) | (i == nt - 1)
    prev_row = jnp.where(first, 0.0, prev_ref[SUBLANES - 1:SUBLANES, :])
    next_row = jnp.where(last, 0.0, next_ref[0:1, :])
    row = lax.broadcasted_iota(jnp.int32, x.shape, 0)
    xm = jnp.where(row == 0, prev_row, pltpu.roll(x, 1, 0))
    xp = jnp.where(row == tm - 1, next_row, pltpu.roll(x, tm - 1, 0))
    y = xm * w_ref[0:1, :] + x * w_ref[1:2, :] + xp * w_ref[2:3, :]
    y = _silu(y)
    qk = ML_HEADS * ML_QK_DIM
    q_ref[...] = (y[:, :qk] * (ML_QK_DIM ** -0.5)).astype(BF16)
    k_ref[...] = y[:, qk:].astype(BF16)


def _conv(pqk, conv_w, n_lat_t):
    rows, w = pqk.shape
    per = TM // SUBLANES
    last_blk = rows // SUBLANES - 1
    return pl.pallas_call(
        functools.partial(_conv_kernel, n_lat_t=n_lat_t),
        grid=(rows // TM,),
        in_specs=[
            pl.BlockSpec((TM, w), lambda i: (i, 0)),
            pl.BlockSpec((SUBLANES, w), lambda i: (jnp.maximum(i * per - 1, 0), 0)),
            pl.BlockSpec((SUBLANES, w), lambda i: (jnp.minimum((i + 1) * per, last_blk), 0)),
            _full(conv_w.shape),
        ],
        out_specs=[pl.BlockSpec((TM, w // 2), lambda i: (i, 0))] * 2,
        out_shape=[jax.ShapeDtypeStruct((rows, w // 2), BF16)] * 2,
        compiler_params=_cparams("arbitrary"),
        name="short_conv",
    )(pqk, pqk, pqk, conv_w)


def _mlstm_kernel(q_ref, k_ref, v_ref, g_ref, o_ref, ct_ref, n_ref, m_ref, *, d):
    i = pl.program_id(0)
    ln = CHUNK

    @pl.when(i == 0)
    def _():
        ct_ref[...] = jnp.zeros(ct_ref.shape, F32)
        n_ref[...] = jnp.zeros(n_ref.shape, F32)
        m_ref[...] = jnp.zeros(m_ref.shape, F32)

    r = lax.broadcasted_iota(jnp.int32, (ln, ln), 0)
    c = lax.broadcasted_iota(jnp.int32, (ln, ln), 1)
    tri = (c <= r) if d == 0 else (c >= r)
    tri_b = jnp.where(tri, 1.0, 0.0).astype(BF16)
    lane_lo = c < ML_QK_DIM
    row_lo = r < ML_QK_DIM

    g = g_ref[...]
    g_t = g.T
    nt_dims = (((1,), (1,)), ((), ()))
    bc_col = sum(jnp.dot(tri_b, piece, preferred_element_type=F32) for piece in _split3(g))
    bc_row = sum(lax.dot_general(piece, tri_b, nt_dims, preferred_element_type=F32) for piece in _split3(g_t))
    last = ln - 1 if d == 0 else 0

    for pair in range(ML_HEADS // 2):
        q_pair = q_ref[:, pair * LANES:(pair + 1) * LANES]
        k_pair = k_ref[:, pair * LANES:(pair + 1) * LANES]
        k_f = k_pair.astype(F32)
        k_t = k_f.T
        for half in range(2):
            h = 2 * pair + half
            il = 2 * ML_HEADS * d + h
            fl = il + ML_HEADS
            keep_lane = lane_lo if half == 0 else ~lane_lo
            keep_row = row_lo if half == 0 else ~row_lo
            bcol = bc_col[:, fl:fl + 1]
            brow = bc_row[fl:fl + 1, :]
            irow = g_t[il:il + 1, :]
            icol = g[:, il:il + 1]
            btot = bc_col[last:last + 1, fl:fl + 1]
            m_prev = m_ref[h:h + 1, 0:1]

            dmat = jnp.where(tri, bcol - brow + irow, -jnp.inf)
            inter = bcol + m_prev
            mrow = jnp.maximum(inter, jnp.max(dmat, axis=-1, keepdims=True))
            wts = jnp.exp(dmat - mrow)
            a = jnp.exp(inter - mrow)

            qm = jnp.where(keep_lane, q_pair, jnp.zeros_like(q_pair))
            s = lax.dot_general(qm, k_pair, nt_dims, preferred_element_type=F32) * wts
            vh = v_ref[:, h * LANES:(h + 1) * LANES]
            ct = ct_ref[h]
            n_row = n_ref[h]
            num = (jnp.dot(s.astype(BF16), vh, preferred_element_type=F32)
                   + a * jnp.dot(qm, ct.astype(BF16), preferred_element_type=F32))
            den = (jnp.sum(s, axis=-1, keepdims=True)
                   + a * jnp.sum(qm.astype(F32) * n_row, axis=-1, keepdims=True))
            o_ref[:, h * LANES:(h + 1) * LANES] = num / jnp.maximum(jnp.abs(den), jnp.exp(-mrow))

            wv = btot - bcol + icol
            m_loc = jnp.max(wv, axis=0, keepdims=True)
            e = jnp.exp(wv - m_loc)
            ev = (e * vh.astype(F32)).astype(BF16)
            k_tm = jnp.where(keep_row, k_t, 0.0).astype(BF16)
            c_loc = jnp.dot(k_tm, ev, preferred_element_type=F32)
            n_loc = jnp.sum(jnp.where(keep_lane, e * k_f, 0.0), axis=0, keepdims=True)
            m_new = jnp.maximum(btot + m_prev, m_loc)
            decay = jnp.exp(btot + m_prev - m_new)
            gain = jnp.exp(m_loc - m_new)
            ct_ref[h] = decay * ct + gain * c_loc
            n_ref[h] = decay * n_row + gain * n_loc
            m_ref[h:h + 1, :] = jnp.broadcast_to(m_new, (1, LANES))


def _mlstm_direction(q, k, v, gates, d, n_lat_ch):
    rows = q.shape[0]
    nch = rows // CHUNK
    if d == 0:
        order = lambda i: (i + n_lat_ch) % nch
    else:
        order = lambda i: nch - 1 - i
    blk = lambda w: pl.BlockSpec((CHUNK, w), lambda i: (order(i), 0))
    vw = v.shape[1]
    return pl.pallas_call(
        functools.partial(_mlstm_kernel, d=d),
        grid=(nch,),
        in_specs=[blk(q.shape[1]), blk(k.shape[1]), blk(vw), blk(LANES)],
        out_specs=blk(vw),
        out_shape=jax.ShapeDtypeStruct((rows, vw), F32),
        scratch_shapes=[
            pltpu.VMEM((ML_HEADS, LANES, ML_V_DIM), F32),
            pltpu.VMEM((ML_HEADS, 1, LANES), F32),
            pltpu.VMEM((ML_HEADS, LANES), F32),
        ],
        compiler_params=_cparams("arbitrary"),
        name="mlstm_fwd" if d == 0 else "mlstm_bwd",
    )(q, k, v, gates)


def _out_odd_kernel(h_ref, mod_ref, hf_ref, hb_ref, og_ref, ng_ref, w_ref, o_ref):
    hs = hf_ref[...] + hb_ref[...]
    parts = []
    for hh in range(ML_HEADS):
        blk = hs[:, hh * ML_V_DIM:(hh + 1) * ML_V_DIM]
        ms = jnp.mean(blk * blk, axis=-1, keepdims=True)
        parts.append(blk * lax.rsqrt(ms + EPS))
    hn = jnp.concatenate(parts, axis=1) * ng_ref[...]
    gated = (hn * jax.nn.sigmoid(og_ref[...])).astype(BF16)
    y = jnp.dot(gated, w_ref[...], preferred_element_type=F32)
    o_ref[...] = h_ref[...] + mod_ref[0][5:6] * y


def _out_odd(h, mod, hf, hb, og, out_norm_g, w_out, n_tiles, n_lat_t):
    d = h.shape[1]
    vw = hf.shape[1]
    row_tile = lambda w: pl.BlockSpec((TM, w), lambda i: (i, 0))
    return pl.pallas_call(
        _out_odd_kernel,
        grid=(n_tiles,),
        in_specs=[row_tile(d), pl.BlockSpec((1, N_MOD, d), lambda i: (i // n_lat_t, 0, 0)),
                  row_tile(vw), row_tile(vw), row_tile(vw), _full((1, vw)), _full(w_out.shape)],
        out_specs=row_tile(d),
        out_shape=jax.ShapeDtypeStruct((n_tiles * TM, d), F32),
        compiler_params=_cparams("arbitrary"),
        name="odd_out",
    )(h, mod, hf, hb, og, out_norm_g.reshape(1, vw), w_out.astype(BF16))


def kernel(x, c, ctx, c_ctx, mod_w, mod_b, norm_g, ffn_w13, ffn_w2, ab_w_in, ab_gate_norm_g, ab_spatial_w, ab_spatial_b, ab_q_norm_g, ab_k_norm_g, ab_w_out, ml_w_in, ml_conv_w, ml_gate_b, ml_out_norm_g, ml_w_out, final_norm_g):
    b, t_lat, d = x.shape
    ctx_len = ctx.shape[1]
    depth = mod_w.shape[0]
    assert b == 1 and t_lat % TM == 0 and ctx_len % TM == 0 and t_lat % ATT_TK == 0
    n_lat_t = t_lat // TM
    n_all_t = (t_lat + ctx_len) // TM

    mods = _mods(c, c_ctx, mod_w, mod_b)
    h = jnp.concatenate([x[0], ctx[0]], axis=0)
    for l in range(depth):
        keep_ctx = l < depth - 1
        mod = mods[l]
        h = _ffn(h, mod, norm_g[l, 0], ffn_w13[l, 0], ffn_w2[l, 0], n_lat_t, 0)
        n_out_t = n_all_t if keep_ctx else n_lat_t
        if l % 2 == 0:
            e = l // 2
            gated, q, kd0, kd1, vt = _even_proj(
                h, mod, norm_g[l, 1], ab_w_in[e], ab_gate_norm_g[e], ab_spatial_w[e], ab_spatial_b[e],
                ab_q_norm_g[e], ab_k_norm_g[e], t_lat, n_lat_t)
            attn = _attention(q, kd0, kd1, vt, t_lat, n_lat_t)
            h = _out_even(h, mod, gated, attn, ab_w_out[e], n_out_t, n_lat_t)
        else:
            o = l // 2
            pqk, v, og, gates = _odd_proj(h, mod, norm_g[l, 1], ml_w_in[o], ml_gate_b[o], n_lat_t)
            q, k = _conv(pqk, ml_conv_w[o], n_lat_t)
            hf = _mlstm_direction(q, k, v, gates, 0, t_lat // CHUNK)
            hb = _mlstm_direction(q, k, v, gates, 1, t_lat // CHUNK)
            h = _out_odd(h, mod, hf, hb, og, ml_out_norm_g[o], ml_w_out[o], n_out_t, n_lat_t)
        final_g = final_norm_g if l == depth - 1 else None
        h = _ffn(h, mod, norm_g[l, 2], ffn_w13[l, 1], ffn_w2[l, 1], n_lat_t, 6, final_g)
    return h[:t_lat][None]
```

```python
import functools

import jax
import jax.numpy as jnp
from jax import lax
from jax.experimental import pallas as pl
from jax.experimental.pallas import tpu as pltpu

F32 = jnp.float32
BF16 = jnp.bfloat16

LANES = 128
SUBLANES = 8
VMEM_LIMIT = 56 * 1024 * 1024

GRID_W = 64
CHUNK = 128
EPS = 1e-6
ROPE_THETA = 10000.0
LOG2_E = 1.4426950408889634
N_MOD = 9
A_GROUPS = 8
A_GROUP_DIM = 64
ATT_HEADS = 8
ATT_KV_HEADS = 2
ATT_HEAD_DIM = 64
ML_HEADS = 8
ML_QK_DIM = 64
ML_V_DIM = 128
CONV_WIDTH = 3

TM = 256
ATT_TK = 1024
ATT_LOOKAHEAD = 3
ATT_LOOKAHEAD_BOUNDED = 1
ATT_SCORE_BOUND = 60.0
V_ROWS = ATT_HEAD_DIM + 16


def _cparams(*sem):
    return pltpu.CompilerParams(dimension_semantics=sem, vmem_limit_bytes=VMEM_LIMIT)


def _full(shape):
    n = len(shape)
    return pl.BlockSpec(shape, lambda *_: (0,) * n)


def _rms_mod(x, g, shift, scale):
    ms = jnp.mean(x * x, axis=-1, keepdims=True)
    y = x * lax.rsqrt(ms + EPS) * g
    return y * (1 + scale) + shift


def _silu(x):
    return x * jax.nn.sigmoid(x)


def _split3(x):
    x1 = x.astype(BF16)
    r = x - x1.astype(F32)
    x2 = r.astype(BF16)
    x3 = (r - x2.astype(F32)).astype(BF16)
    return x1, x2, x3


def _group_mean_sq(x, bmat):
    sq = x * x
    hi = sq.astype(BF16)
    lo = (sq - hi.astype(F32)).astype(BF16)
    return (jnp.dot(hi, bmat, preferred_element_type=F32)
            + jnp.dot(lo, bmat, preferred_element_type=F32))


def _mod_kernel(c_ref, w_ref, b_ref, o_ref):
    s = _silu(c_ref[...])
    o_ref[0] = jnp.dot(s, w_ref[0], preferred_element_type=F32,
                       precision=lax.Precision.HIGHEST) + b_ref[0]


def _mods(c, c_ctx, mod_w, mod_b):
    depth, d, nd = mod_w.shape
    cvec = jnp.zeros((SUBLANES, d), F32).at[0].set(c[0]).at[1].set(c_ctx)
    out = pl.pallas_call(
        _mod_kernel,
        grid=(depth, nd // d),
        in_specs=[
            pl.BlockSpec((SUBLANES, d), lambda l, j: (0, 0)),
            pl.BlockSpec((1, d, d), lambda l, j: (l, 0, j)),
            pl.BlockSpec((1, 1, d), lambda l, j: (l, 0, j)),
        ],
        out_specs=pl.BlockSpec((1, SUBLANES, d), lambda l, j: (l, 0, j)),
        out_shape=jax.ShapeDtypeStruct((depth, SUBLANES, nd), F32),
        compiler_params=_cparams("arbitrary", "arbitrary"),
        name="modulation",
    )(cvec, mod_w, mod_b.reshape(depth, 1, nd))
    return out[:, :2].reshape(depth, 2, N_MOD, d)


def _ffn_kernel(h_ref, mod_ref, g_ref, w13_ref, w2_ref, *rest, mod_base, final):
    o_ref = rest[-1]
    x = h_ref[...]
    m = mod_ref[0]
    z = _rms_mod(x, g_ref[...], m[mod_base:mod_base + 1], m[mod_base + 1:mod_base + 2]).astype(BF16)
    f = w2_ref.shape[0]
    gu = jnp.dot(z, w13_ref[...], preferred_element_type=F32)
    a = (_silu(gu[:, :f]) * gu[:, f:]).astype(BF16)
    y = jnp.dot(a, w2_ref[...], preferred_element_type=F32)
    out = x + (0.5 * m[mod_base + 2:mod_base + 3]) * y
    if final:
        ms = jnp.mean(out * out, axis=-1, keepdims=True)
        out = out * lax.rsqrt(ms + EPS) * rest[0][...]
    o_ref[...] = out


def _ffn(h, mod, g, w13, w2, n_lat_t, mod_base, final_g=None):
    rows, d = h.shape
    f = w2.shape[0]
    final = final_g is not None
    in_specs = [
        pl.BlockSpec((TM, d), lambda i: (i, 0)),
        pl.BlockSpec((1, N_MOD, d), lambda i: (i // n_lat_t, 0, 0)),
        _full((1, d)),
        _full((d, 2 * f)),
        _full((f, d)),
    ]
    args = [h, mod, g.reshape(1, d), w13.astype(BF16), w2.astype(BF16)]
    if final:
        in_specs.append(_full((1, d)))
        args.append(final_g.reshape(1, d))
    return pl.pallas_call(
        functools.partial(_ffn_kernel, mod_base=mod_base, final=final),
        grid=(rows // TM,),
        in_specs=in_specs,
        out_specs=pl.BlockSpec((TM, d), lambda i: (i, 0)),
        out_shape=jax.ShapeDtypeStruct((rows, d), F32),
        compiler_params=_cparams("arbitrary"),
        name="ffn",
    )(*args)


def _even_proj_kernel(h_ref, mod_ref, g_ref, win_ref, gng_ref, qg_ref, kg_ref, spw_ref, spb_ref,
                      cos_ref, sin_ref, b512_ref, b128_ref,
                      gated_ref, q_ref, kd0_ref, kd1_ref, vt_ref):
    x = h_ref[...]
    m = mod_ref[0]
    z = _rms_mod(x, g_ref[...], m[3:4], m[4:5]).astype(BF16)
    p = jnp.dot(z, win_ref[...], preferred_element_type=F32)
    aw = A_GROUPS * A_GROUP_DIM
    qw = ATT_HEADS * ATT_HEAD_DIM
    u = jax.nn.gelu(p[:, :aw])
    v = jax.nn.gelu(p[:, aw:2 * aw])
    b512 = b512_ref[...]
    vn = (v * lax.rsqrt(_group_mean_sq(v, b512) + EPS) * gng_ref[...]).astype(BF16)

    lane = lax.broadcasted_iota(jnp.int32, (CHUNK, LANES), 1)
    lo = lane < A_GROUP_DIM
    for c in range(TM // CHUNK):
        r0 = c * CHUNK
        for j in range(aw // LANES):
            vt = vn[r0:r0 + CHUNK, j * LANES:(j + 1) * LANES]
            s0 = jnp.dot(spw_ref[2 * j], vt, preferred_element_type=F32)
            s1 = jnp.dot(spw_ref[2 * j + 1], vt, preferred_element_type=F32)
            s = jnp.where(lo, s0, s1) + spb_ref[:, j * LANES:(j + 1) * LANES]
            gated_ref[r0:r0 + CHUNK, j * LANES:(j + 1) * LANES] = (
                u[r0:r0 + CHUNK, j * LANES:(j + 1) * LANES] * s).astype(BF16)

    o = 2 * aw
    pq = p[:, o:o + qw]
    pk = p[:, o + qw:o + qw + LANES]
    va = p[:, o + qw + LANES:o + qw + 2 * LANES]
    qn = pq * lax.rsqrt(_group_mean_sq(pq, b512) + EPS) * qg_ref[...]
    kn = pk * lax.rsqrt(_group_mean_sq(pk, b128_ref[...]) + EPS) * kg_ref[...]

    cos = cos_ref[...]
    sin = sin_ref[...]
    lane_t = lax.broadcasted_iota(jnp.int32, (TM, LANES), 1)
    first_half = (lane_t % 32) < 16
    lo_t = lane_t < ATT_HEAD_DIM

    def rope(t):
        partner = jnp.where(first_half, pltpu.roll(t, LANES - 16, 1), pltpu.roll(t, 16, 1))
        return t * cos + partner * sin

    scale = ATT_HEAD_DIM ** -0.5 * LOG2_E
    for j in range(qw // LANES):
        q_ref[:, j * LANES:(j + 1) * LANES] = (rope(qn[:, j * LANES:(j + 1) * LANES]) * scale).astype(BF16)
    kr = rope(kn)
    kswap = pltpu.roll(kr, ATT_HEAD_DIM, 1)
    kd0_ref[...] = jnp.where(lo_t, kr, kswap).astype(BF16)
    kd1_ref[...] = jnp.where(lo_t, kswap, kr).astype(BF16)
    va_t = va.T
    sub = lax.broadcasted_iota(jnp.int32, (V_ROWS - ATT_HEAD_DIM, TM), 0)
    ones_rows = jnp.where(sub == 0, 1.0, 0.0).astype(BF16)
    for g in range(ATT_KV_HEADS):
        vt_ref[g, 0, :ATT_HEAD_DIM, :] = va_t[g * ATT_HEAD_DIM:(g + 1) * ATT_HEAD_DIM].astype(BF16)
        vt_ref[g, 0, ATT_HEAD_DIM:, :] = ones_rows


def _group_matrix(width, gs):
    r = jnp.arange(width) // gs
    return jnp.where(r[:, None] == r[None, :], 1.0 / gs, 0.0).astype(BF16)


def _rope_tables(t_lat, rows):
    axis_dim = ATT_HEAD_DIM // 2
    inv_freq = ROPE_THETA ** (-jnp.arange(0, axis_dim, 2, dtype=F32) / axis_dim)
    t = jnp.arange(rows)
    is_lat = (t < t_lat)[:, None]
    row = (t // GRID_W).astype(F32)[:, None]
    col = (t % GRID_W).astype(F32)[:, None]
    lane = jnp.arange(LANES)
    freq = inv_freq[lane % (axis_dim // 2)][None, :]
    use_row = ((lane % ATT_HEAD_DIM) < axis_dim)[None, :]
    ang = jnp.where(is_lat, jnp.where(use_row, row, col) * freq, 0.0)
    sign = jnp.where((lane % axis_dim) < axis_dim // 2, -1.0, 1.0)[None, :]
    return jnp.cos(ang), jnp.sin(ang) * sign


def _even_proj(h, mod, g, w_in, gate_norm_g, sp_w, sp_b, q_norm_g, k_norm_g, t_lat, n_lat_t):
    rows, d = h.shape
    aw = A_GROUPS * A_GROUP_DIM
    qw = ATT_HEADS * ATT_HEAD_DIM
    cos, sin = _rope_tables(t_lat, rows)
    spb = jnp.repeat(sp_b.T, A_GROUP_DIM, axis=1)
    row_tile = lambda w: pl.BlockSpec((TM, w), lambda i: (i, 0))
    out_shape = [jax.ShapeDtypeStruct((rows, aw), BF16), jax.ShapeDtypeStruct((rows, qw), BF16)]
    out_shape += [jax.ShapeDtypeStruct((rows, LANES), BF16)] * 2
    out_shape += [jax.ShapeDtypeStruct((ATT_KV_HEADS, rows // TM, V_ROWS, TM), BF16)]
    vt_spec = pl.BlockSpec((ATT_KV_HEADS, 1, V_ROWS, TM), lambda i: (0, i, 0, 0))
    return pl.pallas_call(
        _even_proj_kernel,
        grid=(rows // TM,),
        in_specs=[
            row_tile(d),
            pl.BlockSpec((1, N_MOD, d), lambda i: (i // n_lat_t, 0, 0)),
            _full((1, d)),
            _full(w_in.shape),
            _full((1, aw)),
            _full((1, qw)),
            _full((1, LANES)),
            _full(sp_w.shape),
            _full((CHUNK, aw)),
            row_tile(LANES),
            row_tile(LANES),
            _full((aw, aw)),
            _full((LANES, LANES)),
        ],
        out_specs=[row_tile(aw), row_tile(qw), row_tile(LANES), row_tile(LANES), vt_spec],
        out_shape=out_shape,
        compiler_params=_cparams("arbitrary"),
        name="even_proj",
    )(h, mod, g.reshape(1, d), w_in.astype(BF16), gate_norm_g.reshape(1, aw),
      jnp.tile(q_norm_g, ATT_HEADS).reshape(1, qw), jnp.tile(k_norm_g, ATT_KV_HEADS).reshape(1, LANES),
      sp_w.astype(BF16), spb, cos, sin, _group_matrix(aw, A_GROUP_DIM), _group_matrix(LANES, ATT_HEAD_DIM))


def _attn_kernel(bounded_ref, q_ref, kd0_ref, kd1_ref, vt_ref, o_ref, qs_ref, m_ref, acc_ref,
                 *, n_lat_t, t_lat, ctx_len):
    i = pl.program_id(0)
    tq = q_ref.shape[0]
    lane = lax.broadcasted_iota(jnp.int32, (tq, LANES), 1)
    lo = lane < ATT_HEAD_DIM
    for h in range(ATT_HEADS):
        t = q_ref[:, (h // 2) * LANES:(h // 2 + 1) * LANES]
        qs_ref[h * tq:(h + 1) * tq] = jnp.where(lo if h % 2 == 0 else ~lo, t, jnp.zeros_like(t))

    kd_refs = (kd0_ref, kd1_ref)
    n_streams = ATT_HEADS // 2
    blocks_per_chunk = ATT_TK // TM
    nt_dims = (((1,), (1,)), ((), ()))

    def group_of(s):
        return s // (n_streams // ATT_KV_HEADS)

    def scores(s, k_start, n_blk):
        kc = kd_refs[group_of(s)][pl.ds(k_start, n_blk * TM), :]
        qs = qs_ref[2 * s * tq:(2 * s + 2) * tq]
        return lax.dot_general(kc, qs, nt_dims, preferred_element_type=F32)

    def p_times_v(s, pt, n_blk, blk0):
        pv = None
        for j in range(n_blk):
            part = jnp.dot(vt_ref[group_of(s), blk0 + j], pt[j * TM:(j + 1) * TM], preferred_element_type=F32)
            pv = part if pv is None else pv + part
        return pv

    def softmax_pv_online(s, st, n_blk, blk0):
        m_old = m_ref[s]
        m_new = jnp.maximum(m_old, jnp.max(st, axis=0, keepdims=True))
        alpha = jnp.exp2(m_old - m_new)
        pt = jnp.exp2(st - m_new).astype(BF16)
        m_ref[s] = m_new
        acc_ref[s] = alpha * acc_ref[s] + p_times_v(s, pt, n_blk, blk0)

    def softmax_pv_bounded(s, st, n_blk, blk0):
        pt = jnp.exp2(st).astype(BF16)
        acc_ref[s] += p_times_v(s, pt, n_blk, blk0)

    def sweep(softmax_pv, lookahead):
        def step(k_start, n_blk, blk0):
            pending = [scores(s, k_start, n_blk) for s in range(min(lookahead, n_streams))]
            for s in range(n_streams):
                st = pending.pop(0)
                if s + lookahead < n_streams:
                    pending.append(scores(s + lookahead, k_start, n_blk))
                softmax_pv(s, st, n_blk, blk0)

        def body(c, carry):
            step(pl.multiple_of(c * ATT_TK, ATT_TK), blocks_per_chunk, c * blocks_per_chunk)
            return carry

        n_full = jnp.where(i < n_lat_t, t_lat // ATT_TK, 0)
        lax.fori_loop(0, n_full, body, 0)
        step(t_lat, ctx_len // TM, t_lat // TM)

    acc_ref[...] = jnp.zeros(acc_ref.shape, F32)
    bounded = bounded_ref[0] == 1

    @pl.when(bounded)
    def _():
        sweep(softmax_pv_bounded, ATT_LOOKAHEAD_BOUNDED)

    @pl.when(jnp.logical_not(bounded))
    def _():
        m_ref[...] = jnp.full(m_ref.shape, -jnp.inf, F32)
        sweep(softmax_pv_online, ATT_LOOKAHEAD)

    for s in range(n_streams):
        acc = acc_ref[s]
        o_t = acc[:ATT_HEAD_DIM] * (1.0 / acc[ATT_HEAD_DIM:ATT_HEAD_DIM + 1])
        pair = jnp.concatenate([o_t[:, :tq], o_t[:, tq:]], axis=0)
        o_ref[:, s * LANES:(s + 1) * LANES] = pair.T.astype(BF16)


def _attention(q, kd0, kd1, vt, q_norm_g, k_norm_g, t_lat, n_lat_t):
    rows, qw = q.shape
    ctx_len = rows - t_lat
    score_bound = (ATT_HEAD_DIM ** 0.5 * LOG2_E) * jnp.max(jnp.abs(q_norm_g)) * jnp.max(jnp.abs(k_norm_g))
    bounded = (score_bound <= ATT_SCORE_BOUND).astype(jnp.int32).reshape(1)
    return pl.pallas_call(
        functools.partial(_attn_kernel, n_lat_t=n_lat_t, t_lat=t_lat, ctx_len=ctx_len),
        grid=(rows // TM,),
        in_specs=[pl.BlockSpec(memory_space=pltpu.SMEM),
                  pl.BlockSpec((TM, qw), lambda i: (i, 0)), _full((rows, LANES)), _full((rows, LANES)),
                  _full(vt.shape)],
        out_specs=pl.BlockSpec((TM, qw), lambda i: (i, 0)),
        out_shape=jax.ShapeDtypeStruct((rows, qw), BF16),
        scratch_shapes=[
            pltpu.VMEM((ATT_HEADS * TM, LANES), BF16),
            pltpu.VMEM((ATT_HEADS // 2, 1, 2 * TM), F32),
            pltpu.VMEM((ATT_HEADS // 2, V_ROWS, 2 * TM), F32),
        ],
        compiler_params=_cparams("arbitrary"),
        name="gqa_attention",
    )(bounded, q, kd0, kd1, vt)


def _out_even_kernel(h_ref, mod_ref, a_ref, b_ref, w_ref, o_ref):
    aw = a_ref.shape[1]
    y = (jnp.dot(a_ref[...], w_ref[:aw], preferred_element_type=F32)
         + jnp.dot(b_ref[...], w_ref[aw:], preferred_element_type=F32))
    o_ref[...] = h_ref[...] + mod_ref[0][5:6] * y


def _out_even(h, mod, gated, attn, w_out, n_tiles, n_lat_t):
    d = h.shape[1]
    row_tile = lambda w: pl.BlockSpec((TM, w), lambda i: (i, 0))
    return pl.pallas_call(
        _out_even_kernel,
        grid=(n_tiles,),
        in_specs=[row_tile(d), pl.BlockSpec((1, N_MOD, d), lambda i: (i // n_lat_t, 0, 0)),
                  row_tile(gated.shape[1]), row_tile(attn.shape[1]), _full(w_out.shape)],
        out_specs=row_tile(d),
        out_shape=jax.ShapeDtypeStruct((n_tiles * TM, d), F32),
        compiler_params=_cparams("arbitrary"),
        name="even_out",
    )(h, mod, gated, attn, w_out.astype(BF16))


def _odd_proj_kernel(h_ref, mod_ref, g_ref, win_ref, gb_ref, pqk_ref, v_ref, o_ref, gates_ref):
    x = h_ref[...]
    m = mod_ref[0]
    z = _rms_mod(x, g_ref[...], m[3:4], m[4:5]).astype(BF16)
    p = jnp.dot(z, win_ref[...], preferred_element_type=F32)
    qk = 2 * ML_HEADS * ML_QK_DIM
    vw = ML_HEADS * ML_V_DIM
    pqk_ref[...] = p[:, :qk]
    v_ref[...] = p[:, qk:qk + vw].astype(BF16)
    o_ref[...] = p[:, qk + vw:qk + 2 * vw]
    gts = p[:, qk + 2 * vw:] + gb_ref[...]
    lane = lax.broadcasted_iota(jnp.int32, gts.shape, 1)
    is_f = (lane < 4 * ML_HEADS) & ((lane // ML_HEADS) % 2 == 1)
    log_sig = jnp.minimum(gts, 0.0) - jnp.log1p(jnp.exp(-jnp.abs(gts)))
    gates_ref[...] = jnp.where(is_f, log_sig, gts)


def _odd_proj(h, mod, g, w_in, gate_b, n_lat_t):
    rows, d = h.shape
    qk = 2 * ML_HEADS * ML_QK_DIM
    vw = ML_HEADS * ML_V_DIM
    n_gate = 4 * ML_HEADS
    w_pad = jnp.pad(w_in, ((0, 0), (0, LANES - n_gate))).astype(BF16)
    gb = jnp.pad(gate_b, (0, LANES - n_gate)).reshape(1, LANES)
    row_tile = lambda w: pl.BlockSpec((TM, w), lambda i: (i, 0))
    return pl.pallas_call(
        _odd_proj_kernel,
        grid=(rows // TM,),
        in_specs=[row_tile(d), pl.BlockSpec((1, N_MOD, d), lambda i: (i // n_lat_t, 0, 0)),
                  _full((1, d)), _full(w_pad.shape), _full((1, LANES))],
        out_specs=[row_tile(qk), row_tile(vw), row_tile(vw), row_tile(LANES)],
        out_shape=[jax.ShapeDtypeStruct((rows, qk), F32), jax.ShapeDtypeStruct((rows, vw), BF16),
                   jax.ShapeDtypeStruct((rows, vw), F32), jax.ShapeDtypeStruct((rows, LANES), F32)],
        compiler_params=_cparams("arbitrary"),
        name="odd_proj",
    )(h, mod, g.reshape(1, d), w_pad, gb)


def _conv_kernel(x_ref, prev_ref, next_ref, w_ref, q_ref, k_ref, *, n_lat_t):
    i = pl.program_id(0)
    nt = pl.num_programs(0)
    x = x_ref[...]
    tm = x.shape[0]
    first = (i == 0) | (i == n_lat_t)
    last = (i == n_lat_t - 1) | (i == nt - 1)
    prev_row = jnp.where(first, 0.0, prev_ref[SUBLANES - 1:SUBLANES, :])
    next_row = jnp.where(last, 0.0, next_ref[0:1, :])
    row = lax.broadcasted_iota(jnp.int32, x.shape, 0)
    xm = jnp.where(row == 0, prev_row, pltpu.roll(x, 1, 0))
    xp = jnp.where(row == tm - 1, next_row, pltpu.roll(x, tm - 1, 0))
    y = xm * w_ref[0:1, :] + x * w_ref[1:2, :] + xp * w_ref[2:3, :]
    y = _silu(y)
    qk = ML_HEADS * ML_QK_DIM
    q_ref[...] = (y[:, :qk] * (ML_QK_DIM ** -0.5)).astype(BF16)
    k_ref[...] = y[:, qk:].astype(BF16)


def _conv(pqk, conv_w, n_lat_t):
    rows, w = pqk.shape
    per = TM // SUBLANES
    last_blk = rows // SUBLANES - 1
    return pl.pallas_call(
        functools.partial(_conv_kernel, n_lat_t=n_lat_t),
        grid=(rows // TM,),
        in_specs=[
            pl.BlockSpec((TM, w), lambda i: (i, 0)),
            pl.BlockSpec((SUBLANES, w), lambda i: (jnp.maximum(i * per - 1, 0), 0)),
            pl.BlockSpec((SUBLANES, w), lambda i: (jnp.minimum((i + 1) * per, last_blk), 0)),
            _full(conv_w.shape),
        ],
        out_specs=[pl.BlockSpec((TM, w // 2), lambda i: (i, 0))] * 2,
        out_shape=[jax.ShapeDtypeStruct((rows, w // 2), BF16)] * 2,
        compiler_params=_cparams("arbitrary"),
        name="short_conv",
    )(pqk, pqk, pqk, conv_w)


def _mlstm_kernel(q_ref, k_ref, v_ref, g_ref, o_ref, ct_ref, n_ref, m_ref, *, d):
    i = pl.program_id(0)
    ln = CHUNK

    @pl.when(i == 0)
    def _():
        ct_ref[...] = jnp.zeros(ct_ref.shape, F32)
        n_ref[...] = jnp.zeros(n_ref.shape, F32)
        m_ref[...] = jnp.zeros(m_ref.shape, F32)

    r = lax.broadcasted_iota(jnp.int32, (ln, ln), 0)
    c = lax.broadcasted_iota(jnp.int32, (ln, ln), 1)
    tri = (c <= r) if d == 0 else (c >= r)
    tri_b = jnp.where(tri, 1.0, 0.0).astype(BF16)
    lane_lo = c < ML_QK_DIM
    row_lo = r < ML_QK_DIM

    g = g_ref[...]
    g_t = g.T
    nt_dims = (((1,), (1,)), ((), ()))
    bc_col = sum(jnp.dot(tri_b, piece, preferred_element_type=F32) for piece in _split3(g))
    bc_row = sum(lax.dot_general(piece, tri_b, nt_dims, preferred_element_type=F32) for piece in _split3(g_t))
    last = ln - 1 if d == 0 else 0

    for pair in range(ML_HEADS // 2):
        q_pair = q_ref[:, pair * LANES:(pair + 1) * LANES]
        k_pair = k_ref[:, pair * LANES:(pair + 1) * LANES]
        k_f = k_pair.astype(F32)
        k_t = k_f.T
        for half in range(2):
            h = 2 * pair + half
            il = 2 * ML_HEADS * d + h
            fl = il + ML_HEADS
            keep_lane = lane_lo if half == 0 else ~lane_lo
            keep_row = row_lo if half == 0 else ~row_lo
            bcol = bc_col[:, fl:fl + 1]
            brow = bc_row[fl:fl + 1, :]
            irow = g_t[il:il + 1, :]
            icol = g[:, il:il + 1]
            btot = bc_col[last:last + 1, fl:fl + 1]
            m_prev = m_ref[h:h + 1, 0:1]

            dmat = jnp.where(tri, bcol - brow + irow, -jnp.inf)
            inter = bcol + m_prev
            mrow = jnp.maximum(inter, jnp.max(dmat, axis=-1, keepdims=True))
            wts = jnp.exp(dmat - mrow)
            a = jnp.exp(inter - mrow)

            qm = jnp.where(keep_lane, q_pair, jnp.zeros_like(q_pair))
            s = lax.dot_general(qm, k_pair, nt_dims, preferred_element_type=F32) * wts
            vh = v_ref[:, h * LANES:(h + 1) * LANES]
            ct = ct_ref[h]
            n_row = n_ref[h]
            num = (jnp.dot(s.astype(BF16), vh, preferred_element_type=F32)
                   + a * jnp.dot(qm, ct.astype(BF16), preferred_element_type=F32))
            den = (jnp.sum(s, axis=-1, keepdims=True)
                   + a * jnp.sum(qm.astype(F32) * n_row, axis=-1, keepdims=True))
            o_ref[:, h * LANES:(h + 1) * LANES] = num / jnp.maximum(jnp.abs(den), jnp.exp(-mrow))

            wv = btot - bcol + icol
            m_loc = jnp.max(wv, axis=0, keepdims=True)
            e = jnp.exp(wv - m_loc)
            ev = (e * vh.astype(F32)).astype(BF16)
            k_tm = jnp.where(keep_row, k_t, 0.0).astype(BF16)
            c_loc = jnp.dot(k_tm, ev, preferred_element_type=F32)
            n_loc = jnp.sum(jnp.where(keep_lane, e * k_f, 0.0), axis=0, keepdims=True)
            m_new = jnp.maximum(btot + m_prev, m_loc)
            decay = jnp.exp(btot + m_prev - m_new)
            gain = jnp.exp(m_loc - m_new)
            ct_ref[h] = decay * ct + gain * c_loc
            n_ref[h] = decay * n_row + gain * n_loc
            m_ref[h:h + 1, :] = jnp.broadcast_to(m_new, (1, LANES))


def _mlstm_direction(q, k, v, gates, d, n_lat_ch):
    rows = q.shape[0]
    nch = rows // CHUNK
    if d == 0:
        order = lambda i: (i + n_lat_ch) % nch
    else:
        order = lambda i: nch - 1 - i
    blk = lambda w: pl.BlockSpec((CHUNK, w), lambda i: (order(i), 0))
    vw = v.shape[1]
    return pl.pallas_call(
        functools.partial(_mlstm_kernel, d=d),
        grid=(nch,),
        in_specs=[blk(q.shape[1]), blk(k.shape[1]), blk(vw), blk(LANES)],
        out_specs=blk(vw),
        out_shape=jax.ShapeDtypeStruct((rows, vw), F32),
        scratch_shapes=[
            pltpu.VMEM((ML_HEADS, LANES, ML_V_DIM), F32),
            pltpu.VMEM((ML_HEADS, 1, LANES), F32),
            pltpu.VMEM((ML_HEADS, LANES), F32),
        ],
        compiler_params=_cparams("arbitrary"),
        name="mlstm_fwd" if d == 0 else "mlstm_bwd",
    )(q, k, v, gates)


def _out_odd_kernel(h_ref, mod_ref, hf_ref, hb_ref, og_ref, ng_ref, w_ref, o_ref):
    hs = hf_ref[...] + hb_ref[...]
    parts = []
    for hh in range(ML_HEADS):
        blk = hs[:, hh * ML_V_DIM:(hh + 1) * ML_V_DIM]
        ms = jnp.mean(blk * blk, axis=-1, keepdims=True)
        parts.append(blk * lax.rsqrt(ms + EPS))
    hn = jnp.concatenate(parts, axis=1) * ng_ref[...]
    gated = (hn * jax.nn.sigmoid(og_ref[...])).astype(BF16)
    y = jnp.dot(gated, w_ref[...], preferred_element_type=F32)
    o_ref[...] = h_ref[...] + mod_ref[0][5:6] * y


def _out_odd(h, mod, hf, hb, og, out_norm_g, w_out, n_tiles, n_lat_t):
    d = h.shape[1]
    vw = hf.shape[1]
    row_tile = lambda w: pl.BlockSpec((TM, w), lambda i: (i, 0))
    return pl.pallas_call(
        _out_odd_kernel,
        grid=(n_tiles,),
        in_specs=[row_tile(d), pl.BlockSpec((1, N_MOD, d), lambda i: (i // n_lat_t, 0, 0)),
                  row_tile(vw), row_tile(vw), row_tile(vw), _full((1, vw)), _full(w_out.shape)],
        out_specs=row_tile(d),
        out_shape=jax.ShapeDtypeStruct((n_tiles * TM, d), F32),
        compiler_params=_cparams("arbitrary"),
        name="odd_out",
    )(h, mod, hf, hb, og, out_norm_g.reshape(1, vw), w_out.astype(BF16))


def kernel(x, c, ctx, c_ctx, mod_w, mod_b, norm_g, ffn_w13, ffn_w2, ab_w_in, ab_gate_norm_g, ab_spatial_w, ab_spatial_b, ab_q_norm_g, ab_k_norm_g, ab_w_out, ml_w_in, ml_conv_w, ml_gate_b, ml_out_norm_g, ml_w_out, final_norm_g):
    b, t_lat, d = x.shape
    ctx_len = ctx.shape[1]
    depth = mod_w.shape[0]
    assert b == 1 and t_lat % TM == 0 and ctx_len % TM == 0 and t_lat % ATT_TK == 0
    n_lat_t = t_lat // TM
    n_all_t = (t_lat + ctx_len) // TM

    mods = _mods(c, c_ctx, mod_w, mod_b)
    h = jnp.concatenate([x[0], ctx[0]], axis=0)
    for l in range(depth):
        keep_ctx = l < depth - 1
        mod = mods[l]
        h = _ffn(h, mod, norm_g[l, 0], ffn_w13[l, 0], ffn_w2[l, 0], n_lat_t, 0)
        n_out_t = n_all_t if keep_ctx else n_lat_t
        if l % 2 == 0:
            e = l // 2
            gated, q, kd0, kd1, vt = _even_proj(
                h, mod, norm_g[l, 1], ab_w_in[e], ab_gate_norm_g[e], ab_spatial_w[e], ab_spatial_b[e],
                ab_q_norm_g[e], ab_k_norm_g[e], t_lat, n_lat_t)
            attn = _attention(q, kd0, kd1, vt, ab_q_norm_g[e], ab_k_norm_g[e], t_lat, n_lat_t)
            h = _out_even(h, mod, gated, attn, ab_w_out[e], n_out_t, n_lat_t)
        else:
            o = l // 2
            pqk, v, og, gates = _odd_proj(h, mod, norm_g[l, 1], ml_w_in[o], ml_gate_b[o], n_lat_t)
            q, k = _conv(pqk, ml_conv_w[o], n_lat_t)
            hf = _mlstm_direction(q, k, v, gates, 0, t_lat // CHUNK)
            hb = _mlstm_direction(q, k, v, gates, 1, t_lat // CHUNK)
            h = _out_odd(h, mod, hf, hb, og, ml_out_norm_g[o], ml_w_out[o], n_out_t, n_lat_t)
        final_g = final_norm_g if l == depth - 1 else None
        h = _ffn(h, mod, norm_g[l, 2], ffn_w13[l, 1], ffn_w2[l, 1], n_lat_t, 6, final_g)
    return h[:t_lat][None]
```

```python
import functools

import jax
import jax.numpy as jnp
from jax import lax
from jax.experimental import pallas as pl
from jax.experimental.pallas import tpu as pltpu

F32 = jnp.float32
BF16 = jnp.bfloat16

LANES = 128
SUBLANES = 8
VMEM_LIMIT = 56 * 1024 * 1024

GRID_W = 64
CHUNK = 128
EPS = 1e-6
ROPE_THETA = 10000.0
LOG2_E = 1.4426950408889634
N_MOD = 9
A_GROUPS = 8
A_GROUP_DIM = 64
ATT_HEADS = 8
ATT_KV_HEADS = 2
ATT_HEAD_DIM = 64
ML_HEADS = 8
ML_QK_DIM = 64
ML_V_DIM = 128
CONV_WIDTH = 3

TM = 256
ATT_TK = 1024
ATT_LOOKAHEAD = 3
ATT_LOOKAHEAD_BOUNDED = 1
ML_CHUNKS_PER_STEP = 2
ATT_SCORE_BOUND = 60.0
V_ROWS = ATT_HEAD_DIM + 16


def _cparams(*sem):
    return pltpu.CompilerParams(dimension_semantics=sem, vmem_limit_bytes=VMEM_LIMIT)


def _full(shape):
    n = len(shape)
    return pl.BlockSpec(shape, lambda *_: (0,) * n)


def _rms_mod(x, g, shift, scale):
    ms = jnp.mean(x * x, axis=-1, keepdims=True)
    y = x * lax.rsqrt(ms + EPS) * g
    return y * (1 + scale) + shift


def _silu(x):
    return x * jax.nn.sigmoid(x)


def _split3(x):
    x1 = x.astype(BF16)
    r = x - x1.astype(F32)
    x2 = r.astype(BF16)
    x3 = (r - x2.astype(F32)).astype(BF16)
    return x1, x2, x3


def _group_mean_sq(x, bmat):
    sq = x * x
    hi = sq.astype(BF16)
    lo = (sq - hi.astype(F32)).astype(BF16)
    return (jnp.dot(hi, bmat, preferred_element_type=F32)
            + jnp.dot(lo, bmat, preferred_element_type=F32))


def _mod_kernel(c_ref, w_ref, b_ref, o_ref):
    s = _silu(c_ref[...])
    o_ref[0] = jnp.dot(s, w_ref[0], preferred_element_type=F32,
                       precision=lax.Precision.HIGHEST) + b_ref[0]


def _mods(c, c_ctx, mod_w, mod_b):
    depth, d, nd = mod_w.shape
    cvec = jnp.zeros((SUBLANES, d), F32).at[0].set(c[0]).at[1].set(c_ctx)
    out = pl.pallas_call(
        _mod_kernel,
        grid=(depth, nd // d),
        in_specs=[
            pl.BlockSpec((SUBLANES, d), lambda l, j: (0, 0)),
            pl.BlockSpec((1, d, d), lambda l, j: (l, 0, j)),
            pl.BlockSpec((1, 1, d), lambda l, j: (l, 0, j)),
        ],
        out_specs=pl.BlockSpec((1, SUBLANES, d), lambda l, j: (l, 0, j)),
        out_shape=jax.ShapeDtypeStruct((depth, SUBLANES, nd), F32),
        compiler_params=_cparams("arbitrary", "arbitrary"),
        name="modulation",
    )(cvec, mod_w, mod_b.reshape(depth, 1, nd))
    return out[:, :2].reshape(depth, 2, N_MOD, d)


def _ffn_kernel(h_ref, mod_ref, g_ref, w13_ref, w2_ref, *rest, mod_base, final, n_lat_t, split_input):
    o_ref = rest[-1]
    x = h_ref[...]
    if split_input:
        x = jnp.where(pl.program_id(0) < n_lat_t, x, rest[0][...])
        rest = rest[1:]
    m = mod_ref[0]
    z = _rms_mod(x, g_ref[...], m[mod_base:mod_base + 1], m[mod_base + 1:mod_base + 2]).astype(BF16)
    f = w2_ref.shape[0]
    gu = jnp.dot(z, w13_ref[...], preferred_element_type=F32)
    a = (_silu(gu[:, :f]) * gu[:, f:]).astype(BF16)
    y = jnp.dot(a, w2_ref[...], preferred_element_type=F32)
    out = x + (0.5 * m[mod_base + 2:mod_base + 3]) * y
    if final:
        ms = jnp.mean(out * out, axis=-1, keepdims=True)
        out = out * lax.rsqrt(ms + EPS) * rest[0][...]
    o_ref[...] = out


def _ffn(h, mod, g, w13_all, w2_all, layer, which, n_lat_t, mod_base, final_g=None, ctx_rows=None):
    d = h.shape[1]
    f = w2_all.shape[2]
    final = final_g is not None
    split_input = ctx_rows is not None
    rows = h.shape[0] + (ctx_rows.shape[0] if split_input else 0)
    last_lat = n_lat_t - 1
    in_specs = [
        pl.BlockSpec((TM, d), (lambda i: (jnp.minimum(i, last_lat), 0)) if split_input else (lambda i: (i, 0))),
        pl.BlockSpec((1, N_MOD, d), lambda i: (i // n_lat_t, 0, 0)),
        _full((1, d)),
        pl.BlockSpec((None, None, d, 2 * f), lambda i: (layer, which, 0, 0)),
        pl.BlockSpec((None, None, f, d), lambda i: (layer, which, 0, 0)),
    ]
    args = [h, mod, g.reshape(1, d), w13_all, w2_all]
    if split_input:
        in_specs.append(pl.BlockSpec((TM, d), lambda i: (jnp.maximum(i - n_lat_t, 0), 0)))
        args.append(ctx_rows)
    if final:
        in_specs.append(_full((1, d)))
        args.append(final_g.reshape(1, d))
    return pl.pallas_call(
        functools.partial(_ffn_kernel, mod_base=mod_base, final=final, n_lat_t=n_lat_t, split_input=split_input),
        grid=(rows // TM,),
        in_specs=in_specs,
        out_specs=pl.BlockSpec((TM, d), lambda i: (i, 0)),
        out_shape=jax.ShapeDtypeStruct((rows, d), F32),
        compiler_params=_cparams("arbitrary"),
        name="ffn",
    )(*args)


def _even_proj_kernel(h_ref, mod_ref, g_ref, win_ref, gng_ref, qg_ref, kg_ref, spw_ref, spb_ref,
                      cos_ref, sin_ref, b512_ref, b128_ref,
                      gated_ref, q_ref, kd0_ref, kd1_ref, vt_ref):
    x = h_ref[...]
    m = mod_ref[0]
    z = _rms_mod(x, g_ref[...], m[3:4], m[4:5]).astype(BF16)
    p = jnp.dot(z, win_ref[...], preferred_element_type=F32)
    aw = A_GROUPS * A_GROUP_DIM
    qw = ATT_HEADS * ATT_HEAD_DIM
    u = jax.nn.gelu(p[:, :aw])
    v = jax.nn.gelu(p[:, aw:2 * aw])
    b512 = b512_ref[...]
    vn = (v * lax.rsqrt(_group_mean_sq(v, b512) + EPS) * gng_ref[...]).astype(BF16)

    lane = lax.broadcasted_iota(jnp.int32, (CHUNK, LANES), 1)
    lo = lane < A_GROUP_DIM
    for c in range(TM // CHUNK):
        r0 = c * CHUNK
        for j in range(aw // LANES):
            vt = vn[r0:r0 + CHUNK, j * LANES:(j + 1) * LANES]
            s0 = jnp.dot(spw_ref[2 * j], vt, preferred_element_type=F32)
            s1 = jnp.dot(spw_ref[2 * j + 1], vt, preferred_element_type=F32)
            s = jnp.where(lo, s0, s1) + spb_ref[:, j * LANES:(j + 1) * LANES]
            gated_ref[r0:r0 + CHUNK, j * LANES:(j + 1) * LANES] = (
                u[r0:r0 + CHUNK, j * LANES:(j + 1) * LANES] * s).astype(BF16)

    o = 2 * aw
    pq = p[:, o:o + qw]
    pk = p[:, o + qw:o + qw + LANES]
    va = p[:, o + qw + LANES:o + qw + 2 * LANES]
    qn = pq * lax.rsqrt(_group_mean_sq(pq, b512) + EPS) * qg_ref[...]
    kn = pk * lax.rsqrt(_group_mean_sq(pk, b128_ref[...]) + EPS) * kg_ref[...]

    cos = cos_ref[...]
    sin = sin_ref[...]
    lane_t = lax.broadcasted_iota(jnp.int32, (TM, LANES), 1)
    first_half = (lane_t % 32) < 16
    lo_t = lane_t < ATT_HEAD_DIM

    def rope(t):
        partner = jnp.where(first_half, pltpu.roll(t, LANES - 16, 1), pltpu.roll(t, 16, 1))
        return t * cos + partner * sin

    scale = ATT_HEAD_DIM ** -0.5 * LOG2_E
    for j in range(qw // LANES):
        q_ref[:, j * LANES:(j + 1) * LANES] = (rope(qn[:, j * LANES:(j + 1) * LANES]) * scale).astype(BF16)
    kr = rope(kn)
    kswap = pltpu.roll(kr, ATT_HEAD_DIM, 1)
    kd0_ref[...] = jnp.where(lo_t, kr, kswap).astype(BF16)
    kd1_ref[...] = jnp.where(lo_t, kswap, kr).astype(BF16)
    va_t = va.T
    sub = lax.broadcasted_iota(jnp.int32, (V_ROWS - ATT_HEAD_DIM, TM), 0)
    ones_rows = jnp.where(sub == 0, 1.0, 0.0).astype(BF16)
    for g in range(ATT_KV_HEADS):
        vt_ref[g, 0, :ATT_HEAD_DIM, :] = va_t[g * ATT_HEAD_DIM:(g + 1) * ATT_HEAD_DIM].astype(BF16)
        vt_ref[g, 0, ATT_HEAD_DIM:, :] = ones_rows


def _group_matrix(width, gs):
    r = jnp.arange(width) // gs
    return jnp.where(r[:, None] == r[None, :], 1.0 / gs, 0.0).astype(BF16)


def _rope_tables(t_lat, rows):
    axis_dim = ATT_HEAD_DIM // 2
    inv_freq = ROPE_THETA ** (-jnp.arange(0, axis_dim, 2, dtype=F32) / axis_dim)
    n_row = t_lat // GRID_W
    ang_r = jnp.arange(n_row, dtype=F32)[:, None] * inv_freq
    ang_c = jnp.arange(GRID_W, dtype=F32)[:, None] * inv_freq

    def table(fn, signed):
        r = jnp.broadcast_to(fn(ang_r)[:, None, :], (n_row, GRID_W, axis_dim // 2))
        c = jnp.broadcast_to(fn(ang_c)[None, :, :], (n_row, GRID_W, axis_dim // 2))
        lo = -1.0 if signed else 1.0
        head = jnp.concatenate([lo * r, r, lo * c, c], axis=-1)
        lat = jnp.tile(head, (1, 1, LANES // ATT_HEAD_DIM)).reshape(t_lat, LANES)
        ctx = jnp.full((rows - t_lat, LANES), fn(jnp.zeros((), F32)), F32)
        return jnp.concatenate([lat, ctx], axis=0)

    return table(jnp.cos, False), table(jnp.sin, True)


def _even_proj(h, mod, g, w_in, gate_norm_g, sp_w, sp_b, q_norm_g, k_norm_g, t_lat, n_lat_t):
    rows, d = h.shape
    aw = A_GROUPS * A_GROUP_DIM
    qw = ATT_HEADS * ATT_HEAD_DIM
    cos, sin = _rope_tables(t_lat, rows)
    spb = jnp.repeat(sp_b.T, A_GROUP_DIM, axis=1)
    row_tile = lambda w: pl.BlockSpec((TM, w), lambda i: (i, 0))
    out_shape = [jax.ShapeDtypeStruct((rows, aw), BF16), jax.ShapeDtypeStruct((rows, qw), BF16)]
    out_shape += [jax.ShapeDtypeStruct((rows, LANES), BF16)] * 2
    out_shape += [jax.ShapeDtypeStruct((ATT_KV_HEADS, rows // TM, V_ROWS, TM), BF16)]
    vt_spec = pl.BlockSpec((ATT_KV_HEADS, 1, V_ROWS, TM), lambda i: (0, i, 0, 0))
    return pl.pallas_call(
        _even_proj_kernel,
        grid=(rows // TM,),
        in_specs=[
            row_tile(d),
            pl.BlockSpec((1, N_MOD, d), lambda i: (i // n_lat_t, 0, 0)),
            _full((1, d)),
            _full(w_in.shape),
            _full((1, aw)),
            _full((1, qw)),
            _full((1, LANES)),
            _full(sp_w.shape),
            _full((CHUNK, aw)),
            row_tile(LANES),
            row_tile(LANES),
            _full((aw, aw)),
            _full((LANES, LANES)),
        ],
        out_specs=[row_tile(aw), row_tile(qw), row_tile(LANES), row_tile(LANES), vt_spec],
        out_shape=out_shape,
        compiler_params=_cparams("arbitrary"),
        name="even_proj",
    )(h, mod, g.reshape(1, d), w_in.astype(BF16), gate_norm_g.reshape(1, aw),
      jnp.tile(q_norm_g, ATT_HEADS).reshape(1, qw), jnp.tile(k_norm_g, ATT_KV_HEADS).reshape(1, LANES),
      sp_w.astype(BF16), spb, cos, sin, _group_matrix(aw, A_GROUP_DIM), _group_matrix(LANES, ATT_HEAD_DIM))


def _attn_kernel(bounded_ref, q_ref, kd0_ref, kd1_ref, vt_ref, o_ref, qs_ref, m_ref, acc_ref,
                 *, n_lat_t, t_lat, ctx_len):
    i = pl.program_id(0)
    tq = q_ref.shape[0]
    lane = lax.broadcasted_iota(jnp.int32, (tq, LANES), 1)
    lo = lane < ATT_HEAD_DIM
    for h in range(ATT_HEADS):
        t = q_ref[:, (h // 2) * LANES:(h // 2 + 1) * LANES]
        qs_ref[h * tq:(h + 1) * tq] = jnp.where(lo if h % 2 == 0 else ~lo, t, jnp.zeros_like(t))

    kd_refs = (kd0_ref, kd1_ref)
    n_streams = ATT_HEADS // 2
    blocks_per_chunk = ATT_TK // TM
    nt_dims = (((1,), (1,)), ((), ()))

    def group_of(s):
        return s // (n_streams // ATT_KV_HEADS)

    def scores(s, k_start, n_blk):
        kc = kd_refs[group_of(s)][pl.ds(k_start, n_blk * TM), :]
        qs = qs_ref[2 * s * tq:(2 * s + 2) * tq]
        return lax.dot_general(kc, qs, nt_dims, preferred_element_type=F32)

    def p_times_v(s, pt, n_blk, blk0):
        pv = None
        for j in range(n_blk):
            part = jnp.dot(vt_ref[group_of(s), blk0 + j], pt[j * TM:(j + 1) * TM], preferred_element_type=F32)
            pv = part if pv is None else pv + part
        return pv

    def softmax_pv_online(s, st, n_blk, blk0):
        m_old = m_ref[s]
        m_new = jnp.maximum(m_old, jnp.max(st, axis=0, keepdims=True))
        alpha = jnp.exp2(m_old - m_new)
        pt = jnp.exp2(st - m_new).astype(BF16)
        m_ref[s] = m_new
        acc_ref[s] = alpha * acc_ref[s] + p_times_v(s, pt, n_blk, blk0)

    def softmax_pv_bounded(s, st, n_blk, blk0):
        pt = jnp.exp2(st).astype(BF16)
        acc_ref[s] += p_times_v(s, pt, n_blk, blk0)

    def sweep(softmax_pv, lookahead):
        def step(k_start, n_blk, blk0):
            pending = [scores(s, k_start, n_blk) for s in range(min(lookahead, n_streams))]
            for s in range(n_streams):
                st = pending.pop(0)
                if s + lookahead < n_streams:
                    pending.append(scores(s + lookahead, k_start, n_blk))
                softmax_pv(s, st, n_blk, blk0)

        def body(c, carry):
            step(pl.multiple_of(c * ATT_TK, ATT_TK), blocks_per_chunk, c * blocks_per_chunk)
            return carry

        n_full = jnp.where(i < n_lat_t, t_lat // ATT_TK, 0)
        lax.fori_loop(0, n_full, body, 0)
        step(t_lat, ctx_len // TM, t_lat // TM)

    acc_ref[...] = jnp.zeros(acc_ref.shape, F32)
    bounded = bounded_ref[0] == 1

    @pl.when(bounded)
    def _():
        sweep(softmax_pv_bounded, ATT_LOOKAHEAD_BOUNDED)

    @pl.when(jnp.logical_not(bounded))
    def _():
        m_ref[...] = jnp.full(m_ref.shape, -jnp.inf, F32)
        sweep(softmax_pv_online, ATT_LOOKAHEAD)

    for s in range(n_streams):
        acc = acc_ref[s]
        o_t = acc[:ATT_HEAD_DIM] * (1.0 / acc[ATT_HEAD_DIM:ATT_HEAD_DIM + 1])
        pair = jnp.concatenate([o_t[:, :tq], o_t[:, tq:]], axis=0)
        o_ref[:, s * LANES:(s + 1) * LANES] = pair.T.astype(BF16)


def _attention(q, kd0, kd1, vt, q_norm_g, k_norm_g, t_lat, n_lat_t):
    rows, qw = q.shape
    ctx_len = rows - t_lat
    score_bound = (ATT_HEAD_DIM ** 0.5 * LOG2_E) * jnp.max(jnp.abs(q_norm_g)) * jnp.max(jnp.abs(k_norm_g))
    bounded = (score_bound <= ATT_SCORE_BOUND).astype(jnp.int32).reshape(1)
    return pl.pallas_call(
        functools.partial(_attn_kernel, n_lat_t=n_lat_t, t_lat=t_lat, ctx_len=ctx_len),
        grid=(rows // TM,),
        in_specs=[pl.BlockSpec(memory_space=pltpu.SMEM),
                  pl.BlockSpec((TM, qw), lambda i: (i, 0)), _full((rows, LANES)), _full((rows, LANES)),
                  _full(vt.shape)],
        out_specs=pl.BlockSpec((TM, qw), lambda i: (i, 0)),
        out_shape=jax.ShapeDtypeStruct((rows, qw), BF16),
        scratch_shapes=[
            pltpu.VMEM((ATT_HEADS * TM, LANES), BF16),
            pltpu.VMEM((ATT_HEADS // 2, 1, 2 * TM), F32),
            pltpu.VMEM((ATT_HEADS // 2, V_ROWS, 2 * TM), F32),
        ],
        compiler_params=_cparams("arbitrary"),
        name="gqa_attention",
    )(bounded, q, kd0, kd1, vt)


def _out_even_kernel(h_ref, mod_ref, a_ref, b_ref, w_ref, o_ref):
    aw = a_ref.shape[1]
    y = (jnp.dot(a_ref[...], w_ref[:aw], preferred_element_type=F32)
         + jnp.dot(b_ref[...], w_ref[aw:], preferred_element_type=F32))
    o_ref[...] = h_ref[...] + mod_ref[0][5:6] * y


def _out_even(h, mod, gated, attn, w_out, n_tiles, n_lat_t):
    d = h.shape[1]
    row_tile = lambda w: pl.BlockSpec((TM, w), lambda i: (i, 0))
    return pl.pallas_call(
        _out_even_kernel,
        grid=(n_tiles,),
        in_specs=[row_tile(d), pl.BlockSpec((1, N_MOD, d), lambda i: (i // n_lat_t, 0, 0)),
                  row_tile(gated.shape[1]), row_tile(attn.shape[1]), _full(w_out.shape)],
        out_specs=row_tile(d),
        out_shape=jax.ShapeDtypeStruct((n_tiles * TM, d), F32),
        compiler_params=_cparams("arbitrary"),
        name="even_out",
    )(h, mod, gated, attn, w_out.astype(BF16))


def _odd_proj_kernel(h_ref, mod_ref, g_ref, win_ref, gb_ref, pqk_ref, v_ref, o_ref, gates_ref):
    x = h_ref[...]
    m = mod_ref[0]
    z = _rms_mod(x, g_ref[...], m[3:4], m[4:5]).astype(BF16)
    p = jnp.dot(z, win_ref[...], preferred_element_type=F32)
    qk = 2 * ML_HEADS * ML_QK_DIM
    vw = ML_HEADS * ML_V_DIM
    pqk_ref[...] = p[:, :qk]
    v_ref[...] = p[:, qk:qk + vw].astype(BF16)
    o_ref[...] = p[:, qk + vw:qk + 2 * vw]
    gts = p[:, qk + 2 * vw:] + gb_ref[...]
    lane = lax.broadcasted_iota(jnp.int32, gts.shape, 1)
    is_f = (lane < 4 * ML_HEADS) & ((lane // ML_HEADS) % 2 == 1)
    log_sig = jnp.minimum(gts, 0.0) - jnp.log1p(jnp.exp(-jnp.abs(gts)))
    gates_ref[...] = jnp.where(is_f, log_sig, gts)


def _odd_proj(h, mod, g, w_in, gate_b, n_lat_t):
    rows, d = h.shape
    qk = 2 * ML_HEADS * ML_QK_DIM
    vw = ML_HEADS * ML_V_DIM
    n_gate = 4 * ML_HEADS
    w_pad = jnp.pad(w_in, ((0, 0), (0, LANES - n_gate))).astype(BF16)
    gb = jnp.pad(gate_b, (0, LANES - n_gate)).reshape(1, LANES)
    row_tile = lambda w: pl.BlockSpec((TM, w), lambda i: (i, 0))
    return pl.pallas_call(
        _odd_proj_kernel,
        grid=(rows // TM,),
        in_specs=[row_tile(d), pl.BlockSpec((1, N_MOD, d), lambda i: (i // n_lat_t, 0, 0)),
                  _full((1, d)), _full(w_pad.shape), _full((1, LANES))],
        out_specs=[row_tile(qk), row_tile(vw), row_tile(vw), row_tile(LANES)],
        out_shape=[jax.ShapeDtypeStruct((rows, qk), F32), jax.ShapeDtypeStruct((rows, vw), BF16),
                   jax.ShapeDtypeStruct((rows, vw), F32), jax.ShapeDtypeStruct((rows, LANES), F32)],
        compiler_params=_cparams("arbitrary"),
        name="odd_proj",
    )(h, mod, g.reshape(1, d), w_pad, gb)


def _conv_kernel(x_ref, prev_ref, next_ref, w_ref, q_ref, k_ref, *, n_lat_t):
    i = pl.program_id(0)
    nt = pl.num_programs(0)
    x = x_ref[...]
    tm = x.shape[0]
    first = (i == 0) | (i == n_lat_t)
    last = (i == n_lat_t - 1) | (i == nt - 1)
    prev_row = jnp.where(first, 0.0, prev_ref[SUBLANES - 1:SUBLANES, :])
    next_row = jnp.where(last, 0.0, next_ref[0:1, :])
    row = lax.broadcasted_iota(jnp.int32, x.shape, 0)
    xm = jnp.where(row == 0, prev_row, pltpu.roll(x, 1, 0))
    xp = jnp.where(row == tm - 1, next_row, pltpu.roll(x, tm - 1, 0))
    y = xm * w_ref[0:1, :] + x * w_ref[1:2, :] + xp * w_ref[2:3, :]
    y = _silu(y)
    qk = ML_HEADS * ML_QK_DIM
    q_ref[...] = (y[:, :qk] * (ML_QK_DIM ** -0.5)).astype(BF16)
    k_ref[...] = y[:, qk:].astype(BF16)


def _conv(pqk, conv_w, n_lat_t):
    rows, w = pqk.shape
    per = TM // SUBLANES
    last_blk = rows // SUBLANES - 1
    return pl.pallas_call(
        functools.partial(_conv_kernel, n_lat_t=n_lat_t),
        grid=(rows // TM,),
        in_specs=[
            pl.BlockSpec((TM, w), lambda i: (i, 0)),
            pl.BlockSpec((SUBLANES, w), lambda i: (jnp.maximum(i * per - 1, 0), 0)),
            pl.BlockSpec((SUBLANES, w), lambda i: (jnp.minimum((i + 1) * per, last_blk), 0)),
            _full(conv_w.shape),
        ],
        out_specs=[pl.BlockSpec((TM, w // 2), lambda i: (i, 0))] * 2,
        out_shape=[jax.ShapeDtypeStruct((rows, w // 2), BF16)] * 2,
        compiler_params=_cparams("arbitrary"),
        name="short_conv",
    )(pqk, pqk, pqk, conv_w)


def _mlstm_prepare(d, g):
    ln = CHUNK
    r = lax.broadcasted_iota(jnp.int32, (ln, ln), 0)
    c = lax.broadcasted_iota(jnp.int32, (ln, ln), 1)
    tri = (c <= r) if d == 0 else (c >= r)
    tri_b = jnp.where(tri, 1.0, 0.0).astype(BF16)
    g_t = g.T
    nt_dims = (((1,), (1,)), ((), ()))
    bc_col = sum(jnp.dot(tri_b, piece, preferred_element_type=F32) for piece in _split3(g))
    bc_row = sum(lax.dot_general(piece, tri_b, nt_dims, preferred_element_type=F32) for piece in _split3(g_t))
    il0 = 2 * ML_HEADS * d
    fl0 = il0 + ML_HEADS
    u = g_t[il0:il0 + ML_HEADS, :] - bc_row[fl0:fl0 + ML_HEADS, :]
    lane = lax.broadcasted_iota(jnp.int32, u.shape, 1)
    sh = 1
    while sh < ln:
        if d == 0:
            shifted = jnp.where(lane >= sh, pltpu.roll(u, sh, 1), -jnp.inf)
        else:
            shifted = jnp.where(lane < ln - sh, pltpu.roll(u, ln - sh, 1), -jnp.inf)
        u = jnp.maximum(u, shifted)
        sh *= 2
    run_max_cols = jnp.concatenate([u, jnp.zeros((ln - ML_HEADS, ln), F32)], axis=0).T
    return dict(tri=tri, g=g, g_t=g_t, bc_col=bc_col, bc_row=bc_row, run_max_cols=run_max_cols,
                lane_lo=c < ML_QK_DIM, row_lo=r < ML_QK_DIM, last=ln - 1 if d == 0 else 0)


def _mlstm_head(d, h, pre, q_pair, k_pair, k_t, v_ref, o_ref, ct_ref, m_ref):
    half = h % 2
    nt_dims = (((1,), (1,)), ((), ()))
    il = 2 * ML_HEADS * d + h
    fl = il + ML_HEADS
    keep_lane = pre["lane_lo"] if half == 0 else ~pre["lane_lo"]
    keep_row = pre["row_lo"] if half == 0 else ~pre["row_lo"]
    bcol = pre["bc_col"][:, fl:fl + 1]
    brow = pre["bc_row"][fl:fl + 1, :]
    irow = pre["g_t"][il:il + 1, :]
    icol = pre["g"][:, il:il + 1]
    btot = pre["bc_col"][pre["last"]:pre["last"] + 1, fl:fl + 1]
    m_prev = m_ref[h:h + 1, 0:1]

    dmat = jnp.where(pre["tri"], bcol - brow + irow, -jnp.inf)
    inter = bcol + m_prev
    mrow = jnp.maximum(inter, bcol + pre["run_max_cols"][:, h:h + 1])
    wts = jnp.exp(dmat - mrow)
    a = jnp.exp(inter - mrow)

    qm = jnp.where(keep_lane, q_pair, jnp.zeros_like(q_pair))
    s = lax.dot_general(qm, k_pair, nt_dims, preferred_element_type=F32) * wts
    vh = v_ref[:, h * LANES:(h + 1) * LANES]
    v_aug = jnp.concatenate([vh, jnp.ones_like(vh)], axis=1)
    ct = ct_ref[h]
    num_den = (jnp.dot(s.astype(BF16), v_aug, preferred_element_type=F32)
               + a * jnp.dot(qm, ct.astype(BF16), preferred_element_type=F32))
    num = num_den[:, :ML_V_DIM]
    den = num_den[:, ML_V_DIM:]
    o_ref[:, h * LANES:(h + 1) * LANES] = num / jnp.maximum(jnp.abs(den), jnp.exp(-mrow))

    wv = btot - bcol + icol
    m_loc = jnp.max(wv, axis=0, keepdims=True)
    e = jnp.exp(wv - m_loc)
    ev = (e * v_aug.astype(F32)).astype(BF16)
    k_tm = jnp.where(keep_row, k_t, 0.0).astype(BF16)
    c_loc = jnp.dot(k_tm, ev, preferred_element_type=F32)
    m_new = jnp.maximum(btot + m_prev, m_loc)
    decay = jnp.exp(btot + m_prev - m_new)
    gain = jnp.exp(m_loc - m_new)
    ct_ref[h] = decay * ct + gain * c_loc
    m_ref[h:h + 1, :] = jnp.broadcast_to(m_new, (1, LANES))


def _mlstm_kernel(qf_ref, kf_ref, vf_ref, gf_ref, qb_ref, kb_ref, vb_ref, gb_ref, of_ref, ob_ref,
                  ct_ref, m_ref):
    @pl.when(pl.program_id(0) == 0)
    def _():
        ct_ref[...] = jnp.zeros(ct_ref.shape, F32)
        m_ref[...] = jnp.zeros(m_ref.shape, F32)

    dirs = ((qf_ref, kf_ref, vf_ref, gf_ref, of_ref), (qb_ref, kb_ref, vb_ref, gb_ref, ob_ref))

    def chunk_rows(d, step):
        c = step if d == 0 else ML_CHUNKS_PER_STEP - 1 - step
        return pl.ds(c * CHUNK, CHUNK)

    pre = [[_mlstm_prepare(d, dirs[d][3][chunk_rows(d, step), :]) for d in range(2)]
           for step in range(ML_CHUNKS_PER_STEP)]
    for step in range(ML_CHUNKS_PER_STEP):
        for pair in range(ML_HEADS // 2):
            loaded = []
            for d in range(2):
                rows = chunk_rows(d, step)
                q_pair = dirs[d][0][rows, pair * LANES:(pair + 1) * LANES]
                k_pair = dirs[d][1][rows, pair * LANES:(pair + 1) * LANES]
                loaded.append((q_pair, k_pair, k_pair.astype(F32).T))
            for half in range(2):
                for d in range(2):
                    rows = chunk_rows(d, step)
                    q_pair, k_pair, k_t = loaded[d]
                    _mlstm_head(d, 2 * pair + half, pre[step][d], q_pair, k_pair, k_t,
                                dirs[d][2].at[rows], dirs[d][4].at[rows], ct_ref.at[d], m_ref.at[d])


def _mlstm(q, k, v, gates, n_lat_ch):
    rows = q.shape[0]
    step_rows = ML_CHUNKS_PER_STEP * CHUNK
    assert rows % step_rows == 0 and (n_lat_ch * CHUNK) % step_rows == 0
    nblk = rows // step_rows
    n_lat_blk = n_lat_ch * CHUNK // step_rows
    fwd = lambda i: (i + n_lat_blk) % nblk
    bwd = lambda i: nblk - 1 - i
    vw = v.shape[1]

    def blocks(order):
        blk = lambda w: pl.BlockSpec((step_rows, w), lambda i: (order(i), 0))
        return [blk(q.shape[1]), blk(k.shape[1]), blk(vw), blk(LANES)]

    out_blk = lambda order: pl.BlockSpec((step_rows, vw), lambda i: (order(i), 0))
    return pl.pallas_call(
        _mlstm_kernel,
        grid=(nblk,),
        in_specs=blocks(fwd) + blocks(bwd),
        out_specs=[out_blk(fwd), out_blk(bwd)],
        out_shape=[jax.ShapeDtypeStruct((rows, vw), F32)] * 2,
        scratch_shapes=[
            pltpu.VMEM((2, ML_HEADS, LANES, ML_V_DIM + LANES), F32),
            pltpu.VMEM((2, ML_HEADS, LANES), F32),
        ],
        compiler_params=_cparams("arbitrary"),
        name="mlstm",
    )(q, k, v, gates, q, k, v, gates)


def _out_odd_kernel(h_ref, mod_ref, hf_ref, hb_ref, og_ref, ng_ref, w_ref, o_ref):
    hs = hf_ref[...] + hb_ref[...]
    parts = []
    for hh in range(ML_HEADS):
        blk = hs[:, hh * ML_V_DIM:(hh + 1) * ML_V_DIM]
        ms = jnp.mean(blk * blk, axis=-1, keepdims=True)
        parts.append(blk * lax.rsqrt(ms + EPS))
    hn = jnp.concatenate(parts, axis=1) * ng_ref[...]
    gated = (hn * jax.nn.sigmoid(og_ref[...])).astype(BF16)
    y = jnp.dot(gated, w_ref[...], preferred_element_type=F32)
    o_ref[...] = h_ref[...] + mod_ref[0][5:6] * y


def _out_odd(h, mod, hf, hb, og, out_norm_g, w_out, n_tiles, n_lat_t):
    d = h.shape[1]
    vw = hf.shape[1]
    row_tile = lambda w: pl.BlockSpec((TM, w), lambda i: (i, 0))
    return pl.pallas_call(
        _out_odd_kernel,
        grid=(n_tiles,),
        in_specs=[row_tile(d), pl.BlockSpec((1, N_MOD, d), lambda i: (i // n_lat_t, 0, 0)),
                  row_tile(vw), row_tile(vw), row_tile(vw), _full((1, vw)), _full(w_out.shape)],
        out_specs=row_tile(d),
        out_shape=jax.ShapeDtypeStruct((n_tiles * TM, d), F32),
        compiler_params=_cparams("arbitrary"),
        name="odd_out",
    )(h, mod, hf, hb, og, out_norm_g.reshape(1, vw), w_out.astype(BF16))


def kernel(x, c, ctx, c_ctx, mod_w, mod_b, norm_g, ffn_w13, ffn_w2, ab_w_in, ab_gate_norm_g, ab_spatial_w, ab_spatial_b, ab_q_norm_g, ab_k_norm_g, ab_w_out, ml_w_in, ml_conv_w, ml_gate_b, ml_out_norm_g, ml_w_out, final_norm_g):
    b, t_lat, d = x.shape
    ctx_len = ctx.shape[1]
    depth = mod_w.shape[0]
    assert b == 1 and t_lat % TM == 0 and ctx_len % TM == 0 and t_lat % ATT_TK == 0
    n_lat_t = t_lat // TM
    n_all_t = (t_lat + ctx_len) // TM

    mods = _mods(c, c_ctx, mod_w, mod_b)
    w13_all = ffn_w13.astype(BF16)
    w2_all = ffn_w2.astype(BF16)
    h = x[0]
    for l in range(depth):
        keep_ctx = l < depth - 1
        mod = mods[l]
        h = _ffn(h, mod, norm_g[l, 0], w13_all, w2_all, l, 0, n_lat_t, 0, ctx_rows=ctx[0] if l == 0 else None)
        n_out_t = n_all_t if keep_ctx else n_lat_t
        if l % 2 == 0:
            e = l // 2
            gated, q, kd0, kd1, vt = _even_proj(
                h, mod, norm_g[l, 1], ab_w_in[e], ab_gate_norm_g[e], ab_spatial_w[e], ab_spatial_b[e],
                ab_q_norm_g[e], ab_k_norm_g[e], t_lat, n_lat_t)
            attn = _attention(q, kd0, kd1, vt, ab_q_norm_g[e], ab_k_norm_g[e], t_lat, n_lat_t)
            h = _out_even(h, mod, gated, attn, ab_w_out[e], n_out_t, n_lat_t)
        else:
            o = l // 2
            pqk, v, og, gates = _odd_proj(h, mod, norm_g[l, 1], ml_w_in[o], ml_gate_b[o], n_lat_t)
            q, k = _conv(pqk, ml_conv_w[o], n_lat_t)
            hf, hb = _mlstm(q, k, v, gates, t_lat // CHUNK)
            h = _out_odd(h, mod, hf, hb, og, ml_out_norm_g[o], ml_w_out[o], n_out_t, n_lat_t)
        final_g = final_norm_g if l == depth - 1 else None
        h = _ffn(h, mod, norm_g[l, 2], w13_all, w2_all, l, 1, n_lat_t, 6, final_g)
    return h[:t_lat][None]
```

```python
import functools

import jax
import jax.numpy as jnp
from jax import lax
from jax.experimental import pallas as pl
from jax.experimental.pallas import tpu as pltpu

F32 = jnp.float32
BF16 = jnp.bfloat16

LANES = 128
SUBLANES = 8
VMEM_LIMIT = 56 * 1024 * 1024

GRID_W = 64
CHUNK = 128
EPS = 1e-6
ROPE_THETA = 10000.0
LOG2_E = 1.4426950408889634
N_MOD = 9
A_GROUPS = 8
A_GROUP_DIM = 64
ATT_HEADS = 8
ATT_KV_HEADS = 2
ATT_HEAD_DIM = 64
ML_HEADS = 8
ML_QK_DIM = 64
ML_V_DIM = 128
CONV_WIDTH = 3

TM = 256
ATT_TK = 1024
ATT_LOOKAHEAD = 3
ATT_LOOKAHEAD_BOUNDED = 2
ML_CHUNKS_PER_STEP = 2
ATT_SCORE_BOUND = 60.0
V_ROWS = ATT_HEAD_DIM + 16


def _cparams(*sem):
    return pltpu.CompilerParams(dimension_semantics=sem, vmem_limit_bytes=VMEM_LIMIT)


def _full(shape):
    n = len(shape)
    return pl.BlockSpec(shape, lambda *_: (0,) * n)


def _rms_mod(x, g, shift, scale):
    ms = jnp.mean(x * x, axis=-1, keepdims=True)
    y = x * lax.rsqrt(ms + EPS) * g
    return y * (1 + scale) + shift


def _silu(x):
    return x * jax.nn.sigmoid(x)


def _split3(x):
    x1 = x.astype(BF16)
    r = x - x1.astype(F32)
    x2 = r.astype(BF16)
    x3 = (r - x2.astype(F32)).astype(BF16)
    return x1, x2, x3


def _group_mean_sq(x, bmat):
    sq = x * x
    hi = sq.astype(BF16)
    lo = (sq - hi.astype(F32)).astype(BF16)
    return (jnp.dot(hi, bmat, preferred_element_type=F32)
            + jnp.dot(lo, bmat, preferred_element_type=F32))


def _mod_kernel(c_ref, w_ref, b_ref, o_ref):
    s = _silu(c_ref[...])
    o_ref[0] = jnp.dot(s, w_ref[0], preferred_element_type=F32,
                       precision=lax.Precision.HIGHEST) + b_ref[0]


def _mods(c, c_ctx, mod_w, mod_b):
    depth, d, nd = mod_w.shape
    cvec = jnp.zeros((SUBLANES, d), F32).at[0].set(c[0]).at[1].set(c_ctx)
    out = pl.pallas_call(
        _mod_kernel,
        grid=(depth, nd // d),
        in_specs=[
            pl.BlockSpec((SUBLANES, d), lambda l, j: (0, 0)),
            pl.BlockSpec((1, d, d), lambda l, j: (l, 0, j)),
            pl.BlockSpec((1, 1, d), lambda l, j: (l, 0, j)),
        ],
        out_specs=pl.BlockSpec((1, SUBLANES, d), lambda l, j: (l, 0, j)),
        out_shape=jax.ShapeDtypeStruct((depth, SUBLANES, nd), F32),
        compiler_params=_cparams("arbitrary", "arbitrary"),
        name="modulation",
    )(cvec, mod_w, mod_b.reshape(depth, 1, nd))
    return out[:, :2].reshape(depth, 2, N_MOD, d)


def _even_mixer_out(a_ref, b_ref, w_ref):
    aw = a_ref.shape[1]
    return (jnp.dot(a_ref[...], w_ref[:aw], preferred_element_type=F32)
            + jnp.dot(b_ref[...], w_ref[aw:], preferred_element_type=F32))


def _odd_mixer_out(hf_ref, hb_ref, og_ref, ng_ref, w_ref):
    hs = hf_ref[...] + hb_ref[...]
    parts = []
    for hh in range(ML_HEADS):
        blk = hs[:, hh * ML_V_DIM:(hh + 1) * ML_V_DIM]
        ms = jnp.mean(blk * blk, axis=-1, keepdims=True)
        parts.append(blk * lax.rsqrt(ms + EPS))
    hn = jnp.concatenate(parts, axis=1) * ng_ref[...]
    gated = (hn * jax.nn.sigmoid(og_ref[...])).astype(BF16)
    return jnp.dot(gated, w_ref[...], preferred_element_type=F32)


_MIXER_OUT = {"even": (_even_mixer_out, 3), "odd": (_odd_mixer_out, 5)}


def _ffn_kernel(h_ref, mod_ref, g_ref, w13_ref, w2_ref, *rest, mod_base, final, n_lat_t, split_input, mixer):
    o_ref = rest[-1]
    x = h_ref[...]
    m = mod_ref[0]
    if split_input:
        x = jnp.where(pl.program_id(0) < n_lat_t, x, rest[0][...])
        rest = rest[1:]
    if mixer is not None:
        fn, n_ops = _MIXER_OUT[mixer]
        x = x + m[5:6] * fn(*rest[:n_ops])
        rest = rest[n_ops:]
    z = _rms_mod(x, g_ref[...], m[mod_base:mod_base + 1], m[mod_base + 1:mod_base + 2]).astype(BF16)
    f = w2_ref.shape[0]
    gu = jnp.dot(z, w13_ref[...], preferred_element_type=F32)
    a = (_silu(gu[:, :f]) * gu[:, f:]).astype(BF16)
    y = jnp.dot(a, w2_ref[...], preferred_element_type=F32)
    out = x + (0.5 * m[mod_base + 2:mod_base + 3]) * y
    if final:
        ms = jnp.mean(out * out, axis=-1, keepdims=True)
        out = out * lax.rsqrt(ms + EPS) * rest[0][...]
    o_ref[...] = out


def _ffn(h, mod, g, w13_all, w2_all, layer, which, n_tiles, n_lat_t, mod_base, final_g=None, ctx_rows=None,
         mixer=None, mixer_args=()):
    d = h.shape[1]
    f = w2_all.shape[2]
    final = final_g is not None
    split_input = ctx_rows is not None
    last_lat = n_lat_t - 1
    row_tile = lambda w: pl.BlockSpec((TM, w), lambda i: (i, 0))
    in_specs = [
        pl.BlockSpec((TM, d), (lambda i: (jnp.minimum(i, last_lat), 0)) if split_input else (lambda i: (i, 0))),
        pl.BlockSpec((1, N_MOD, d), lambda i: (i // n_lat_t, 0, 0)),
        _full((1, d)),
        pl.BlockSpec((None, None, d, 2 * f), lambda i: (layer, which, 0, 0)),
        pl.BlockSpec((None, None, f, d), lambda i: (layer, which, 0, 0)),
    ]
    args = [h, mod, g.reshape(1, d), w13_all, w2_all]
    if split_input:
        in_specs.append(pl.BlockSpec((TM, d), lambda i: (jnp.maximum(i - n_lat_t, 0), 0)))
        args.append(ctx_rows)
    for arr in mixer_args:
        in_specs.append(row_tile(arr.shape[1]) if arr.shape[0] == h.shape[0] else _full(arr.shape))
        args.append(arr)
    if final:
        in_specs.append(_full((1, d)))
        args.append(final_g.reshape(1, d))
    return pl.pallas_call(
        functools.partial(_ffn_kernel, mod_base=mod_base, final=final, n_lat_t=n_lat_t, split_input=split_input,
                          mixer=mixer),
        grid=(n_tiles,),
        in_specs=in_specs,
        out_specs=row_tile(d),
        out_shape=jax.ShapeDtypeStruct((n_tiles * TM, d), F32),
        compiler_params=_cparams("arbitrary"),
        name="ffn",
    )(*args)


def _even_proj_kernel(h_ref, mod_ref, g_ref, win_ref, gng_ref, qg_ref, kg_ref, spw_ref, spb_ref,
                      cos_ref, sin_ref, b512_ref, b128_ref,
                      gated_ref, q_ref, kd0_ref, kd1_ref, vt_ref):
    x = h_ref[...]
    m = mod_ref[0]
    z = _rms_mod(x, g_ref[...], m[3:4], m[4:5]).astype(BF16)
    p = jnp.dot(z, win_ref[...], preferred_element_type=F32)
    aw = A_GROUPS * A_GROUP_DIM
    qw = ATT_HEADS * ATT_HEAD_DIM
    u = jax.nn.gelu(p[:, :aw])
    v = jax.nn.gelu(p[:, aw:2 * aw])
    b512 = b512_ref[...]
    vn = (v * lax.rsqrt(_group_mean_sq(v, b512) + EPS) * gng_ref[...]).astype(BF16)

    lane = lax.broadcasted_iota(jnp.int32, (CHUNK, LANES), 1)
    lo = lane < A_GROUP_DIM
    for c in range(TM // CHUNK):
        r0 = c * CHUNK
        for j in range(aw // LANES):
            vt = vn[r0:r0 + CHUNK, j * LANES:(j + 1) * LANES]
            s0 = jnp.dot(spw_ref[2 * j], vt, preferred_element_type=F32)
            s1 = jnp.dot(spw_ref[2 * j + 1], vt, preferred_element_type=F32)
            s = jnp.where(lo, s0, s1) + spb_ref[:, j * LANES:(j + 1) * LANES]
            gated_ref[r0:r0 + CHUNK, j * LANES:(j + 1) * LANES] = (
                u[r0:r0 + CHUNK, j * LANES:(j + 1) * LANES] * s).astype(BF16)

    o = 2 * aw
    pq = p[:, o:o + qw]
    pk = p[:, o + qw:o + qw + LANES]
    va = p[:, o + qw + LANES:o + qw + 2 * LANES]
    qn = pq * lax.rsqrt(_group_mean_sq(pq, b512) + EPS) * qg_ref[...]
    kn = pk * lax.rsqrt(_group_mean_sq(pk, b128_ref[...]) + EPS) * kg_ref[...]

    cos = cos_ref[...]
    sin = sin_ref[...]
    lane_t = lax.broadcasted_iota(jnp.int32, (TM, LANES), 1)
    first_half = (lane_t % 32) < 16
    lo_t = lane_t < ATT_HEAD_DIM

    def rope(t):
        partner = jnp.where(first_half, pltpu.roll(t, LANES - 16, 1), pltpu.roll(t, 16, 1))
        return t * cos + partner * sin

    scale = ATT_HEAD_DIM ** -0.5 * LOG2_E
    for j in range(qw // LANES):
        q_ref[:, j * LANES:(j + 1) * LANES] = (rope(qn[:, j * LANES:(j + 1) * LANES]) * scale).astype(BF16)
    kr = rope(kn)
    kswap = pltpu.roll(kr, ATT_HEAD_DIM, 1)
    kd0_ref[...] = jnp.where(lo_t, kr, kswap).astype(BF16)
    kd1_ref[...] = jnp.where(lo_t, kswap, kr).astype(BF16)
    va_t = va.T
    sub = lax.broadcasted_iota(jnp.int32, (V_ROWS - ATT_HEAD_DIM, TM), 0)
    ones_rows = jnp.where(sub == 0, 1.0, 0.0).astype(BF16)
    for g in range(ATT_KV_HEADS):
        vt_ref[g, 0, :ATT_HEAD_DIM, :] = va_t[g * ATT_HEAD_DIM:(g + 1) * ATT_HEAD_DIM].astype(BF16)
        vt_ref[g, 0, ATT_HEAD_DIM:, :] = ones_rows


def _group_matrix(width, gs):
    r = jnp.arange(width) // gs
    return jnp.where(r[:, None] == r[None, :], 1.0 / gs, 0.0).astype(BF16)


def _rope_tables(t_lat, rows):
    axis_dim = ATT_HEAD_DIM // 2
    inv_freq = ROPE_THETA ** (-jnp.arange(0, axis_dim, 2, dtype=F32) / axis_dim)
    n_row = t_lat // GRID_W
    ang_r = jnp.arange(n_row, dtype=F32)[:, None] * inv_freq
    ang_c = jnp.arange(GRID_W, dtype=F32)[:, None] * inv_freq

    def table(fn, signed):
        r = jnp.broadcast_to(fn(ang_r)[:, None, :], (n_row, GRID_W, axis_dim // 2))
        c = jnp.broadcast_to(fn(ang_c)[None, :, :], (n_row, GRID_W, axis_dim // 2))
        lo = -1.0 if signed else 1.0
        head = jnp.concatenate([lo * r, r, lo * c, c], axis=-1)
        lat = jnp.tile(head, (1, 1, LANES // ATT_HEAD_DIM)).reshape(t_lat, LANES)
        ctx = jnp.full((rows - t_lat, LANES), fn(jnp.zeros((), F32)), F32)
        return jnp.concatenate([lat, ctx], axis=0)

    return table(jnp.cos, False), table(jnp.sin, True)


def _even_proj(h, mod, g, w_in, gate_norm_g, sp_w, sp_b, q_norm_g, k_norm_g, t_lat, n_lat_t):
    rows, d = h.shape
    aw = A_GROUPS * A_GROUP_DIM
    qw = ATT_HEADS * ATT_HEAD_DIM
    cos, sin = _rope_tables(t_lat, rows)
    spb = jnp.repeat(sp_b.T, A_GROUP_DIM, axis=1)
    row_tile = lambda w: pl.BlockSpec((TM, w), lambda i: (i, 0))
    out_shape = [jax.ShapeDtypeStruct((rows, aw), BF16), jax.ShapeDtypeStruct((rows, qw), BF16)]
    out_shape += [jax.ShapeDtypeStruct((rows, LANES), BF16)] * 2
    out_shape += [jax.ShapeDtypeStruct((ATT_KV_HEADS, rows // TM, V_ROWS, TM), BF16)]
    vt_spec = pl.BlockSpec((ATT_KV_HEADS, 1, V_ROWS, TM), lambda i: (0, i, 0, 0))
    return pl.pallas_call(
        _even_proj_kernel,
        grid=(rows // TM,),
        in_specs=[
            row_tile(d),
            pl.BlockSpec((1, N_MOD, d), lambda i: (i // n_lat_t, 0, 0)),
            _full((1, d)),
            _full(w_in.shape),
            _full((1, aw)),
            _full((1, qw)),
            _full((1, LANES)),
            _full(sp_w.shape),
            _full((CHUNK, aw)),
            row_tile(LANES),
            row_tile(LANES),
            _full((aw, aw)),
            _full((LANES, LANES)),
        ],
        out_specs=[row_tile(aw), row_tile(qw), row_tile(LANES), row_tile(LANES), vt_spec],
        out_shape=out_shape,
        compiler_params=_cparams("arbitrary"),
        name="even_proj",
    )(h, mod, g.reshape(1, d), w_in.astype(BF16), gate_norm_g.reshape(1, aw),
      jnp.tile(q_norm_g, ATT_HEADS).reshape(1, qw), jnp.tile(k_norm_g, ATT_KV_HEADS).reshape(1, LANES),
      sp_w.astype(BF16), spb, cos, sin, _group_matrix(aw, A_GROUP_DIM), _group_matrix(LANES, ATT_HEAD_DIM))


def _attn_kernel(bounded_ref, q_ref, kd0_ref, kd1_ref, vt_ref, o_ref, qs_ref, m_ref, acc_ref,
                 *, n_lat_t, t_lat, ctx_len):
    i = pl.program_id(0)
    tq = q_ref.shape[0]
    lane = lax.broadcasted_iota(jnp.int32, (tq, LANES), 1)
    lo = lane < ATT_HEAD_DIM
    for h in range(ATT_HEADS):
        t = q_ref[:, (h // 2) * LANES:(h // 2 + 1) * LANES]
        qs_ref[h * tq:(h + 1) * tq] = jnp.where(lo if h % 2 == 0 else ~lo, t, jnp.zeros_like(t))

    kd_refs = (kd0_ref, kd1_ref)
    n_streams = ATT_HEADS // 2
    blocks_per_chunk = ATT_TK // TM
    nt_dims = (((1,), (1,)), ((), ()))

    def group_of(s):
        return s // (n_streams // ATT_KV_HEADS)

    def scores(s, k_start, n_blk):
        kc = kd_refs[group_of(s)][pl.ds(k_start, n_blk * TM), :]
        qs = qs_ref[2 * s * tq:(2 * s + 2) * tq]
        return lax.dot_general(kc, qs, nt_dims, preferred_element_type=F32)

    def p_times_v(s, pt, n_blk, blk0):
        pv = None
        for j in range(n_blk):
            part = jnp.dot(vt_ref[group_of(s), blk0 + j], pt[j * TM:(j + 1) * TM], preferred_element_type=F32)
            pv = part if pv is None else pv + part
        return pv

    def softmax_pv_online(s, st, n_blk, blk0):
        m_old = m_ref[s]
        m_new = jnp.maximum(m_old, jnp.max(st, axis=0, keepdims=True))
        alpha = jnp.exp2(m_old - m_new)
        pt = jnp.exp2(st - m_new).astype(BF16)
        m_ref[s] = m_new
        acc_ref[s] = alpha * acc_ref[s] + p_times_v(s, pt, n_blk, blk0)

    def softmax_pv_bounded(s, st, n_blk, blk0):
        pt = jnp.exp2(st).astype(BF16)
        acc_ref[s] += p_times_v(s, pt, n_blk, blk0)

    def sweep(softmax_pv, lookahead):
        def step(k_start, n_blk, blk0):
            pending = [scores(s, k_start, n_blk) for s in range(min(lookahead, n_streams))]
            for s in range(n_streams):
                st = pending.pop(0)
                if s + lookahead < n_streams:
                    pending.append(scores(s + lookahead, k_start, n_blk))
                softmax_pv(s, st, n_blk, blk0)

        def body(c, carry):
            step(pl.multiple_of(c * ATT_TK, ATT_TK), blocks_per_chunk, c * blocks_per_chunk)
            return carry

        n_full = jnp.where(i < n_lat_t, t_lat // ATT_TK, 0)
        lax.fori_loop(0, n_full, body, 0)
        step(t_lat, ctx_len // TM, t_lat // TM)

    acc_ref[...] = jnp.zeros(acc_ref.shape, F32)
    bounded = bounded_ref[0] == 1

    @pl.when(bounded)
    def _():
        sweep(softmax_pv_bounded, ATT_LOOKAHEAD_BOUNDED)

    @pl.when(jnp.logical_not(bounded))
    def _():
        m_ref[...] = jnp.full(m_ref.shape, -jnp.inf, F32)
        sweep(softmax_pv_online, ATT_LOOKAHEAD)

    for s in range(n_streams):
        acc = acc_ref[s]
        o_t = acc[:ATT_HEAD_DIM] * (1.0 / acc[ATT_HEAD_DIM:ATT_HEAD_DIM + 1])
        pair = jnp.concatenate([o_t[:, :tq], o_t[:, tq:]], axis=0)
        o_ref[:, s * LANES:(s + 1) * LANES] = pair.T.astype(BF16)


def _attention(q, kd0, kd1, vt, q_norm_g, k_norm_g, t_lat, n_lat_t):
    rows, qw = q.shape
    ctx_len = rows - t_lat
    score_bound = (ATT_HEAD_DIM ** 0.5 * LOG2_E) * jnp.max(jnp.abs(q_norm_g)) * jnp.max(jnp.abs(k_norm_g))
    bounded = (score_bound <= ATT_SCORE_BOUND).astype(jnp.int32).reshape(1)
    return pl.pallas_call(
        functools.partial(_attn_kernel, n_lat_t=n_lat_t, t_lat=t_lat, ctx_len=ctx_len),
        grid=(rows // TM,),
        in_specs=[pl.BlockSpec(memory_space=pltpu.SMEM),
                  pl.BlockSpec((TM, qw), lambda i: (i, 0)), _full((rows, LANES)), _full((rows, LANES)),
                  _full(vt.shape)],
        out_specs=pl.BlockSpec((TM, qw), lambda i: (i, 0)),
        out_shape=jax.ShapeDtypeStruct((rows, qw), BF16),
        scratch_shapes=[
            pltpu.VMEM((ATT_HEADS * TM, LANES), BF16),
            pltpu.VMEM((ATT_HEADS // 2, 1, 2 * TM), F32),
            pltpu.VMEM((ATT_HEADS // 2, V_ROWS, 2 * TM), F32),
        ],
        compiler_params=_cparams("arbitrary"),
        name="gqa_attention",
    )(bounded, q, kd0, kd1, vt)


def _odd_proj_kernel(h_ref, mod_ref, g_ref, win_ref, gb_ref, pqk_ref, v_ref, o_ref, gates_ref):
    x = h_ref[...]
    m = mod_ref[0]
    z = _rms_mod(x, g_ref[...], m[3:4], m[4:5]).astype(BF16)
    p = jnp.dot(z, win_ref[...], preferred_element_type=F32)
    qk = 2 * ML_HEADS * ML_QK_DIM
    vw = ML_HEADS * ML_V_DIM
    pqk_ref[...] = p[:, :qk]
    v_ref[...] = p[:, qk:qk + vw].astype(BF16)
    o_ref[...] = p[:, qk + vw:qk + 2 * vw]
    gts = p[:, qk + 2 * vw:] + gb_ref[...]
    lane = lax.broadcasted_iota(jnp.int32, gts.shape, 1)
    is_f = (lane < 4 * ML_HEADS) & ((lane // ML_HEADS) % 2 == 1)
    log_sig = jnp.minimum(gts, 0.0) - jnp.log1p(jnp.exp(-jnp.abs(gts)))
    gates_ref[...] = jnp.where(is_f, log_sig, gts)


def _odd_proj(h, mod, g, w_in, gate_b, n_lat_t):
    rows, d = h.shape
    qk = 2 * ML_HEADS * ML_QK_DIM
    vw = ML_HEADS * ML_V_DIM
    n_gate = 4 * ML_HEADS
    w_pad = jnp.pad(w_in, ((0, 0), (0, LANES - n_gate))).astype(BF16)
    gb = jnp.pad(gate_b, (0, LANES - n_gate)).reshape(1, LANES)
    row_tile = lambda w: pl.BlockSpec((TM, w), lambda i: (i, 0))
    return pl.pallas_call(
        _odd_proj_kernel,
        grid=(rows // TM,),
        in_specs=[row_tile(d), pl.BlockSpec((1, N_MOD, d), lambda i: (i // n_lat_t, 0, 0)),
                  _full((1, d)), _full(w_pad.shape), _full((1, LANES))],
        out_specs=[row_tile(qk), row_tile(vw), row_tile(vw), row_tile(LANES)],
        out_shape=[jax.ShapeDtypeStruct((rows, qk), F32), jax.ShapeDtypeStruct((rows, vw), BF16),
                   jax.ShapeDtypeStruct((rows, vw), F32), jax.ShapeDtypeStruct((rows, LANES), F32)],
        compiler_params=_cparams("arbitrary"),
        name="odd_proj",
    )(h, mod, g.reshape(1, d), w_pad, gb)


def _conv_kernel(x_ref, prev_ref, next_ref, w_ref, q_ref, k_ref, *, n_lat_t):
    i = pl.program_id(0)
    nt = pl.num_programs(0)
    x = x_ref[...]
    tm = x.shape[0]
    first = (i == 0) | (i == n_lat_t)
    last = (i == n_lat_t - 1) | (i == nt - 1)
    prev_row = jnp.where(first, 0.0, prev_ref[SUBLANES - 1:SUBLANES, :])
    next_row = jnp.where(last, 0.0, next_ref[0:1, :])
    row = lax.broadcasted_iota(jnp.int32, x.shape, 0)
    xm = jnp.where(row == 0, prev_row, pltpu.roll(x, 1, 0))
    xp = jnp.where(row == tm - 1, next_row, pltpu.roll(x, tm - 1, 0))
    y = xm * w_ref[0:1, :] + x * w_ref[1:2, :] + xp * w_ref[2:3, :]
    y = _silu(y)
    qk = ML_HEADS * ML_QK_DIM
    q_ref[...] = (y[:, :qk] * (ML_QK_DIM ** -0.5)).astype(BF16)
    k_ref[...] = y[:, qk:].astype(BF16)


def _conv(pqk, conv_w, n_lat_t):
    rows, w = pqk.shape
    per = TM // SUBLANES
    last_blk = rows // SUBLANES - 1
    return pl.pallas_call(
        functools.partial(_conv_kernel, n_lat_t=n_lat_t),
        grid=(rows // TM,),
        in_specs=[
            pl.BlockSpec((TM, w), lambda i: (i, 0)),
            pl.BlockSpec((SUBLANES, w), lambda i: (jnp.maximum(i * per - 1, 0), 0)),
            pl.BlockSpec((SUBLANES, w), lambda i: (jnp.minimum((i + 1) * per, last_blk), 0)),
            _full(conv_w.shape),
        ],
        out_specs=[pl.BlockSpec((TM, w // 2), lambda i: (i, 0))] * 2,
        out_shape=[jax.ShapeDtypeStruct((rows, w // 2), BF16)] * 2,
        compiler_params=_cparams("arbitrary"),
        name="short_conv",
    )(pqk, pqk, pqk, conv_w)


def _mlstm_prepare(d, g):
    ln = CHUNK
    r = lax.broadcasted_iota(jnp.int32, (ln, ln), 0)
    c = lax.broadcasted_iota(jnp.int32, (ln, ln), 1)
    tri = (c <= r) if d == 0 else (c >= r)
    tri_b = jnp.where(tri, 1.0, 0.0).astype(BF16)
    g_t = g.T
    nt_dims = (((1,), (1,)), ((), ()))
    bc_col = sum(jnp.dot(tri_b, piece, preferred_element_type=F32) for piece in _split3(g))
    bc_row = sum(lax.dot_general(piece, tri_b, nt_dims, preferred_element_type=F32) for piece in _split3(g_t))
    il0 = 2 * ML_HEADS * d
    fl0 = il0 + ML_HEADS
    u = g_t[il0:il0 + ML_HEADS, :] - bc_row[fl0:fl0 + ML_HEADS, :]
    lane = lax.broadcasted_iota(jnp.int32, u.shape, 1)
    sh = 1
    while sh < ln:
        if d == 0:
            shifted = jnp.where(lane >= sh, pltpu.roll(u, sh, 1), -jnp.inf)
        else:
            shifted = jnp.where(lane < ln - sh, pltpu.roll(u, ln - sh, 1), -jnp.inf)
        u = jnp.maximum(u, shifted)
        sh *= 2
    run_max_cols = jnp.concatenate([u, jnp.zeros((ln - ML_HEADS, ln), F32)], axis=0).T
    return dict(tri=tri, g=g, g_t=g_t, bc_col=bc_col, bc_row=bc_row, run_max_cols=run_max_cols,
                lane_lo=c < ML_QK_DIM, row_lo=r < ML_QK_DIM, last=ln - 1 if d == 0 else 0)


def _mlstm_head(d, h, pre, q_pair, k_pair, k_t, v_ref, o_ref, ct_ref, m_ref):
    half = h % 2
    nt_dims = (((1,), (1,)), ((), ()))
    il = 2 * ML_HEADS * d + h
    fl = il + ML_HEADS
    keep_lane = pre["lane_lo"] if half == 0 else ~pre["lane_lo"]
    keep_row = pre["row_lo"] if half == 0 else ~pre["row_lo"]
    bcol = pre["bc_col"][:, fl:fl + 1]
    brow = pre["bc_row"][fl:fl + 1, :]
    irow = pre["g_t"][il:il + 1, :]
    icol = pre["g"][:, il:il + 1]
    btot = pre["bc_col"][pre["last"]:pre["last"] + 1, fl:fl + 1]
    m_prev = m_ref[h:h + 1, 0:1]

    dmat = jnp.where(pre["tri"], bcol - brow + irow, -jnp.inf)
    inter = bcol + m_prev
    mrow = jnp.maximum(inter, bcol + pre["run_max_cols"][:, h:h + 1])
    wts = jnp.exp(dmat - mrow)
    a = jnp.exp(inter - mrow)

    qm = jnp.where(keep_lane, q_pair, jnp.zeros_like(q_pair))
    s = lax.dot_general(qm, k_pair, nt_dims, preferred_element_type=F32) * wts
    vh = v_ref[:, h * LANES:(h + 1) * LANES]
    v_aug = jnp.concatenate([vh, jnp.ones_like(vh)], axis=1)
    ct = ct_ref[h]
    num_den = (jnp.dot(s.astype(BF16), v_aug, preferred_element_type=F32)
               + a * jnp.dot(qm, ct.astype(BF16), preferred_element_type=F32))
    num = num_den[:, :ML_V_DIM]
    den = num_den[:, ML_V_DIM:]
    o_ref[:, h * LANES:(h + 1) * LANES] = num / jnp.maximum(jnp.abs(den), jnp.exp(-mrow))

    wv = btot - bcol + icol
    m_loc = jnp.max(wv, axis=0, keepdims=True)
    e = jnp.exp(wv - m_loc)
    ev = (e * v_aug.astype(F32)).astype(BF16)
    k_tm = jnp.where(keep_row, k_t, 0.0).astype(BF16)
    c_loc = jnp.dot(k_tm, ev, preferred_element_type=F32)
    m_new = jnp.maximum(btot + m_prev, m_loc)
    decay = jnp.exp(btot + m_prev - m_new)
    gain = jnp.exp(m_loc - m_new)
    ct_ref[h] = decay * ct + gain * c_loc
    m_ref[h:h + 1, :] = jnp.broadcast_to(m_new, (1, LANES))


def _mlstm_kernel(qf_ref, kf_ref, vf_ref, gf_ref, qb_ref, kb_ref, vb_ref, gb_ref, of_ref, ob_ref,
                  ct_ref, m_ref):
    @pl.when(pl.program_id(0) == 0)
    def _():
        ct_ref[...] = jnp.zeros(ct_ref.shape, F32)
        m_ref[...] = jnp.zeros(m_ref.shape, F32)

    dirs = ((qf_ref, kf_ref, vf_ref, gf_ref, of_ref), (qb_ref, kb_ref, vb_ref, gb_ref, ob_ref))

    def chunk_rows(d, step):
        c = step if d == 0 else ML_CHUNKS_PER_STEP - 1 - step
        return pl.ds(c * CHUNK, CHUNK)

    pre = [[_mlstm_prepare(d, dirs[d][3][chunk_rows(d, step), :]) for d in range(2)]
           for step in range(ML_CHUNKS_PER_STEP)]
    for step in range(ML_CHUNKS_PER_STEP):
        for pair in range(ML_HEADS // 2):
            loaded = []
            for d in range(2):
                rows = chunk_rows(d, step)
                q_pair = dirs[d][0][rows, pair * LANES:(pair + 1) * LANES]
                k_pair = dirs[d][1][rows, pair * LANES:(pair + 1) * LANES]
                loaded.append((q_pair, k_pair, k_pair.astype(F32).T))
            for half in range(2):
                for d in range(2):
                    rows = chunk_rows(d, step)
                    q_pair, k_pair, k_t = loaded[d]
                    _mlstm_head(d, 2 * pair + half, pre[step][d], q_pair, k_pair, k_t,
                                dirs[d][2].at[rows], dirs[d][4].at[rows], ct_ref.at[d], m_ref.at[d])


def _mlstm(q, k, v, gates, n_lat_ch):
    rows = q.shape[0]
    step_rows = ML_CHUNKS_PER_STEP * CHUNK
    assert rows % step_rows == 0 and (n_lat_ch * CHUNK) % step_rows == 0
    nblk = rows // step_rows
    n_lat_blk = n_lat_ch * CHUNK // step_rows
    fwd = lambda i: (i + n_lat_blk) % nblk
    bwd = lambda i: nblk - 1 - i
    vw = v.shape[1]

    def blocks(order):
        blk = lambda w: pl.BlockSpec((step_rows, w), lambda i: (order(i), 0))
        return [blk(q.shape[1]), blk(k.shape[1]), blk(vw), blk(LANES)]

    out_blk = lambda order: pl.BlockSpec((step_rows, vw), lambda i: (order(i), 0))
    return pl.pallas_call(
        _mlstm_kernel,
        grid=(nblk,),
        in_specs=blocks(fwd) + blocks(bwd),
        out_specs=[out_blk(fwd), out_blk(bwd)],
        out_shape=[jax.ShapeDtypeStruct((rows, vw), F32)] * 2,
        scratch_shapes=[
            pltpu.VMEM((2, ML_HEADS, LANES, ML_V_DIM + LANES), F32),
            pltpu.VMEM((2, ML_HEADS, LANES), F32),
        ],
        compiler_params=_cparams("arbitrary"),
        name="mlstm",
    )(q, k, v, gates, q, k, v, gates)


def kernel(x, c, ctx, c_ctx, mod_w, mod_b, norm_g, ffn_w13, ffn_w2, ab_w_in, ab_gate_norm_g, ab_spatial_w, ab_spatial_b, ab_q_norm_g, ab_k_norm_g, ab_w_out, ml_w_in, ml_conv_w, ml_gate_b, ml_out_norm_g, ml_w_out, final_norm_g):
    b, t_lat, d = x.shape
    ctx_len = ctx.shape[1]
    depth = mod_w.shape[0]
    assert b == 1 and t_lat % TM == 0 and ctx_len % TM == 0 and t_lat % ATT_TK == 0
    n_lat_t = t_lat // TM
    n_all_t = (t_lat + ctx_len) // TM

    mods = _mods(c, c_ctx, mod_w, mod_b)
    w13_all = ffn_w13.astype(BF16)
    w2_all = ffn_w2.astype(BF16)
    h = x[0]
    for l in range(depth):
        keep_ctx = l < depth - 1
        mod = mods[l]
        h = _ffn(h, mod, norm_g[l, 0], w13_all, w2_all, l, 0, n_all_t, n_lat_t, 0,
                 ctx_rows=ctx[0] if l == 0 else None)
        n_out_t = n_all_t if keep_ctx else n_lat_t
        if l % 2 == 0:
            e = l // 2
            gated, q, kd0, kd1, vt = _even_proj(
                h, mod, norm_g[l, 1], ab_w_in[e], ab_gate_norm_g[e], ab_spatial_w[e], ab_spatial_b[e],
                ab_q_norm_g[e], ab_k_norm_g[e], t_lat, n_lat_t)
            attn = _attention(q, kd0, kd1, vt, ab_q_norm_g[e], ab_k_norm_g[e], t_lat, n_lat_t)
            mixer, mixer_args = "even", (gated, attn, ab_w_out[e].astype(BF16))
        else:
            o = l // 2
            pqk, v, og, gates = _odd_proj(h, mod, norm_g[l, 1], ml_w_in[o], ml_gate_b[o], n_lat_t)
            q, k = _conv(pqk, ml_conv_w[o], n_lat_t)
            hf, hb = _mlstm(q, k, v, gates, t_lat // CHUNK)
            mixer, mixer_args = "odd", (hf, hb, og, ml_out_norm_g[o].reshape(1, -1), ml_w_out[o].astype(BF16))
        final_g = final_norm_g if l == depth - 1 else None
        h = _ffn(h, mod, norm_g[l, 2], w13_all, w2_all, l, 1, n_out_t, n_lat_t, 6, final_g,
                 mixer=mixer, mixer_args=mixer_args)
    return h[:t_lat][None]
```

```python
import functools

import jax
import jax.numpy as jnp
from jax import lax
from jax.experimental import pallas as pl
from jax.experimental.pallas import tpu as pltpu

F32 = jnp.float32
BF16 = jnp.bfloat16

LANES = 128
SUBLANES = 8
VMEM_LIMIT = 56 * 1024 * 1024

GRID_W = 64
CHUNK = 128
EPS = 1e-6
ROPE_THETA = 10000.0
LOG2_E = 1.4426950408889634
N_MOD = 9
A_GROUPS = 8
A_GROUP_DIM = 64
ATT_HEADS = 8
ATT_KV_HEADS = 2
ATT_HEAD_DIM = 64
ML_HEADS = 8
ML_QK_DIM = 64
ML_V_DIM = 128
CONV_WIDTH = 3

TM = 256
ATT_TK = 1024
ATT_LOOKAHEAD = 3
ATT_LOOKAHEAD_BOUNDED = 2
ML_AUX_ROWS = 16
ML_CHUNKS_PER_STEP = 2
ATT_SCORE_BOUND = 60.0
V_ROWS = ATT_HEAD_DIM + 16


def _cparams(*sem):
    return pltpu.CompilerParams(dimension_semantics=sem, vmem_limit_bytes=VMEM_LIMIT)


def _full(shape):
    n = len(shape)
    return pl.BlockSpec(shape, lambda *_: (0,) * n)


def _rms_mod(x, g, shift, scale):
    ms = jnp.mean(x * x, axis=-1, keepdims=True)
    y = x * lax.rsqrt(ms + EPS) * g
    return y * (1 + scale) + shift


def _silu(x):
    return x * jax.nn.sigmoid(x)


def _split3(x):
    x1 = x.astype(BF16)
    r = x - x1.astype(F32)
    x2 = r.astype(BF16)
    x3 = (r - x2.astype(F32)).astype(BF16)
    return x1, x2, x3


def _group_mean_sq(x, bmat):
    sq = x * x
    hi = sq.astype(BF16)
    lo = (sq - hi.astype(F32)).astype(BF16)
    return (jnp.dot(hi, bmat, preferred_element_type=F32)
            + jnp.dot(lo, bmat, preferred_element_type=F32))


def _mod_kernel(c_ref, w_ref, b_ref, o_ref):
    s = _silu(c_ref[...])
    o_ref[0] = jnp.dot(s, w_ref[0], preferred_element_type=F32,
                       precision=lax.Precision.HIGHEST) + b_ref[0]


def _mods(c, c_ctx, mod_w, mod_b):
    depth, d, nd = mod_w.shape
    cvec = jnp.zeros((SUBLANES, d), F32).at[0].set(c[0]).at[1].set(c_ctx)
    out = pl.pallas_call(
        _mod_kernel,
        grid=(depth, nd // d),
        in_specs=[
            pl.BlockSpec((SUBLANES, d), lambda l, j: (0, 0)),
            pl.BlockSpec((1, d, d), lambda l, j: (l, 0, j)),
            pl.BlockSpec((1, 1, d), lambda l, j: (l, 0, j)),
        ],
        out_specs=pl.BlockSpec((1, SUBLANES, d), lambda l, j: (l, 0, j)),
        out_shape=jax.ShapeDtypeStruct((depth, SUBLANES, nd), F32),
        compiler_params=_cparams("arbitrary", "arbitrary"),
        name="modulation",
    )(cvec, mod_w, mod_b.reshape(depth, 1, nd))
    return out[:, :2].reshape(depth, 2, N_MOD, d)


def _even_mixer_out(a_ref, b_ref, w_ref):
    aw = a_ref.shape[1]
    return (jnp.dot(a_ref[...], w_ref[:aw], preferred_element_type=F32)
            + jnp.dot(b_ref[...], w_ref[aw:], preferred_element_type=F32))


def _odd_mixer_out(hf_ref, hb_ref, og_ref, ng_ref, w_ref):
    hs = hf_ref[...] + hb_ref[...]
    parts = []
    for hh in range(ML_HEADS):
        blk = hs[:, hh * ML_V_DIM:(hh + 1) * ML_V_DIM]
        ms = jnp.mean(blk * blk, axis=-1, keepdims=True)
        parts.append(blk * lax.rsqrt(ms + EPS))
    hn = jnp.concatenate(parts, axis=1) * ng_ref[...]
    gated = (hn * jax.nn.sigmoid(og_ref[...])).astype(BF16)
    return jnp.dot(gated, w_ref[...], preferred_element_type=F32)


_MIXER_OUT = {"even": (_even_mixer_out, 3), "odd": (_odd_mixer_out, 5)}


def _ffn_kernel(h_ref, mod_ref, g_ref, w13_ref, w2_ref, *rest, mod_base, final, n_lat_t, split_input, mixer):
    o_ref = rest[-1]
    x = h_ref[...]
    m = mod_ref[0]
    if split_input:
        x = jnp.where(pl.program_id(0) < n_lat_t, x, rest[0][...])
        rest = rest[1:]
    if mixer is not None:
        fn, n_ops = _MIXER_OUT[mixer]
        x = x + m[5:6] * fn(*rest[:n_ops])
        rest = rest[n_ops:]
    z = _rms_mod(x, g_ref[...], m[mod_base:mod_base + 1], m[mod_base + 1:mod_base + 2]).astype(BF16)
    f = w2_ref.shape[0]
    gu = jnp.dot(z, w13_ref[...], preferred_element_type=F32)
    a = (_silu(gu[:, :f]) * gu[:, f:]).astype(BF16)
    y = jnp.dot(a, w2_ref[...], preferred_element_type=F32)
    out = x + (0.5 * m[mod_base + 2:mod_base + 3]) * y
    if final:
        ms = jnp.mean(out * out, axis=-1, keepdims=True)
        out = out * lax.rsqrt(ms + EPS) * rest[0][...]
    o_ref[...] = out


def _ffn(h, mod, g, w13_all, w2_all, layer, which, n_tiles, n_lat_t, mod_base, final_g=None, ctx_rows=None,
         mixer=None, mixer_args=()):
    d = h.shape[1]
    f = w2_all.shape[2]
    final = final_g is not None
    split_input = ctx_rows is not None
    last_lat = n_lat_t - 1
    row_tile = lambda w: pl.BlockSpec((TM, w), lambda i: (i, 0))
    in_specs = [
        pl.BlockSpec((TM, d), (lambda i: (jnp.minimum(i, last_lat), 0)) if split_input else (lambda i: (i, 0))),
        pl.BlockSpec((1, N_MOD, d), lambda i: (i // n_lat_t, 0, 0)),
        _full((1, d)),
        pl.BlockSpec((None, None, d, 2 * f), lambda i: (layer, which, 0, 0)),
        pl.BlockSpec((None, None, f, d), lambda i: (layer, which, 0, 0)),
    ]
    args = [h, mod, g.reshape(1, d), w13_all, w2_all]
    if split_input:
        in_specs.append(pl.BlockSpec((TM, d), lambda i: (jnp.maximum(i - n_lat_t, 0), 0)))
        args.append(ctx_rows)
    for arr in mixer_args:
        in_specs.append(row_tile(arr.shape[1]) if arr.shape[0] == h.shape[0] else _full(arr.shape))
        args.append(arr)
    if final:
        in_specs.append(_full((1, d)))
        args.append(final_g.reshape(1, d))
    return pl.pallas_call(
        functools.partial(_ffn_kernel, mod_base=mod_base, final=final, n_lat_t=n_lat_t, split_input=split_input,
                          mixer=mixer),
        grid=(n_tiles,),
        in_specs=in_specs,
        out_specs=row_tile(d),
        out_shape=jax.ShapeDtypeStruct((n_tiles * TM, d), F32),
        compiler_params=_cparams("arbitrary"),
        name="ffn",
    )(*args)


def _even_proj_kernel(h_ref, mod_ref, g_ref, win_ref, gng_ref, qg_ref, kg_ref, spw_ref, spb_ref,
                      cos_ref, sin_ref, b512_ref, b128_ref,
                      gated_ref, q_ref, kd0_ref, kd1_ref, vt_ref):
    x = h_ref[...]
    m = mod_ref[0]
    z = _rms_mod(x, g_ref[...], m[3:4], m[4:5]).astype(BF16)
    p = jnp.dot(z, win_ref[...], preferred_element_type=F32)
    aw = A_GROUPS * A_GROUP_DIM
    qw = ATT_HEADS * ATT_HEAD_DIM
    u = jax.nn.gelu(p[:, :aw])
    v = jax.nn.gelu(p[:, aw:2 * aw])
    b512 = b512_ref[...]
    vn = (v * lax.rsqrt(_group_mean_sq(v, b512) + EPS) * gng_ref[...]).astype(BF16)

    lane = lax.broadcasted_iota(jnp.int32, (CHUNK, LANES), 1)
    lo = lane < A_GROUP_DIM
    for c in range(TM // CHUNK):
        r0 = c * CHUNK
        for j in range(aw // LANES):
            vt = vn[r0:r0 + CHUNK, j * LANES:(j + 1) * LANES]
            s0 = jnp.dot(spw_ref[2 * j], vt, preferred_element_type=F32)
            s1 = jnp.dot(spw_ref[2 * j + 1], vt, preferred_element_type=F32)
            s = jnp.where(lo, s0, s1) + spb_ref[:, j * LANES:(j + 1) * LANES]
            gated_ref[r0:r0 + CHUNK, j * LANES:(j + 1) * LANES] = (
                u[r0:r0 + CHUNK, j * LANES:(j + 1) * LANES] * s).astype(BF16)

    o = 2 * aw
    pq = p[:, o:o + qw]
    pk = p[:, o + qw:o + qw + LANES]
    va = p[:, o + qw + LANES:o + qw + 2 * LANES]
    qn = pq * lax.rsqrt(_group_mean_sq(pq, b512) + EPS) * qg_ref[...]
    kn = pk * lax.rsqrt(_group_mean_sq(pk, b128_ref[...]) + EPS) * kg_ref[...]

    cos = cos_ref[...]
    sin = sin_ref[...]
    lane_t = lax.broadcasted_iota(jnp.int32, (TM, LANES), 1)
    first_half = (lane_t % 32) < 16
    lo_t = lane_t < ATT_HEAD_DIM

    def rope(t):
        partner = jnp.where(first_half, pltpu.roll(t, LANES - 16, 1), pltpu.roll(t, 16, 1))
        return t * cos + partner * sin

    scale = ATT_HEAD_DIM ** -0.5 * LOG2_E
    for j in range(qw // LANES):
        q_ref[:, j * LANES:(j + 1) * LANES] = (rope(qn[:, j * LANES:(j + 1) * LANES]) * scale).astype(BF16)
    kr = rope(kn)
    kswap = pltpu.roll(kr, ATT_HEAD_DIM, 1)
    kd0_ref[...] = jnp.where(lo_t, kr, kswap).astype(BF16)
    kd1_ref[...] = jnp.where(lo_t, kswap, kr).astype(BF16)
    va_t = va.T
    sub = lax.broadcasted_iota(jnp.int32, (V_ROWS - ATT_HEAD_DIM, TM), 0)
    ones_rows = jnp.where(sub == 0, 1.0, 0.0).astype(BF16)
    for g in range(ATT_KV_HEADS):
        vt_ref[g, 0, :ATT_HEAD_DIM, :] = va_t[g * ATT_HEAD_DIM:(g + 1) * ATT_HEAD_DIM].astype(BF16)
        vt_ref[g, 0, ATT_HEAD_DIM:, :] = ones_rows


def _group_matrix(width, gs):
    r = jnp.arange(width) // gs
    return jnp.where(r[:, None] == r[None, :], 1.0 / gs, 0.0).astype(BF16)


def _rope_tables(t_lat, rows):
    axis_dim = ATT_HEAD_DIM // 2
    inv_freq = ROPE_THETA ** (-jnp.arange(0, axis_dim, 2, dtype=F32) / axis_dim)
    n_row = t_lat // GRID_W
    ang_r = jnp.arange(n_row, dtype=F32)[:, None] * inv_freq
    ang_c = jnp.arange(GRID_W, dtype=F32)[:, None] * inv_freq

    def table(fn, signed):
        r = jnp.broadcast_to(fn(ang_r)[:, None, :], (n_row, GRID_W, axis_dim // 2))
        c = jnp.broadcast_to(fn(ang_c)[None, :, :], (n_row, GRID_W, axis_dim // 2))
        lo = -1.0 if signed else 1.0
        head = jnp.concatenate([lo * r, r, lo * c, c], axis=-1)
        lat = jnp.tile(head, (1, 1, LANES // ATT_HEAD_DIM)).reshape(t_lat, LANES)
        ctx = jnp.full((rows - t_lat, LANES), fn(jnp.zeros((), F32)), F32)
        return jnp.concatenate([lat, ctx], axis=0)

    return table(jnp.cos, False), table(jnp.sin, True)


def _even_proj(h, mod, g, w_in, gate_norm_g, sp_w, sp_b, q_norm_g, k_norm_g, t_lat, n_lat_t):
    rows, d = h.shape
    aw = A_GROUPS * A_GROUP_DIM
    qw = ATT_HEADS * ATT_HEAD_DIM
    cos, sin = _rope_tables(t_lat, rows)
    spb = jnp.repeat(sp_b.T, A_GROUP_DIM, axis=1)
    row_tile = lambda w: pl.BlockSpec((TM, w), lambda i: (i, 0))
    out_shape = [jax.ShapeDtypeStruct((rows, aw), BF16), jax.ShapeDtypeStruct((rows, qw), BF16)]
    out_shape += [jax.ShapeDtypeStruct((rows, LANES), BF16)] * 2
    out_shape += [jax.ShapeDtypeStruct((ATT_KV_HEADS, rows // TM, V_ROWS, TM), BF16)]
    vt_spec = pl.BlockSpec((ATT_KV_HEADS, 1, V_ROWS, TM), lambda i: (0, i, 0, 0))
    return pl.pallas_call(
        _even_proj_kernel,
        grid=(rows // TM,),
        in_specs=[
            row_tile(d),
            pl.BlockSpec((1, N_MOD, d), lambda i: (i // n_lat_t, 0, 0)),
            _full((1, d)),
            _full(w_in.shape),
            _full((1, aw)),
            _full((1, qw)),
            _full((1, LANES)),
            _full(sp_w.shape),
            _full((CHUNK, aw)),
            row_tile(LANES),
            row_tile(LANES),
            _full((aw, aw)),
            _full((LANES, LANES)),
        ],
        out_specs=[row_tile(aw), row_tile(qw), row_tile(LANES), row_tile(LANES), vt_spec],
        out_shape=out_shape,
        compiler_params=_cparams("arbitrary"),
        name="even_proj",
    )(h, mod, g.reshape(1, d), w_in.astype(BF16), gate_norm_g.reshape(1, aw),
      jnp.tile(q_norm_g, ATT_HEADS).reshape(1, qw), jnp.tile(k_norm_g, ATT_KV_HEADS).reshape(1, LANES),
      sp_w.astype(BF16), spb, cos, sin, _group_matrix(aw, A_GROUP_DIM), _group_matrix(LANES, ATT_HEAD_DIM))


def _attn_kernel(bounded_ref, q_ref, kd0_ref, kd1_ref, vt_ref, o_ref, qs_ref, m_ref, acc_ref,
                 *, n_lat_t, t_lat, ctx_len):
    i = pl.program_id(0)
    tq = q_ref.shape[0]
    lane = lax.broadcasted_iota(jnp.int32, (tq, LANES), 1)
    lo = lane < ATT_HEAD_DIM
    for h in range(ATT_HEADS):
        t = q_ref[:, (h // 2) * LANES:(h // 2 + 1) * LANES]
        qs_ref[h * tq:(h + 1) * tq] = jnp.where(lo if h % 2 == 0 else ~lo, t, jnp.zeros_like(t))

    kd_refs = (kd0_ref, kd1_ref)
    n_streams = ATT_HEADS // 2
    blocks_per_chunk = ATT_TK // TM
    nt_dims = (((1,), (1,)), ((), ()))

    def group_of(s):
        return s // (n_streams // ATT_KV_HEADS)

    def scores(s, k_start, n_blk):
        kc = kd_refs[group_of(s)][pl.ds(k_start, n_blk * TM), :]
        qs = qs_ref[2 * s * tq:(2 * s + 2) * tq]
        return lax.dot_general(kc, qs, nt_dims, preferred_element_type=F32)

    def p_times_v(s, pt, n_blk, blk0):
        pv = None
        for j in range(n_blk):
            part = jnp.dot(vt_ref[group_of(s), blk0 + j], pt[j * TM:(j + 1) * TM], preferred_element_type=F32)
            pv = part if pv is None else pv + part
        return pv

    def softmax_pv_online(s, st, n_blk, blk0):
        m_old = m_ref[s]
        m_new = jnp.maximum(m_old, jnp.max(st, axis=0, keepdims=True))
        alpha = jnp.exp2(m_old - m_new)
        pt = jnp.exp2(st - m_new).astype(BF16)
        m_ref[s] = m_new
        acc_ref[s] = alpha * acc_ref[s] + p_times_v(s, pt, n_blk, blk0)

    def softmax_pv_bounded(s, st, n_blk, blk0):
        pt = jnp.exp2(st).astype(BF16)
        acc_ref[s] += p_times_v(s, pt, n_blk, blk0)

    def sweep(softmax_pv, lookahead):
        def step(k_start, n_blk, blk0):
            pending = [scores(s, k_start, n_blk) for s in range(min(lookahead, n_streams))]
            for s in range(n_streams):
                st = pending.pop(0)
                if s + lookahead < n_streams:
                    pending.append(scores(s + lookahead, k_start, n_blk))
                softmax_pv(s, st, n_blk, blk0)

        def body(c, carry):
            step(pl.multiple_of(c * ATT_TK, ATT_TK), blocks_per_chunk, c * blocks_per_chunk)
            return carry

        n_full = jnp.where(i < n_lat_t, t_lat // ATT_TK, 0)
        lax.fori_loop(0, n_full, body, 0)
        step(t_lat, ctx_len // TM, t_lat // TM)

    acc_ref[...] = jnp.zeros(acc_ref.shape, F32)
    bounded = bounded_ref[0] == 1

    @pl.when(bounded)
    def _():
        sweep(softmax_pv_bounded, ATT_LOOKAHEAD_BOUNDED)

    @pl.when(jnp.logical_not(bounded))
    def _():
        m_ref[...] = jnp.full(m_ref.shape, -jnp.inf, F32)
        sweep(softmax_pv_online, ATT_LOOKAHEAD)

    for s in range(n_streams):
        acc = acc_ref[s]
        o_t = acc[:ATT_HEAD_DIM] * (1.0 / acc[ATT_HEAD_DIM:ATT_HEAD_DIM + 1])
        pair = jnp.concatenate([o_t[:, :tq], o_t[:, tq:]], axis=0)
        o_ref[:, s * LANES:(s + 1) * LANES] = pair.T.astype(BF16)


def _attention(q, kd0, kd1, vt, q_norm_g, k_norm_g, t_lat, n_lat_t):
    rows, qw = q.shape
    ctx_len = rows - t_lat
    score_bound = (ATT_HEAD_DIM ** 0.5 * LOG2_E) * jnp.max(jnp.abs(q_norm_g)) * jnp.max(jnp.abs(k_norm_g))
    bounded = (score_bound <= ATT_SCORE_BOUND).astype(jnp.int32).reshape(1)
    return pl.pallas_call(
        functools.partial(_attn_kernel, n_lat_t=n_lat_t, t_lat=t_lat, ctx_len=ctx_len),
        grid=(rows // TM,),
        in_specs=[pl.BlockSpec(memory_space=pltpu.SMEM),
                  pl.BlockSpec((TM, qw), lambda i: (i, 0)), _full((rows, LANES)), _full((rows, LANES)),
                  _full(vt.shape)],
        out_specs=pl.BlockSpec((TM, qw), lambda i: (i, 0)),
        out_shape=jax.ShapeDtypeStruct((rows, qw), BF16),
        scratch_shapes=[
            pltpu.VMEM((ATT_HEADS * TM, LANES), BF16),
            pltpu.VMEM((ATT_HEADS // 2, 1, 2 * TM), F32),
            pltpu.VMEM((ATT_HEADS // 2, V_ROWS, 2 * TM), F32),
        ],
        compiler_params=_cparams("arbitrary"),
        name="gqa_attention",
    )(bounded, q, kd0, kd1, vt)


def _odd_proj_kernel(h_ref, mod_ref, g_ref, win_ref, gb_ref, pqk_ref, vt_ref, o_ref, gates_ref):
    x = h_ref[...]
    m = mod_ref[0]
    z = _rms_mod(x, g_ref[...], m[3:4], m[4:5]).astype(BF16)
    p = jnp.dot(z, win_ref[...], preferred_element_type=F32)
    qk = 2 * ML_HEADS * ML_QK_DIM
    vw = ML_HEADS * ML_V_DIM
    pqk_ref[...] = p[:, :qk]
    for c in range(TM // CHUNK):
        vt_ref[c] = p[c * CHUNK:(c + 1) * CHUNK, qk:qk + vw].T.astype(BF16)
    o_ref[...] = p[:, qk + vw:qk + 2 * vw]
    gts = p[:, qk + 2 * vw:] + gb_ref[...]
    lane = lax.broadcasted_iota(jnp.int32, gts.shape, 1)
    is_f = (lane < 4 * ML_HEADS) & ((lane // ML_HEADS) % 2 == 1)
    log_sig = jnp.minimum(gts, 0.0) - jnp.log1p(jnp.exp(-jnp.abs(gts)))
    gates_ref[...] = jnp.where(is_f, log_sig, gts)


def _odd_proj(h, mod, g, w_in, gate_b, n_lat_t):
    rows, d = h.shape
    qk = 2 * ML_HEADS * ML_QK_DIM
    vw = ML_HEADS * ML_V_DIM
    n_gate = 4 * ML_HEADS
    w_pad = jnp.pad(w_in, ((0, 0), (0, LANES - n_gate))).astype(BF16)
    gb = jnp.pad(gate_b, (0, LANES - n_gate)).reshape(1, LANES)
    row_tile = lambda w: pl.BlockSpec((TM, w), lambda i: (i, 0))
    return pl.pallas_call(
        _odd_proj_kernel,
        grid=(rows // TM,),
        in_specs=[row_tile(d), pl.BlockSpec((1, N_MOD, d), lambda i: (i // n_lat_t, 0, 0)),
                  _full((1, d)), _full(w_pad.shape), _full((1, LANES))],
        out_specs=[row_tile(qk), pl.BlockSpec((TM // CHUNK, vw, CHUNK), lambda i: (i, 0, 0)),
                   row_tile(vw), row_tile(LANES)],
        out_shape=[jax.ShapeDtypeStruct((rows, qk), F32), jax.ShapeDtypeStruct((rows // CHUNK, vw, CHUNK), BF16),
                   jax.ShapeDtypeStruct((rows, vw), F32), jax.ShapeDtypeStruct((rows, LANES), F32)],
        compiler_params=_cparams("arbitrary"),
        name="odd_proj",
    )(h, mod, g.reshape(1, d), w_pad, gb)


def _conv_kernel(x_ref, prev_ref, next_ref, w_ref, q_ref, k_ref, *, n_lat_t):
    i = pl.program_id(0)
    nt = pl.num_programs(0)
    x = x_ref[...]
    tm = x.shape[0]
    first = (i == 0) | (i == n_lat_t)
    last = (i == n_lat_t - 1) | (i == nt - 1)
    prev_row = jnp.where(first, 0.0, prev_ref[SUBLANES - 1:SUBLANES, :])
    next_row = jnp.where(last, 0.0, next_ref[0:1, :])
    row = lax.broadcasted_iota(jnp.int32, x.shape, 0)
    xm = jnp.where(row == 0, prev_row, pltpu.roll(x, 1, 0))
    xp = jnp.where(row == tm - 1, next_row, pltpu.roll(x, tm - 1, 0))
    y = xm * w_ref[0:1, :] + x * w_ref[1:2, :] + xp * w_ref[2:3, :]
    y = _silu(y)
    qk = ML_HEADS * ML_QK_DIM
    q_ref[...] = (y[:, :qk] * (ML_QK_DIM ** -0.5)).astype(BF16)
    k_ref[...] = y[:, qk:].astype(BF16)


def _conv(pqk, conv_w, n_lat_t):
    rows, w = pqk.shape
    per = TM // SUBLANES
    last_blk = rows // SUBLANES - 1
    return pl.pallas_call(
        functools.partial(_conv_kernel, n_lat_t=n_lat_t),
        grid=(rows // TM,),
        in_specs=[
            pl.BlockSpec((TM, w), lambda i: (i, 0)),
            pl.BlockSpec((SUBLANES, w), lambda i: (jnp.maximum(i * per - 1, 0), 0)),
            pl.BlockSpec((SUBLANES, w), lambda i: (jnp.minimum((i + 1) * per, last_blk), 0)),
            _full(conv_w.shape),
        ],
        out_specs=[pl.BlockSpec((TM, w // 2), lambda i: (i, 0))] * 2,
        out_shape=[jax.ShapeDtypeStruct((rows, w // 2), BF16)] * 2,
        compiler_params=_cparams("arbitrary"),
        name="short_conv",
    )(pqk, pqk, pqk, conv_w)


def _mlstm_prepare(d, g):
    ln = CHUNK
    r = lax.broadcasted_iota(jnp.int32, (ln, ln), 0)
    c = lax.broadcasted_iota(jnp.int32, (ln, ln), 1)
    tri = (c <= r) if d == 0 else (c >= r)
    tri_b = jnp.where(tri, 1.0, 0.0).astype(BF16)
    g_t = g.T
    nt_dims = (((1,), (1,)), ((), ()))
    bc_col = sum(jnp.dot(tri_b, piece, preferred_element_type=F32) for piece in _split3(g))
    bc_row = sum(lax.dot_general(piece, tri_b, nt_dims, preferred_element_type=F32) for piece in _split3(g_t))
    il0 = 2 * ML_HEADS * d
    fl0 = il0 + ML_HEADS
    u = g_t[il0:il0 + ML_HEADS, :] - bc_row[fl0:fl0 + ML_HEADS, :]
    lane = lax.broadcasted_iota(jnp.int32, u.shape, 1)
    sh = 1
    while sh < ln:
        if d == 0:
            shifted = jnp.where(lane >= sh, pltpu.roll(u, sh, 1), -jnp.inf)
        else:
            shifted = jnp.where(lane < ln - sh, pltpu.roll(u, ln - sh, 1), -jnp.inf)
        u = jnp.maximum(u, shifted)
        sh *= 2
    vis = (c >= r) if d == 0 else (c <= r)
    return dict(vis=vis, g=g, g_t=g_t, bc_col=bc_col, bc_row=bc_row, run_max=u,
                lane_lo=c < ML_QK_DIM, last=ln - 1 if d == 0 else 0)


def _mlstm_head(d, h, pre, q_pair, k_pair, vt_ref, o_ref, c_ref, m_ref):
    half = h % 2
    nt_dims = (((1,), (1,)), ((), ()))
    il = 2 * ML_HEADS * d + h
    fl = il + ML_HEADS
    keep_lane = pre["lane_lo"] if half == 0 else ~pre["lane_lo"]
    brow = pre["bc_row"][fl:fl + 1, :]
    irow = pre["g_t"][il:il + 1, :]
    src_col = pre["g"][:, il:il + 1] - pre["bc_col"][:, fl:fl + 1]
    btot = pre["bc_col"][pre["last"]:pre["last"] + 1, fl:fl + 1]
    m_prev = m_ref[h:h + 1, 0:1]

    inter = brow + m_prev
    m_row = jnp.maximum(inter, brow + pre["run_max"][h:h + 1, :])
    wts_t = jnp.exp(jnp.where(pre["vis"], (brow - m_row) + src_col, -jnp.inf))
    a = jnp.exp(inter - m_row)

    qm = jnp.where(keep_lane, q_pair, jnp.zeros_like(q_pair))
    s_t = lax.dot_general(k_pair, qm, nt_dims, preferred_element_type=F32) * wts_t
    vt = vt_ref[h * ML_V_DIM:(h + 1) * ML_V_DIM, :]
    vt_aug = jnp.concatenate([vt, jnp.ones((ML_AUX_ROWS, vt.shape[1]), BF16)], axis=0)
    c_aug = c_ref[h]
    num_den = (jnp.dot(vt_aug, s_t.astype(BF16), preferred_element_type=F32)
               + a * lax.dot_general(c_aug.astype(BF16), qm, nt_dims, preferred_element_type=F32))
    den = num_den[ML_V_DIM:ML_V_DIM + 1]
    h_t = num_den[:ML_V_DIM] / jnp.maximum(jnp.abs(den), jnp.exp(-m_row))
    o_ref[:, h * LANES:(h + 1) * LANES] = h_t.T

    w_row = btot - brow + irow
    m_loc = jnp.max(w_row, axis=1, keepdims=True)
    e = jnp.exp(w_row - m_loc)
    ev_t = (vt_aug.astype(F32) * e).astype(BF16)
    km = jnp.where(keep_lane, k_pair, jnp.zeros_like(k_pair))
    c_loc = jnp.dot(ev_t, km, preferred_element_type=F32)
    m_new = jnp.maximum(btot + m_prev, m_loc)
    decay = jnp.exp(btot + m_prev - m_new)
    gain = jnp.exp(m_loc - m_new)
    c_ref[h] = decay * c_aug + gain * c_loc
    m_ref[h:h + 1, :] = jnp.broadcast_to(m_new, (1, LANES))


def _mlstm_kernel(qf_ref, kf_ref, vf_ref, gf_ref, qb_ref, kb_ref, vb_ref, gb_ref, of_ref, ob_ref,
                  ct_ref, m_ref):
    @pl.when(pl.program_id(0) == 0)
    def _():
        ct_ref[...] = jnp.zeros(ct_ref.shape, F32)
        m_ref[...] = jnp.zeros(m_ref.shape, F32)

    dirs = ((qf_ref, kf_ref, vf_ref, gf_ref, of_ref), (qb_ref, kb_ref, vb_ref, gb_ref, ob_ref))

    def chunk_of(d, step):
        return step if d == 0 else ML_CHUNKS_PER_STEP - 1 - step

    def chunk_rows(d, step):
        return pl.ds(chunk_of(d, step) * CHUNK, CHUNK)

    pre = [[_mlstm_prepare(d, dirs[d][3][chunk_rows(d, step), :]) for d in range(2)]
           for step in range(ML_CHUNKS_PER_STEP)]
    for step in range(ML_CHUNKS_PER_STEP):
        for pair in range(ML_HEADS // 2):
            loaded = []
            for d in range(2):
                rows = chunk_rows(d, step)
                loaded.append((dirs[d][0][rows, pair * LANES:(pair + 1) * LANES],
                               dirs[d][1][rows, pair * LANES:(pair + 1) * LANES]))
            for half in range(2):
                for d in range(2):
                    q_pair, k_pair = loaded[d]
                    _mlstm_head(d, 2 * pair + half, pre[step][d], q_pair, k_pair,
                                dirs[d][2].at[chunk_of(d, step)], dirs[d][4].at[chunk_rows(d, step)],
                                ct_ref.at[d], m_ref.at[d])


def _mlstm(q, k, vt, gates, n_lat_ch):
    rows = q.shape[0]
    step_rows = ML_CHUNKS_PER_STEP * CHUNK
    assert rows % step_rows == 0 and (n_lat_ch * CHUNK) % step_rows == 0
    nblk = rows // step_rows
    n_lat_blk = n_lat_ch * CHUNK // step_rows
    fwd = lambda i: (i + n_lat_blk) % nblk
    bwd = lambda i: nblk - 1 - i
    vw = vt.shape[1]

    def blocks(order):
        blk = lambda w: pl.BlockSpec((step_rows, w), lambda i: (order(i), 0))
        vt_blk = pl.BlockSpec((ML_CHUNKS_PER_STEP, vw, CHUNK), lambda i: (order(i), 0, 0))
        return [blk(q.shape[1]), blk(k.shape[1]), vt_blk, blk(LANES)]

    out_blk = lambda order: pl.BlockSpec((step_rows, vw), lambda i: (order(i), 0))
    return pl.pallas_call(
        _mlstm_kernel,
        grid=(nblk,),
        in_specs=blocks(fwd) + blocks(bwd),
        out_specs=[out_blk(fwd), out_blk(bwd)],
        out_shape=[jax.ShapeDtypeStruct((rows, vw), F32)] * 2,
        scratch_shapes=[
            pltpu.VMEM((2, ML_HEADS, ML_V_DIM + ML_AUX_ROWS, LANES), F32),
            pltpu.VMEM((2, ML_HEADS, LANES), F32),
        ],
        compiler_params=_cparams("arbitrary"),
        name="mlstm",
    )(q, k, vt, gates, q, k, vt, gates)


def kernel(x, c, ctx, c_ctx, mod_w, mod_b, norm_g, ffn_w13, ffn_w2, ab_w_in, ab_gate_norm_g, ab_spatial_w, ab_spatial_b, ab_q_norm_g, ab_k_norm_g, ab_w_out, ml_w_in, ml_conv_w, ml_gate_b, ml_out_norm_g, ml_w_out, final_norm_g):
    b, t_lat, d = x.shape
    ctx_len = ctx.shape[1]
    depth = mod_w.shape[0]
    assert b == 1 and t_lat % TM == 0 and ctx_len % TM == 0 and t_lat % ATT_TK == 0
    n_lat_t = t_lat // TM
    n_all_t = (t_lat + ctx_len) // TM

    mods = _mods(c, c_ctx, mod_w, mod_b)
    w13_all = ffn_w13.astype(BF16)
    w2_all = ffn_w2.astype(BF16)
    h = x[0]
    for l in range(depth):
        keep_ctx = l < depth - 1
        mod = mods[l]
        h = _ffn(h, mod, norm_g[l, 0], w13_all, w2_all, l, 0, n_all_t, n_lat_t, 0,
                 ctx_rows=ctx[0] if l == 0 else None)
        n_out_t = n_all_t if keep_ctx else n_lat_t
        if l % 2 == 0:
            e = l // 2
            gated, q, kd0, kd1, vt = _even_proj(
                h, mod, norm_g[l, 1], ab_w_in[e], ab_gate_norm_g[e], ab_spatial_w[e], ab_spatial_b[e],
                ab_q_norm_g[e], ab_k_norm_g[e], t_lat, n_lat_t)
            attn = _attention(q, kd0, kd1, vt, ab_q_norm_g[e], ab_k_norm_g[e], t_lat, n_lat_t)
            mixer, mixer_args = "even", (gated, attn, ab_w_out[e].astype(BF16))
        else:
            o = l // 2
            pqk, vt, og, gates = _odd_proj(h, mod, norm_g[l, 1], ml_w_in[o], ml_gate_b[o], n_lat_t)
            q, k = _conv(pqk, ml_conv_w[o], n_lat_t)
            hf, hb = _mlstm(q, k, vt, gates, t_lat // CHUNK)
            mixer, mixer_args = "odd", (hf, hb, og, ml_out_norm_g[o].reshape(1, -1), ml_w_out[o].astype(BF16))
        final_g = final_norm_g if l == depth - 1 else None
        h = _ffn(h, mod, norm_g[l, 2], w13_all, w2_all, l, 1, n_out_t, n_lat_t, 6, final_g,
                 mixer=mixer, mixer_args=mixer_args)
    return h[:t_lat][None]
```

```python
import functools

import jax
import jax.numpy as jnp
from jax import lax
from jax.experimental import pallas as pl
from jax.experimental.pallas import tpu as pltpu

F32 = jnp.float32
BF16 = jnp.bfloat16

LANES = 128
SUBLANES = 8
VMEM_LIMIT = 56 * 1024 * 1024

GRID_W = 64
CHUNK = 128
EPS = 1e-6
ROPE_THETA = 10000.0
LOG2_E = 1.4426950408889634
N_MOD = 9
A_GROUPS = 8
A_GROUP_DIM = 64
ATT_HEADS = 8
ATT_KV_HEADS = 2
ATT_HEAD_DIM = 64
ML_HEADS = 8
ML_QK_DIM = 64
ML_V_DIM = 128
CONV_WIDTH = 3

TM = 256
ATT_TK = 4096
ATT_LOOKAHEAD = 3
ATT_LOOKAHEAD_BOUNDED = 2
ML_AUX_ROWS = 16
ML_CHUNKS_PER_STEP = 2
ATT_SCORE_BOUND = 60.0
V_ROWS = ATT_HEAD_DIM + 16


def _cparams(*sem):
    return pltpu.CompilerParams(dimension_semantics=sem, vmem_limit_bytes=VMEM_LIMIT)


def _full(shape):
    n = len(shape)
    return pl.BlockSpec(shape, lambda *_: (0,) * n)


def _rms_mod(x, g, shift, scale):
    ms = jnp.mean(x * x, axis=-1, keepdims=True)
    y = x * lax.rsqrt(ms + EPS) * g
    return y * (1 + scale) + shift


def _silu(x):
    return x * jax.nn.sigmoid(x)


def _split3(x):
    x1 = x.astype(BF16)
    r = x - x1.astype(F32)
    x2 = r.astype(BF16)
    x3 = (r - x2.astype(F32)).astype(BF16)
    return x1, x2, x3


def _group_mean_sq(x, bmat):
    sq = x * x
    hi = sq.astype(BF16)
    lo = (sq - hi.astype(F32)).astype(BF16)
    return (jnp.dot(hi, bmat, preferred_element_type=F32)
            + jnp.dot(lo, bmat, preferred_element_type=F32))


def _mod_kernel(c_ref, w_ref, b_ref, o_ref):
    s = _silu(c_ref[...])
    o_ref[0] = jnp.dot(s, w_ref[0], preferred_element_type=F32,
                       precision=lax.Precision.HIGHEST) + b_ref[0]


def _mods(c, c_ctx, mod_w, mod_b):
    depth, d, nd = mod_w.shape
    cvec = jnp.zeros((SUBLANES, d), F32).at[0].set(c[0]).at[1].set(c_ctx)
    out = pl.pallas_call(
        _mod_kernel,
        grid=(depth, nd // d),
        in_specs=[
            pl.BlockSpec((SUBLANES, d), lambda l, j: (0, 0)),
            pl.BlockSpec((1, d, d), lambda l, j: (l, 0, j)),
            pl.BlockSpec((1, 1, d), lambda l, j: (l, 0, j)),
        ],
        out_specs=pl.BlockSpec((1, SUBLANES, d), lambda l, j: (l, 0, j)),
        out_shape=jax.ShapeDtypeStruct((depth, SUBLANES, nd), F32),
        compiler_params=_cparams("arbitrary", "arbitrary"),
        name="modulation",
    )(cvec, mod_w, mod_b.reshape(depth, 1, nd))
    return out[:, :2].reshape(depth, 2, N_MOD, d)


def _even_mixer_out(a_ref, b_ref, w_ref):
    aw = a_ref.shape[1]
    return (jnp.dot(a_ref[...], w_ref[:aw], preferred_element_type=F32)
            + jnp.dot(b_ref[...], w_ref[aw:], preferred_element_type=F32))


def _odd_mixer_out(hf_ref, hb_ref, og_ref, ng_ref, w_ref):
    hs = hf_ref[...] + hb_ref[...]
    parts = []
    for hh in range(ML_HEADS):
        blk = hs[:, hh * ML_V_DIM:(hh + 1) * ML_V_DIM]
        ms = jnp.mean(blk * blk, axis=-1, keepdims=True)
        parts.append(blk * lax.rsqrt(ms + EPS))
    hn = jnp.concatenate(parts, axis=1) * ng_ref[...]
    gated = (hn * jax.nn.sigmoid(og_ref[...])).astype(BF16)
    return jnp.dot(gated, w_ref[...], preferred_element_type=F32)


_MIXER_OUT = {"even": (_even_mixer_out, 3), "odd": (_odd_mixer_out, 5)}


def _ffn_kernel(h_ref, mod_ref, g_ref, w13_ref, w2_ref, *rest, mod_base, final, n_lat_t, split_input, mixer):
    o_ref = rest[-1]
    x = h_ref[...]
    m = mod_ref[0]
    if split_input:
        x = jnp.where(pl.program_id(0) < n_lat_t, x, rest[0][...])
        rest = rest[1:]
    if mixer is not None:
        fn, n_ops = _MIXER_OUT[mixer]
        x = x + m[5:6] * fn(*rest[:n_ops])
        rest = rest[n_ops:]
    z = _rms_mod(x, g_ref[...], m[mod_base:mod_base + 1], m[mod_base + 1:mod_base + 2]).astype(BF16)
    f = w2_ref.shape[0]
    gu = jnp.dot(z, w13_ref[...], preferred_element_type=F32)
    a = (_silu(gu[:, :f]) * gu[:, f:]).astype(BF16)
    y = jnp.dot(a, w2_ref[...], preferred_element_type=F32)
    out = x + (0.5 * m[mod_base + 2:mod_base + 3]) * y
    if final:
        ms = jnp.mean(out * out, axis=-1, keepdims=True)
        out = out * lax.rsqrt(ms + EPS) * rest[0][...]
    o_ref[...] = out


def _ffn(h, mod, g, w13_all, w2_all, layer, which, n_tiles, n_lat_t, mod_base, final_g=None, ctx_rows=None,
         mixer=None, mixer_args=()):
    d = h.shape[1]
    f = w2_all.shape[2]
    final = final_g is not None
    split_input = ctx_rows is not None
    last_lat = n_lat_t - 1
    row_tile = lambda w: pl.BlockSpec((TM, w), lambda i: (i, 0))
    in_specs = [
        pl.BlockSpec((TM, d), (lambda i: (jnp.minimum(i, last_lat), 0)) if split_input else (lambda i: (i, 0))),
        pl.BlockSpec((1, N_MOD, d), lambda i: (i // n_lat_t, 0, 0)),
        _full((1, d)),
        pl.BlockSpec((None, None, d, 2 * f), lambda i: (layer, which, 0, 0)),
        pl.BlockSpec((None, None, f, d), lambda i: (layer, which, 0, 0)),
    ]
    args = [h, mod, g.reshape(1, d), w13_all, w2_all]
    if split_input:
        in_specs.append(pl.BlockSpec((TM, d), lambda i: (jnp.maximum(i - n_lat_t, 0), 0)))
        args.append(ctx_rows)
    for arr in mixer_args:
        in_specs.append(row_tile(arr.shape[1]) if arr.shape[0] == h.shape[0] else _full(arr.shape))
        args.append(arr)
    if final:
        in_specs.append(_full((1, d)))
        args.append(final_g.reshape(1, d))
    return pl.pallas_call(
        functools.partial(_ffn_kernel, mod_base=mod_base, final=final, n_lat_t=n_lat_t, split_input=split_input,
                          mixer=mixer),
        grid=(n_tiles,),
        in_specs=in_specs,
        out_specs=row_tile(d),
        out_shape=jax.ShapeDtypeStruct((n_tiles * TM, d), F32),
        compiler_params=_cparams("arbitrary"),
        name="ffn",
    )(*args)


def _even_proj_kernel(h_ref, mod_ref, g_ref, win_ref, gng_ref, qg_ref, kg_ref, spw_ref, spb_ref,
                      cos_r_ref, sin_r_ref, cos_c_ref, sin_c_ref, b512_ref, b128_ref,
                      gated_ref, q_ref, kd0_ref, kd1_ref, vt_ref, *, n_lat_t):
    x = h_ref[...]
    m = mod_ref[0]
    z = _rms_mod(x, g_ref[...], m[3:4], m[4:5]).astype(BF16)
    p = jnp.dot(z, win_ref[...], preferred_element_type=F32)
    aw = A_GROUPS * A_GROUP_DIM
    qw = ATT_HEADS * ATT_HEAD_DIM
    u = jax.nn.gelu(p[:, :aw])
    v = jax.nn.gelu(p[:, aw:2 * aw])
    b512 = b512_ref[...]
    vn = (v * lax.rsqrt(_group_mean_sq(v, b512) + EPS) * gng_ref[...]).astype(BF16)

    lane = lax.broadcasted_iota(jnp.int32, (CHUNK, LANES), 1)
    lo = lane < A_GROUP_DIM
    for c in range(TM // CHUNK):
        r0 = c * CHUNK
        for j in range(aw // LANES):
            vt = vn[r0:r0 + CHUNK, j * LANES:(j + 1) * LANES]
            s0 = jnp.dot(spw_ref[2 * j], vt, preferred_element_type=F32)
            s1 = jnp.dot(spw_ref[2 * j + 1], vt, preferred_element_type=F32)
            s = jnp.where(lo, s0, s1) + spb_ref[:, j * LANES:(j + 1) * LANES]
            gated_ref[r0:r0 + CHUNK, j * LANES:(j + 1) * LANES] = (
                u[r0:r0 + CHUNK, j * LANES:(j + 1) * LANES] * s).astype(BF16)

    o = 2 * aw
    pq = p[:, o:o + qw]
    pk = p[:, o + qw:o + qw + LANES]
    va = p[:, o + qw + LANES:o + qw + 2 * LANES]
    qn = pq * lax.rsqrt(_group_mean_sq(pq, b512) + EPS) * qg_ref[...]
    kn = pk * lax.rsqrt(_group_mean_sq(pk, b128_ref[...]) + EPS) * kg_ref[...]

    is_lat = pl.program_id(0) < n_lat_t
    lane_g = lax.broadcasted_iota(jnp.int32, (GRID_W, LANES), 1)
    row_lanes = (lane_g % ATT_HEAD_DIM) < ATT_HEAD_DIM // 2

    def tile_table(r_ref, c_ref, ctx_value):
        col_part = jnp.where(is_lat, c_ref[...], ctx_value)
        return jnp.concatenate([jnp.where(row_lanes, r_ref[gr:gr + 1, :], col_part)
                                for gr in range(TM // GRID_W)], axis=0)

    cos = tile_table(cos_r_ref, cos_c_ref, 1.0)
    sin = tile_table(sin_r_ref, sin_c_ref, 0.0)
    lane_t = lax.broadcasted_iota(jnp.int32, (TM, LANES), 1)
    first_half = (lane_t % 32) < 16
    lo_t = lane_t < ATT_HEAD_DIM

    def rope(t):
        partner = jnp.where(first_half, pltpu.roll(t, LANES - 16, 1), pltpu.roll(t, 16, 1))
        return t * cos + partner * sin

    scale = ATT_HEAD_DIM ** -0.5 * LOG2_E
    for j in range(qw // LANES):
        q_ref[:, j * LANES:(j + 1) * LANES] = (rope(qn[:, j * LANES:(j + 1) * LANES]) * scale).astype(BF16)
    kr = rope(kn)
    kswap = pltpu.roll(kr, ATT_HEAD_DIM, 1)
    kd0_ref[...] = jnp.where(lo_t, kr, kswap).astype(BF16)
    kd1_ref[...] = jnp.where(lo_t, kswap, kr).astype(BF16)
    va_t = va.T
    sub = lax.broadcasted_iota(jnp.int32, (V_ROWS - ATT_HEAD_DIM, TM), 0)
    ones_rows = jnp.where(sub == 0, 1.0, 0.0).astype(BF16)
    for g in range(ATT_KV_HEADS):
        vt_ref[g, 0, :ATT_HEAD_DIM, :] = va_t[g * ATT_HEAD_DIM:(g + 1) * ATT_HEAD_DIM].astype(BF16)
        vt_ref[g, 0, ATT_HEAD_DIM:, :] = ones_rows


def _group_matrix(width, gs):
    r = jnp.arange(width) // gs
    return jnp.where(r[:, None] == r[None, :], 1.0 / gs, 0.0).astype(BF16)


def _rope_tables(t_lat, rows):
    axis_dim = ATT_HEAD_DIM // 2
    n_freq = axis_dim // 2
    inv_freq = ROPE_THETA ** (-jnp.arange(0, axis_dim, 2, dtype=F32) / axis_dim)
    per_tile = TM // GRID_W
    n_lat_t = t_lat // TM
    n_ctx_t = (rows - t_lat) // TM
    lane = jnp.arange(LANES)
    sign = jnp.where((lane % axis_dim) < n_freq, -1.0, 1.0)
    ang_r = jnp.arange(t_lat // GRID_W, dtype=F32)[:, None] * inv_freq[lane % n_freq][None, :]
    ang_r = jnp.pad(ang_r.reshape(n_lat_t, per_tile, LANES), ((0, n_ctx_t), (0, SUBLANES - per_tile), (0, 0)))
    ang_r = ang_r.reshape((n_lat_t + n_ctx_t) * SUBLANES, LANES)
    ang_c = jnp.arange(GRID_W, dtype=F32)[:, None] * inv_freq[lane % n_freq][None, :]
    return jnp.cos(ang_r), jnp.sin(ang_r) * sign, jnp.cos(ang_c), jnp.sin(ang_c) * sign


def _even_proj(h, mod, g, w_in, gate_norm_g, sp_w, sp_b, q_norm_g, k_norm_g, t_lat, n_lat_t):
    rows, d = h.shape
    aw = A_GROUPS * A_GROUP_DIM
    qw = ATT_HEADS * ATT_HEAD_DIM
    cos_r, sin_r, cos_c, sin_c = _rope_tables(t_lat, rows)
    spb = jnp.repeat(sp_b.T, A_GROUP_DIM, axis=1)
    row_tile = lambda w: pl.BlockSpec((TM, w), lambda i: (i, 0))
    out_shape = [jax.ShapeDtypeStruct((rows, aw), BF16), jax.ShapeDtypeStruct((rows, qw), BF16)]
    out_shape += [jax.ShapeDtypeStruct((rows, LANES), BF16)] * 2
    out_shape += [jax.ShapeDtypeStruct((ATT_KV_HEADS, rows // TM, V_ROWS, TM), BF16)]
    vt_spec = pl.BlockSpec((ATT_KV_HEADS, 1, V_ROWS, TM), lambda i: (0, i, 0, 0))
    return pl.pallas_call(
        functools.partial(_even_proj_kernel, n_lat_t=n_lat_t),
        grid=(rows // TM,),
        in_specs=[
            row_tile(d),
            pl.BlockSpec((1, N_MOD, d), lambda i: (i // n_lat_t, 0, 0)),
            _full((1, d)),
            _full(w_in.shape),
            _full((1, aw)),
            _full((1, qw)),
            _full((1, LANES)),
            _full(sp_w.shape),
            _full((CHUNK, aw)),
            pl.BlockSpec((SUBLANES, LANES), lambda i: (i, 0)),
            pl.BlockSpec((SUBLANES, LANES), lambda i: (i, 0)),
            _full((GRID_W, LANES)),
            _full((GRID_W, LANES)),
            _full((aw, aw)),
            _full((LANES, LANES)),
        ],
        out_specs=[row_tile(aw), row_tile(qw), row_tile(LANES), row_tile(LANES), vt_spec],
        out_shape=out_shape,
        compiler_params=_cparams("arbitrary"),
        name="even_proj",
    )(h, mod, g.reshape(1, d), w_in.astype(BF16), gate_norm_g.reshape(1, aw),
      jnp.tile(q_norm_g, ATT_HEADS).reshape(1, qw), jnp.tile(k_norm_g, ATT_KV_HEADS).reshape(1, LANES),
      sp_w.astype(BF16), spb, cos_r, sin_r, cos_c, sin_c,
      _group_matrix(aw, A_GROUP_DIM), _group_matrix(LANES, ATT_HEAD_DIM))


def _attn_kernel(bounded_ref, q_ref, kd0_ref, kd1_ref, vt_ref, o_ref, qs_ref, m_ref, acc_ref,
                 *, n_lat_t, t_lat, ctx_len):
    i = pl.program_id(0)
    tq = q_ref.shape[0]
    lane = lax.broadcasted_iota(jnp.int32, (tq, LANES), 1)
    lo = lane < ATT_HEAD_DIM
    for h in range(ATT_HEADS):
        t = q_ref[:, (h // 2) * LANES:(h // 2 + 1) * LANES]
        qs_ref[h * tq:(h + 1) * tq] = jnp.where(lo if h % 2 == 0 else ~lo, t, jnp.zeros_like(t))

    kd_refs = (kd0_ref, kd1_ref)
    n_streams = ATT_HEADS // 2
    blocks_per_chunk = ATT_TK // TM
    nt_dims = (((1,), (1,)), ((), ()))

    def group_of(s):
        return s // (n_streams // ATT_KV_HEADS)

    def scores(s, k_start, n_blk):
        kc = kd_refs[group_of(s)][pl.ds(k_start, n_blk * TM), :]
        qs = qs_ref[2 * s * tq:(2 * s + 2) * tq]
        return lax.dot_general(kc, qs, nt_dims, preferred_element_type=F32)

    def p_times_v(s, pt, n_blk, blk0):
        pv = None
        for j in range(n_blk):
            part = jnp.dot(vt_ref[group_of(s), blk0 + j], pt[j * TM:(j + 1) * TM], preferred_element_type=F32)
            pv = part if pv is None else pv + part
        return pv

    def softmax_pv_online(s, st, n_blk, blk0):
        m_old = m_ref[s]
        m_new = jnp.maximum(m_old, jnp.max(st, axis=0, keepdims=True))
        alpha = jnp.exp2(m_old - m_new)
        pt = jnp.exp2(st - m_new).astype(BF16)
        m_ref[s] = m_new
        acc_ref[s] = alpha * acc_ref[s] + p_times_v(s, pt, n_blk, blk0)

    def softmax_pv_bounded(s, st, n_blk, blk0):
        pt = jnp.exp2(st).astype(BF16)
        acc_ref[s] += p_times_v(s, pt, n_blk, blk0)

    def sweep(softmax_pv, lookahead):
        def step(k_start, n_blk, blk0):
            pending = [scores(s, k_start, n_blk) for s in range(min(lookahead, n_streams))]
            for s in range(n_streams):
                st = pending.pop(0)
                if s + lookahead < n_streams:
                    pending.append(scores(s + lookahead, k_start, n_blk))
                softmax_pv(s, st, n_blk, blk0)

        def body(c, carry):
            step(pl.multiple_of(c * ATT_TK, ATT_TK), blocks_per_chunk, c * blocks_per_chunk)
            return carry

        n_full = jnp.where(i < n_lat_t, t_lat // ATT_TK, 0)
        lax.fori_loop(0, n_full, body, 0)
        step(t_lat, ctx_len // TM, t_lat // TM)

    acc_ref[...] = jnp.zeros(acc_ref.shape, F32)
    bounded = bounded_ref[0] == 1

    @pl.when(bounded)
    def _():
        sweep(softmax_pv_bounded, ATT_LOOKAHEAD_BOUNDED)

    @pl.when(jnp.logical_not(bounded))
    def _():
        m_ref[...] = jnp.full(m_ref.shape, -jnp.inf, F32)
        sweep(softmax_pv_online, ATT_LOOKAHEAD)

    for s in range(n_streams):
        acc = acc_ref[s]
        o_t = acc[:ATT_HEAD_DIM] * (1.0 / acc[ATT_HEAD_DIM:ATT_HEAD_DIM + 1])
        pair = jnp.concatenate([o_t[:, :tq], o_t[:, tq:]], axis=0)
        o_ref[:, s * LANES:(s + 1) * LANES] = pair.T.astype(BF16)


def _attention(q, kd0, kd1, vt, q_norm_g, k_norm_g, t_lat, n_lat_t):
    rows, qw = q.shape
    ctx_len = rows - t_lat
    score_bound = (ATT_HEAD_DIM ** 0.5 * LOG2_E) * jnp.max(jnp.abs(q_norm_g)) * jnp.max(jnp.abs(k_norm_g))
    bounded = (score_bound <= ATT_SCORE_BOUND).astype(jnp.int32).reshape(1)
    return pl.pallas_call(
        functools.partial(_attn_kernel, n_lat_t=n_lat_t, t_lat=t_lat, ctx_len=ctx_len),
        grid=(rows // TM,),
        in_specs=[pl.BlockSpec(memory_space=pltpu.SMEM),
                  pl.BlockSpec((TM, qw), lambda i: (i, 0)), _full((rows, LANES)), _full((rows, LANES)),
                  _full(vt.shape)],
        out_specs=pl.BlockSpec((TM, qw), lambda i: (i, 0)),
        out_shape=jax.ShapeDtypeStruct((rows, qw), BF16),
        scratch_shapes=[
            pltpu.VMEM((ATT_HEADS * TM, LANES), BF16),
            pltpu.VMEM((ATT_HEADS // 2, 1, 2 * TM), F32),
            pltpu.VMEM((ATT_HEADS // 2, V_ROWS, 2 * TM), F32),
        ],
        compiler_params=_cparams("arbitrary"),
        name="gqa_attention",
    )(bounded, q, kd0, kd1, vt)


def _odd_proj_kernel(h_ref, mod_ref, g_ref, win_ref, gb_ref, pqk_ref, vt_ref, o_ref, gates_ref):
    x = h_ref[...]
    m = mod_ref[0]
    z = _rms_mod(x, g_ref[...], m[3:4], m[4:5]).astype(BF16)
    p = jnp.dot(z, win_ref[...], preferred_element_type=F32)
    qk = 2 * ML_HEADS * ML_QK_DIM
    vw = ML_HEADS * ML_V_DIM
    pqk_ref[...] = p[:, :qk]
    for c in range(TM // CHUNK):
        vt_ref[c] = p[c * CHUNK:(c + 1) * CHUNK, qk:qk + vw].T.astype(BF16)
    o_ref[...] = p[:, qk + vw:qk + 2 * vw]
    gts = p[:, qk + 2 * vw:] + gb_ref[...]
    lane = lax.broadcasted_iota(jnp.int32, gts.shape, 1)
    is_f = (lane < 4 * ML_HEADS) & ((lane // ML_HEADS) % 2 == 1)
    log_sig = jnp.minimum(gts, 0.0) - jnp.log1p(jnp.exp(-jnp.abs(gts)))
    gates_ref[...] = jnp.where(is_f, log_sig, gts)


def _odd_proj(h, mod, g, w_in, gate_b, n_lat_t):
    rows, d = h.shape
    qk = 2 * ML_HEADS * ML_QK_DIM
    vw = ML_HEADS * ML_V_DIM
    n_gate = 4 * ML_HEADS
    w_pad = jnp.pad(w_in, ((0, 0), (0, LANES - n_gate))).astype(BF16)
    gb = jnp.pad(gate_b, (0, LANES - n_gate)).reshape(1, LANES)
    row_tile = lambda w: pl.BlockSpec((TM, w), lambda i: (i, 0))
    return pl.pallas_call(
        _odd_proj_kernel,
        grid=(rows // TM,),
        in_specs=[row_tile(d), pl.BlockSpec((1, N_MOD, d), lambda i: (i // n_lat_t, 0, 0)),
                  _full((1, d)), _full(w_pad.shape), _full((1, LANES))],
        out_specs=[row_tile(qk), pl.BlockSpec((TM // CHUNK, vw, CHUNK), lambda i: (i, 0, 0)),
                   row_tile(vw), row_tile(LANES)],
        out_shape=[jax.ShapeDtypeStruct((rows, qk), F32), jax.ShapeDtypeStruct((rows // CHUNK, vw, CHUNK), BF16),
                   jax.ShapeDtypeStruct((rows, vw), F32), jax.ShapeDtypeStruct((rows, LANES), F32)],
        compiler_params=_cparams("arbitrary"),
        name="odd_proj",
    )(h, mod, g.reshape(1, d), w_pad, gb)


def _conv_kernel(x_ref, prev_ref, next_ref, w_ref, q_ref, k_ref, *, n_lat_t):
    i = pl.program_id(0)
    nt = pl.num_programs(0)
    x = x_ref[...]
    tm = x.shape[0]
    first = (i == 0) | (i == n_lat_t)
    last = (i == n_lat_t - 1) | (i == nt - 1)
    prev_row = jnp.where(first, 0.0, prev_ref[SUBLANES - 1:SUBLANES, :])
    next_row = jnp.where(last, 0.0, next_ref[0:1, :])
    row = lax.broadcasted_iota(jnp.int32, x.shape, 0)
    xm = jnp.where(row == 0, prev_row, pltpu.roll(x, 1, 0))
    xp = jnp.where(row == tm - 1, next_row, pltpu.roll(x, tm - 1, 0))
    y = xm * w_ref[0:1, :] + x * w_ref[1:2, :] + xp * w_ref[2:3, :]
    y = _silu(y)
    qk = ML_HEADS * ML_QK_DIM
    q_ref[...] = (y[:, :qk] * (ML_QK_DIM ** -0.5)).astype(BF16)
    k_ref[...] = y[:, qk:].astype(BF16)


def _conv(pqk, conv_w, n_lat_t):
    rows, w = pqk.shape
    per = TM // SUBLANES
    last_blk = rows // SUBLANES - 1
    return pl.pallas_call(
        functools.partial(_conv_kernel, n_lat_t=n_lat_t),
        grid=(rows // TM,),
        in_specs=[
            pl.BlockSpec((TM, w), lambda i: (i, 0)),
            pl.BlockSpec((SUBLANES, w), lambda i: (jnp.maximum(i * per - 1, 0), 0)),
            pl.BlockSpec((SUBLANES, w), lambda i: (jnp.minimum((i + 1) * per, last_blk), 0)),
            _full(conv_w.shape),
        ],
        out_specs=[pl.BlockSpec((TM, w // 2), lambda i: (i, 0))] * 2,
        out_shape=[jax.ShapeDtypeStruct((rows, w // 2), BF16)] * 2,
        compiler_params=_cparams("arbitrary"),
        name="short_conv",
    )(pqk, pqk, pqk, conv_w)


def _mlstm_prepare(d, g):
    ln = CHUNK
    r = lax.broadcasted_iota(jnp.int32, (ln, ln), 0)
    c = lax.broadcasted_iota(jnp.int32, (ln, ln), 1)
    tri = (c <= r) if d == 0 else (c >= r)
    tri_b = jnp.where(tri, 1.0, 0.0).astype(BF16)
    g_t = g.T
    nt_dims = (((1,), (1,)), ((), ()))
    bc_col = sum(jnp.dot(tri_b, piece, preferred_element_type=F32) for piece in _split3(g))
    bc_row = sum(lax.dot_general(piece, tri_b, nt_dims, preferred_element_type=F32) for piece in _split3(g_t))
    il0 = 2 * ML_HEADS * d
    fl0 = il0 + ML_HEADS
    u = g_t[il0:il0 + ML_HEADS, :] - bc_row[fl0:fl0 + ML_HEADS, :]
    lane = lax.broadcasted_iota(jnp.int32, u.shape, 1)
    sh = 1
    while sh < ln:
        if d == 0:
            shifted = jnp.where(lane >= sh, pltpu.roll(u, sh, 1), -jnp.inf)
        else:
            shifted = jnp.where(lane < ln - sh, pltpu.roll(u, ln - sh, 1), -jnp.inf)
        u = jnp.maximum(u, shifted)
        sh *= 2
    vis = (c >= r) if d == 0 else (c <= r)
    return dict(vis=vis, g=g, g_t=g_t, bc_col=bc_col, bc_row=bc_row, run_max=u,
                lane_lo=c < ML_QK_DIM, last=ln - 1 if d == 0 else 0)


def _mlstm_head(d, h, pre, q_pair, k_pair, vt_ref, o_ref, c_ref, m_ref):
    half = h % 2
    nt_dims = (((1,), (1,)), ((), ()))
    il = 2 * ML_HEADS * d + h
    fl = il + ML_HEADS
    keep_lane = pre["lane_lo"] if half == 0 else ~pre["lane_lo"]
    brow = pre["bc_row"][fl:fl + 1, :]
    irow = pre["g_t"][il:il + 1, :]
    src_col = pre["g"][:, il:il + 1] - pre["bc_col"][:, fl:fl + 1]
    btot = pre["bc_col"][pre["last"]:pre["last"] + 1, fl:fl + 1]
    m_prev = m_ref[h:h + 1, 0:1]

    inter = brow + m_prev
    m_row = jnp.maximum(inter, brow + pre["run_max"][h:h + 1, :])
    wts_t = jnp.exp(jnp.where(pre["vis"], (brow - m_row) + src_col, -jnp.inf))
    a = jnp.exp(inter - m_row)

    qm = jnp.where(keep_lane, q_pair, jnp.zeros_like(q_pair))
    s_t = lax.dot_general(k_pair, qm, nt_dims, preferred_element_type=F32) * wts_t
    vt = vt_ref[h * ML_V_DIM:(h + 1) * ML_V_DIM, :]
    vt_aug = jnp.concatenate([vt, jnp.ones((ML_AUX_ROWS, vt.shape[1]), BF16)], axis=0)
    c_aug = c_ref[h]
    num_den = (jnp.dot(vt_aug, s_t.astype(BF16), preferred_element_type=F32)
               + a * lax.dot_general(c_aug.astype(BF16), qm, nt_dims, preferred_element_type=F32))
    den = num_den[ML_V_DIM:ML_V_DIM + 1]
    h_t = num_den[:ML_V_DIM] / jnp.maximum(jnp.abs(den), jnp.exp(-m_row))
    o_ref[:, h * LANES:(h + 1) * LANES] = h_t.T

    w_row = btot - brow + irow
    m_loc = jnp.max(w_row, axis=1, keepdims=True)
    e = jnp.exp(w_row - m_loc)
    ev_t = (vt_aug.astype(F32) * e).astype(BF16)
    km = jnp.where(keep_lane, k_pair, jnp.zeros_like(k_pair))
    c_loc = jnp.dot(ev_t, km, preferred_element_type=F32)
    m_new = jnp.maximum(btot + m_prev, m_loc)
    decay = jnp.exp(btot + m_prev - m_new)
    gain = jnp.exp(m_loc - m_new)
    c_ref[h] = decay * c_aug + gain * c_loc
    m_ref[h:h + 1, :] = jnp.broadcast_to(m_new, (1, LANES))


def _mlstm_kernel(qf_ref, kf_ref, vf_ref, gf_ref, qb_ref, kb_ref, vb_ref, gb_ref, of_ref, ob_ref,
                  ct_ref, m_ref):
    @pl.when(pl.program_id(0) == 0)
    def _():
        ct_ref[...] = jnp.zeros(ct_ref.shape, F32)
        m_ref[...] = jnp.zeros(m_ref.shape, F32)

    dirs = ((qf_ref, kf_ref, vf_ref, gf_ref, of_ref), (qb_ref, kb_ref, vb_ref, gb_ref, ob_ref))

    def chunk_of(d, step):
        return step if d == 0 else ML_CHUNKS_PER_STEP - 1 - step

    def chunk_rows(d, step):
        return pl.ds(chunk_of(d, step) * CHUNK, CHUNK)

    pre = [[_mlstm_prepare(d, dirs[d][3][chunk_rows(d, step), :]) for d in range(2)]
           for step in range(ML_CHUNKS_PER_STEP)]
    for step in range(ML_CHUNKS_PER_STEP):
        for pair in range(ML_HEADS // 2):
            loaded = []
            for d in range(2):
                rows = chunk_rows(d, step)
                loaded.append((dirs[d][0][rows, pair * LANES:(pair + 1) * LANES],
                               dirs[d][1][rows, pair * LANES:(pair + 1) * LANES]))
            for half in range(2):
                for d in range(2):
                    q_pair, k_pair = loaded[d]
                    _mlstm_head(d, 2 * pair + half, pre[step][d], q_pair, k_pair,
                                dirs[d][2].at[chunk_of(d, step)], dirs[d][4].at[chunk_rows(d, step)],
                                ct_ref.at[d], m_ref.at[d])


def _mlstm(q, k, vt, gates, n_lat_ch):
    rows = q.shape[0]
    step_rows = ML_CHUNKS_PER_STEP * CHUNK
    assert rows % step_rows == 0 and (n_lat_ch * CHUNK) % step_rows == 0
    nblk = rows // step_rows
    n_lat_blk = n_lat_ch * CHUNK // step_rows
    fwd = lambda i: (i + n_lat_blk) % nblk
    bwd = lambda i: nblk - 1 - i
    vw = vt.shape[1]

    def blocks(order):
        blk = lambda w: pl.BlockSpec((step_rows, w), lambda i: (order(i), 0))
        vt_blk = pl.BlockSpec((ML_CHUNKS_PER_STEP, vw, CHUNK), lambda i: (order(i), 0, 0))
        return [blk(q.shape[1]), blk(k.shape[1]), vt_blk, blk(LANES)]

    out_blk = lambda order: pl.BlockSpec((step_rows, vw), lambda i: (order(i), 0))
    return pl.pallas_call(
        _mlstm_kernel,
        grid=(nblk,),
        in_specs=blocks(fwd) + blocks(bwd),
        out_specs=[out_blk(fwd), out_blk(bwd)],
        out_shape=[jax.ShapeDtypeStruct((rows, vw), F32)] * 2,
        scratch_shapes=[
            pltpu.VMEM((2, ML_HEADS, ML_V_DIM + ML_AUX_ROWS, LANES), F32),
            pltpu.VMEM((2, ML_HEADS, LANES), F32),
        ],
        compiler_params=_cparams("arbitrary"),
        name="mlstm",
    )(q, k, vt, gates, q, k, vt, gates)


def kernel(x, c, ctx, c_ctx, mod_w, mod_b, norm_g, ffn_w13, ffn_w2, ab_w_in, ab_gate_norm_g, ab_spatial_w, ab_spatial_b, ab_q_norm_g, ab_k_norm_g, ab_w_out, ml_w_in, ml_conv_w, ml_gate_b, ml_out_norm_g, ml_w_out, final_norm_g):
    b, t_lat, d = x.shape
    ctx_len = ctx.shape[1]
    depth = mod_w.shape[0]
    assert b == 1 and t_lat % TM == 0 and ctx_len % TM == 0 and t_lat % ATT_TK == 0
    n_lat_t = t_lat // TM
    n_all_t = (t_lat + ctx_len) // TM

    mods = _mods(c, c_ctx, mod_w, mod_b)
    w13_all = ffn_w13.astype(BF16)
    w2_all = ffn_w2.astype(BF16)
    h = x[0]
    for l in range(depth):
        keep_ctx = l < depth - 1
        mod = mods[l]
        h = _ffn(h, mod, norm_g[l, 0], w13_all, w2_all, l, 0, n_all_t, n_lat_t, 0,
                 ctx_rows=ctx[0] if l == 0 else None)
        n_out_t = n_all_t if keep_ctx else n_lat_t
        if l % 2 == 0:
            e = l // 2
            gated, q, kd0, kd1, vt = _even_proj(
                h, mod, norm_g[l, 1], ab_w_in[e], ab_gate_norm_g[e], ab_spatial_w[e], ab_spatial_b[e],
                ab_q_norm_g[e], ab_k_norm_g[e], t_lat, n_lat_t)
            attn = _attention(q, kd0, kd1, vt, ab_q_norm_g[e], ab_k_norm_g[e], t_lat, n_lat_t)
            mixer, mixer_args = "even", (gated, attn, ab_w_out[e].astype(BF16))
        else:
            o = l // 2
            pqk, vt, og, gates = _odd_proj(h, mod, norm_g[l, 1], ml_w_in[o], ml_gate_b[o], n_lat_t)
            q, k = _conv(pqk, ml_conv_w[o], n_lat_t)
            hf, hb = _mlstm(q, k, vt, gates, t_lat // CHUNK)
            mixer, mixer_args = "odd", (hf, hb, og, ml_out_norm_g[o].reshape(1, -1), ml_w_out[o].astype(BF16))
        final_g = final_norm_g if l == depth - 1 else None
        h = _ffn(h, mod, norm_g[l, 2], w13_all, w2_all, l, 1, n_out_t, n_lat_t, 6, final_g,
                 mixer=mixer, mixer_args=mixer_args)
    return h[:t_lat][None]
```

```python
import functools

import jax
import jax.numpy as jnp
from jax import lax
from jax.experimental import pallas as pl
from jax.experimental.pallas import tpu as pltpu

F32 = jnp.float32
BF16 = jnp.bfloat16

LANES = 128
SUBLANES = 8
VMEM_LIMIT = 56 * 1024 * 1024

GRID_W = 64
CHUNK = 128
EPS = 1e-6
ROPE_THETA = 10000.0
LOG2_E = 1.4426950408889634
N_MOD = 9
A_GROUPS = 8
A_GROUP_DIM = 64
ATT_HEADS = 8
ATT_KV_HEADS = 2
ATT_HEAD_DIM = 64
ML_HEADS = 8
ML_QK_DIM = 64
ML_V_DIM = 128
CONV_WIDTH = 3

TM = 256
CONV_MAX_TILES = 5
ATT_TK = 4096
ATT_LOOKAHEAD = 3
ATT_LOOKAHEAD_BOUNDED = 2
ML_AUX_ROWS = 16
ML_CHUNKS_PER_STEP = 2
ATT_SCORE_BOUND = 60.0
V_ROWS = ATT_HEAD_DIM + 16


def _cparams(*sem):
    return pltpu.CompilerParams(dimension_semantics=sem, vmem_limit_bytes=VMEM_LIMIT)


def _full(shape):
    n = len(shape)
    return pl.BlockSpec(shape, lambda *_: (0,) * n)


def _rms_mod(x, g, shift, scale):
    ms = jnp.mean(x * x, axis=-1, keepdims=True)
    y = x * lax.rsqrt(ms + EPS) * g
    return y * (1 + scale) + shift


def _silu(x):
    return x * jax.nn.sigmoid(x)


def _split3(x):
    x1 = x.astype(BF16)
    r = x - x1.astype(F32)
    x2 = r.astype(BF16)
    x3 = (r - x2.astype(F32)).astype(BF16)
    return x1, x2, x3


def _group_mean_sq(x, bmat):
    sq = x * x
    hi = sq.astype(BF16)
    lo = (sq - hi.astype(F32)).astype(BF16)
    return (jnp.dot(hi, bmat, preferred_element_type=F32)
            + jnp.dot(lo, bmat, preferred_element_type=F32))


def _mod_kernel(c_ref, w_ref, b_ref, o_ref):
    w = w_ref[0]
    o_ref[0] = jnp.zeros(o_ref.shape[1:], F32)
    for r in range(c_ref.shape[0]):
        s = _silu(c_ref[r])
        cols = [jnp.sum(w[:, j * LANES:(j + 1) * LANES] * s, axis=0, keepdims=True)
                for j in range(w.shape[1] // LANES)]
        o_ref[0, r:r + 1, :] = jnp.concatenate(cols, axis=1) + b_ref[0]


def _mods(c, c_ctx, mod_w, mod_b):
    depth, d, nd = mod_w.shape
    crep = jnp.broadcast_to(jnp.stack([c[0], c_ctx])[:, :, None], (2, d, LANES))
    out = pl.pallas_call(
        _mod_kernel,
        grid=(depth, nd // d),
        in_specs=[
            pl.BlockSpec((2, d, LANES), lambda l, j: (0, 0, 0)),
            pl.BlockSpec((1, d, d), lambda l, j: (l, 0, j)),
            pl.BlockSpec((1, 1, d), lambda l, j: (l, 0, j)),
        ],
        out_specs=pl.BlockSpec((1, SUBLANES, d), lambda l, j: (l, 0, j)),
        out_shape=jax.ShapeDtypeStruct((depth, SUBLANES, nd), F32),
        compiler_params=_cparams("arbitrary", "arbitrary"),
        name="modulation",
    )(crep, mod_w, mod_b.reshape(depth, 1, nd))
    return out[:, :2].reshape(depth, 2, N_MOD, d)


def _even_mixer_out(a_ref, b_ref, w_ref):
    aw = a_ref.shape[1]
    return (jnp.dot(a_ref[...], w_ref[:aw], preferred_element_type=F32)
            + jnp.dot(b_ref[...], w_ref[aw:], preferred_element_type=F32))


def _odd_mixer_out(hf_ref, hb_ref, og_ref, ng_ref, w_ref):
    hs = hf_ref[...] + hb_ref[...]
    parts = []
    for hh in range(ML_HEADS):
        blk = hs[:, hh * ML_V_DIM:(hh + 1) * ML_V_DIM]
        ms = jnp.mean(blk * blk, axis=-1, keepdims=True)
        parts.append(blk * lax.rsqrt(ms + EPS))
    hn = jnp.concatenate(parts, axis=1) * ng_ref[...]
    gated = (hn * jax.nn.sigmoid(og_ref[...])).astype(BF16)
    return jnp.dot(gated, w_ref[...], preferred_element_type=F32)


_MIXER_OUT = {"even": (_even_mixer_out, 3), "odd": (_odd_mixer_out, 5)}


def _ffn_kernel(h_ref, mod_ref, g_ref, w13_ref, w2_ref, *rest, mod_base, final, n_lat_t, split_input, mixer):
    o_ref = rest[-1]
    x = h_ref[...]
    m = mod_ref[0]
    if split_input:
        x = jnp.where(pl.program_id(0) < n_lat_t, x, rest[0][...])
        rest = rest[1:]
    if mixer is not None:
        fn, n_ops = _MIXER_OUT[mixer]
        x = x + m[5:6] * fn(*rest[:n_ops])
        rest = rest[n_ops:]
    z = _rms_mod(x, g_ref[...], m[mod_base:mod_base + 1], m[mod_base + 1:mod_base + 2]).astype(BF16)
    f = w2_ref.shape[0]
    gu = jnp.dot(z, w13_ref[...], preferred_element_type=F32)
    a = (_silu(gu[:, :f]) * gu[:, f:]).astype(BF16)
    y = jnp.dot(a, w2_ref[...], preferred_element_type=F32)
    out = x + (0.5 * m[mod_base + 2:mod_base + 3]) * y
    if final:
        ms = jnp.mean(out * out, axis=-1, keepdims=True)
        out = out * lax.rsqrt(ms + EPS) * rest[0][...]
    o_ref[...] = out


def _ffn(h, mod, g, w13_all, w2_all, layer, which, n_tiles, n_lat_t, mod_base, final_g=None, ctx_rows=None,
         mixer=None, mixer_args=()):
    d = h.shape[1]
    f = w2_all.shape[2]
    final = final_g is not None
    split_input = ctx_rows is not None
    last_lat = n_lat_t - 1
    row_tile = lambda w: pl.BlockSpec((TM, w), lambda i: (i, 0))
    in_specs = [
        pl.BlockSpec((TM, d), (lambda i: (jnp.minimum(i, last_lat), 0)) if split_input else (lambda i: (i, 0))),
        pl.BlockSpec((1, N_MOD, d), lambda i: (i // n_lat_t, 0, 0)),
        _full((1, d)),
        pl.BlockSpec((None, None, d, 2 * f), lambda i: (layer, which, 0, 0)),
        pl.BlockSpec((None, None, f, d), lambda i: (layer, which, 0, 0)),
    ]
    args = [h, mod, g.reshape(1, d), w13_all, w2_all]
    if split_input:
        in_specs.append(pl.BlockSpec((TM, d), lambda i: (jnp.maximum(i - n_lat_t, 0), 0)))
        args.append(ctx_rows)
    for arr in mixer_args:
        in_specs.append(row_tile(arr.shape[1]) if arr.shape[0] == h.shape[0] else _full(arr.shape))
        args.append(arr)
    if final:
        in_specs.append(_full((1, d)))
        args.append(final_g.reshape(1, d))
    return pl.pallas_call(
        functools.partial(_ffn_kernel, mod_base=mod_base, final=final, n_lat_t=n_lat_t, split_input=split_input,
                          mixer=mixer),
        grid=(n_tiles,),
        in_specs=in_specs,
        out_specs=row_tile(d),
        out_shape=jax.ShapeDtypeStruct((n_tiles * TM, d), F32),
        compiler_params=_cparams("arbitrary"),
        name="ffn",
    )(*args)


def _even_proj_kernel(h_ref, mod_ref, g_ref, win_ref, gng_ref, qg_ref, kg_ref, spw_ref, spb_ref,
                      cos_r_ref, sin_r_ref, cos_c_ref, sin_c_ref, b512_ref, b128_ref,
                      gated_ref, q_ref, kd0_ref, kd1_ref, vt_ref, *, n_lat_t):
    x = h_ref[...]
    m = mod_ref[0]
    z = _rms_mod(x, g_ref[...], m[3:4], m[4:5]).astype(BF16)
    p = jnp.dot(z, win_ref[...], preferred_element_type=F32)
    aw = A_GROUPS * A_GROUP_DIM
    qw = ATT_HEADS * ATT_HEAD_DIM
    u = jax.nn.gelu(p[:, :aw])
    v = jax.nn.gelu(p[:, aw:2 * aw])
    b512 = b512_ref[...]
    vn = (v * lax.rsqrt(_group_mean_sq(v, b512) + EPS) * gng_ref[...]).astype(BF16)

    lane = lax.broadcasted_iota(jnp.int32, (CHUNK, LANES), 1)
    lo = lane < A_GROUP_DIM
    for c in range(TM // CHUNK):
        r0 = c * CHUNK
        for j in range(aw // LANES):
            vt = vn[r0:r0 + CHUNK, j * LANES:(j + 1) * LANES]
            s0 = jnp.dot(spw_ref[2 * j], vt, preferred_element_type=F32)
            s1 = jnp.dot(spw_ref[2 * j + 1], vt, preferred_element_type=F32)
            s = jnp.where(lo, s0, s1) + spb_ref[:, j * LANES:(j + 1) * LANES]
            gated_ref[r0:r0 + CHUNK, j * LANES:(j + 1) * LANES] = (
                u[r0:r0 + CHUNK, j * LANES:(j + 1) * LANES] * s).astype(BF16)

    o = 2 * aw
    pq = p[:, o:o + qw]
    pk = p[:, o + qw:o + qw + LANES]
    va = p[:, o + qw + LANES:o + qw + 2 * LANES]
    qn = pq * lax.rsqrt(_group_mean_sq(pq, b512) + EPS) * qg_ref[...]
    kn = pk * lax.rsqrt(_group_mean_sq(pk, b128_ref[...]) + EPS) * kg_ref[...]

    is_lat = pl.program_id(0) < n_lat_t
    lane_g = lax.broadcasted_iota(jnp.int32, (GRID_W, LANES), 1)
    row_lanes = (lane_g % ATT_HEAD_DIM) < ATT_HEAD_DIM // 2

    def tile_table(r_ref, c_ref, ctx_value):
        col_part = jnp.where(is_lat, c_ref[...], ctx_value)
        return jnp.concatenate([jnp.where(row_lanes, r_ref[gr:gr + 1, :], col_part)
                                for gr in range(TM // GRID_W)], axis=0)

    cos = tile_table(cos_r_ref, cos_c_ref, 1.0)
    sin = tile_table(sin_r_ref, sin_c_ref, 0.0)
    lane_t = lax.broadcasted_iota(jnp.int32, (TM, LANES), 1)
    first_half = (lane_t % 32) < 16
    lo_t = lane_t < ATT_HEAD_DIM

    def rope(t):
        partner = jnp.where(first_half, pltpu.roll(t, LANES - 16, 1), pltpu.roll(t, 16, 1))
        return t * cos + partner * sin

    scale = ATT_HEAD_DIM ** -0.5 * LOG2_E
    for j in range(qw // LANES):
        q_ref[:, j * LANES:(j + 1) * LANES] = (rope(qn[:, j * LANES:(j + 1) * LANES]) * scale).astype(BF16)
    kr = rope(kn)
    kswap = pltpu.roll(kr, ATT_HEAD_DIM, 1)
    kd0_ref[...] = jnp.where(lo_t, kr, kswap).astype(BF16)
    kd1_ref[...] = jnp.where(lo_t, kswap, kr).astype(BF16)
    va_t = va.T
    sub = lax.broadcasted_iota(jnp.int32, (V_ROWS - ATT_HEAD_DIM, TM), 0)
    ones_rows = jnp.where(sub == 0, 1.0, 0.0).astype(BF16)
    for g in range(ATT_KV_HEADS):
        vt_ref[g, 0, :ATT_HEAD_DIM, :] = va_t[g * ATT_HEAD_DIM:(g + 1) * ATT_HEAD_DIM].astype(BF16)
        vt_ref[g, 0, ATT_HEAD_DIM:, :] = ones_rows


def _group_matrix(width, gs):
    r = jnp.arange(width) // gs
    return jnp.where(r[:, None] == r[None, :], 1.0 / gs, 0.0).astype(BF16)


def _rope_tables(t_lat, rows):
    axis_dim = ATT_HEAD_DIM // 2
    n_freq = axis_dim // 2
    inv_freq = ROPE_THETA ** (-jnp.arange(0, axis_dim, 2, dtype=F32) / axis_dim)
    per_tile = TM // GRID_W
    n_lat_t = t_lat // TM
    n_ctx_t = (rows - t_lat) // TM
    lane = jnp.arange(LANES)
    sign = jnp.where((lane % axis_dim) < n_freq, -1.0, 1.0)
    ang_r = jnp.arange(t_lat // GRID_W, dtype=F32)[:, None] * inv_freq[lane % n_freq][None, :]
    ang_r = jnp.pad(ang_r.reshape(n_lat_t, per_tile, LANES), ((0, n_ctx_t), (0, SUBLANES - per_tile), (0, 0)))
    ang_r = ang_r.reshape((n_lat_t + n_ctx_t) * SUBLANES, LANES)
    ang_c = jnp.arange(GRID_W, dtype=F32)[:, None] * inv_freq[lane % n_freq][None, :]
    return jnp.cos(ang_r), jnp.sin(ang_r) * sign, jnp.cos(ang_c), jnp.sin(ang_c) * sign


def _even_proj(h, mod, g, w_in, gate_norm_g, sp_w, sp_b, q_norm_g, k_norm_g, t_lat, n_lat_t):
    rows, d = h.shape
    aw = A_GROUPS * A_GROUP_DIM
    qw = ATT_HEADS * ATT_HEAD_DIM
    cos_r, sin_r, cos_c, sin_c = _rope_tables(t_lat, rows)
    spb = jnp.repeat(sp_b.T, A_GROUP_DIM, axis=1)
    row_tile = lambda w: pl.BlockSpec((TM, w), lambda i: (i, 0))
    out_shape = [jax.ShapeDtypeStruct((rows, aw), BF16), jax.ShapeDtypeStruct((rows, qw), BF16)]
    out_shape += [jax.ShapeDtypeStruct((rows, LANES), BF16)] * 2
    out_shape += [jax.ShapeDtypeStruct((ATT_KV_HEADS, rows // TM, V_ROWS, TM), BF16)]
    vt_spec = pl.BlockSpec((ATT_KV_HEADS, 1, V_ROWS, TM), lambda i: (0, i, 0, 0))
    return pl.pallas_call(
        functools.partial(_even_proj_kernel, n_lat_t=n_lat_t),
        grid=(rows // TM,),
        in_specs=[
            row_tile(d),
            pl.BlockSpec((1, N_MOD, d), lambda i: (i // n_lat_t, 0, 0)),
            _full((1, d)),
            _full(w_in.shape),
            _full((1, aw)),
            _full((1, qw)),
            _full((1, LANES)),
            _full(sp_w.shape),
            _full((CHUNK, aw)),
            pl.BlockSpec((SUBLANES, LANES), lambda i: (i, 0)),
            pl.BlockSpec((SUBLANES, LANES), lambda i: (i, 0)),
            _full((GRID_W, LANES)),
            _full((GRID_W, LANES)),
            _full((aw, aw)),
            _full((LANES, LANES)),
        ],
        out_specs=[row_tile(aw), row_tile(qw), row_tile(LANES), row_tile(LANES), vt_spec],
        out_shape=out_shape,
        compiler_params=_cparams("arbitrary"),
        name="even_proj",
    )(h, mod, g.reshape(1, d), w_in.astype(BF16), gate_norm_g.reshape(1, aw),
      jnp.tile(q_norm_g, ATT_HEADS).reshape(1, qw), jnp.tile(k_norm_g, ATT_KV_HEADS).reshape(1, LANES),
      sp_w.astype(BF16), spb, cos_r, sin_r, cos_c, sin_c,
      _group_matrix(aw, A_GROUP_DIM), _group_matrix(LANES, ATT_HEAD_DIM))


def _attn_kernel(bounded_ref, q_ref, kd0_ref, kd1_ref, vt_ref, o_ref, qs_ref, m_ref, acc_ref,
                 *, n_lat_t, t_lat, ctx_len):
    i = pl.program_id(0)
    tq = q_ref.shape[0]
    lane = lax.broadcasted_iota(jnp.int32, (tq, LANES), 1)
    lo = lane < ATT_HEAD_DIM
    for h in range(ATT_HEADS):
        t = q_ref[:, (h // 2) * LANES:(h // 2 + 1) * LANES]
        qs_ref[h * tq:(h + 1) * tq] = jnp.where(lo if h % 2 == 0 else ~lo, t, jnp.zeros_like(t))

    kd_refs = (kd0_ref, kd1_ref)
    n_streams = ATT_HEADS // 2
    blocks_per_chunk = ATT_TK // TM
    nt_dims = (((1,), (1,)), ((), ()))

    def group_of(s):
        return s // (n_streams // ATT_KV_HEADS)

    def scores(s, k_start, n_blk):
        kc = kd_refs[group_of(s)][pl.ds(k_start, n_blk * TM), :]
        qs = qs_ref[2 * s * tq:(2 * s + 2) * tq]
        return lax.dot_general(kc, qs, nt_dims, preferred_element_type=F32)

    def p_times_v(s, pt, n_blk, blk0):
        pv = None
        for j in range(n_blk):
            part = jnp.dot(vt_ref[group_of(s), blk0 + j], pt[j * TM:(j + 1) * TM], preferred_element_type=F32)
            pv = part if pv is None else pv + part
        return pv

    def softmax_pv_online(s, st, n_blk, blk0):
        m_old = m_ref[s]
        m_new = jnp.maximum(m_old, jnp.max(st, axis=0, keepdims=True))
        alpha = jnp.exp2(m_old - m_new)
        pt = jnp.exp2(st - m_new).astype(BF16)
        m_ref[s] = m_new
        acc_ref[s] = alpha * acc_ref[s] + p_times_v(s, pt, n_blk, blk0)

    def softmax_pv_bounded(s, st, n_blk, blk0):
        pt = jnp.exp2(st).astype(BF16)
        acc_ref[s] += p_times_v(s, pt, n_blk, blk0)

    def sweep(softmax_pv, lookahead):
        def step(k_start, n_blk, blk0):
            pending = [scores(s, k_start, n_blk) for s in range(min(lookahead, n_streams))]
            for s in range(n_streams):
                st = pending.pop(0)
                if s + lookahead < n_streams:
                    pending.append(scores(s + lookahead, k_start, n_blk))
                softmax_pv(s, st, n_blk, blk0)

        def body(c, carry):
            step(pl.multiple_of(c * ATT_TK, ATT_TK), blocks_per_chunk, c * blocks_per_chunk)
            return carry

        n_full = jnp.where(i < n_lat_t, t_lat // ATT_TK, 0)
        lax.fori_loop(0, n_full, body, 0)
        step(t_lat, ctx_len // TM, t_lat // TM)

    acc_ref[...] = jnp.zeros(acc_ref.shape, F32)
    bounded = bounded_ref[0] == 1

    @pl.when(bounded)
    def _():
        sweep(softmax_pv_bounded, ATT_LOOKAHEAD_BOUNDED)

    @pl.when(jnp.logical_not(bounded))
    def _():
        m_ref[...] = jnp.full(m_ref.shape, -jnp.inf, F32)
        sweep(softmax_pv_online, ATT_LOOKAHEAD)

    for s in range(n_streams):
        acc = acc_ref[s]
        o_t = acc[:ATT_HEAD_DIM] * (1.0 / acc[ATT_HEAD_DIM:ATT_HEAD_DIM + 1])
        pair = jnp.concatenate([o_t[:, :tq], o_t[:, tq:]], axis=0)
        o_ref[:, s * LANES:(s + 1) * LANES] = pair.T.astype(BF16)


def _attention(q, kd0, kd1, vt, q_norm_g, k_norm_g, t_lat, n_lat_t):
    rows, qw = q.shape
    ctx_len = rows - t_lat
    score_bound = (ATT_HEAD_DIM ** 0.5 * LOG2_E) * jnp.max(jnp.abs(q_norm_g)) * jnp.max(jnp.abs(k_norm_g))
    bounded = (score_bound <= ATT_SCORE_BOUND).astype(jnp.int32).reshape(1)
    return pl.pallas_call(
        functools.partial(_attn_kernel, n_lat_t=n_lat_t, t_lat=t_lat, ctx_len=ctx_len),
        grid=(rows // TM,),
        in_specs=[pl.BlockSpec(memory_space=pltpu.SMEM),
                  pl.BlockSpec((TM, qw), lambda i: (i, 0)), _full((rows, LANES)), _full((rows, LANES)),
                  _full(vt.shape)],
        out_specs=pl.BlockSpec((TM, qw), lambda i: (i, 0)),
        out_shape=jax.ShapeDtypeStruct((rows, qw), BF16),
        scratch_shapes=[
            pltpu.VMEM((ATT_HEADS * TM, LANES), BF16),
            pltpu.VMEM((ATT_HEADS // 2, 1, 2 * TM), F32),
            pltpu.VMEM((ATT_HEADS // 2, V_ROWS, 2 * TM), F32),
        ],
        compiler_params=_cparams("arbitrary"),
        name="gqa_attention",
    )(bounded, q, kd0, kd1, vt)


def _odd_proj_kernel(h_ref, mod_ref, g_ref, win_ref, gb_ref, pqk_ref, vt_ref, o_ref, gates_ref):
    x = h_ref[...]
    m = mod_ref[0]
    z = _rms_mod(x, g_ref[...], m[3:4], m[4:5]).astype(BF16)
    p = jnp.dot(z, win_ref[...], preferred_element_type=F32)
    qk = 2 * ML_HEADS * ML_QK_DIM
    vw = ML_HEADS * ML_V_DIM
    pqk_ref[...] = p[:, :qk]
    for c in range(TM // CHUNK):
        vt_ref[c] = p[c * CHUNK:(c + 1) * CHUNK, qk:qk + vw].T.astype(BF16)
    o_ref[...] = p[:, qk + vw:qk + 2 * vw]
    gts = p[:, qk + 2 * vw:] + gb_ref[...]
    lane = lax.broadcasted_iota(jnp.int32, gts.shape, 1)
    is_f = (lane < 4 * ML_HEADS) & ((lane // ML_HEADS) % 2 == 1)
    log_sig = jnp.minimum(gts, 0.0) - jnp.log1p(jnp.exp(-jnp.abs(gts)))
    gates_ref[...] = jnp.where(is_f, log_sig, gts)


def _odd_proj(h, mod, g, w_in, gate_b, n_lat_t):
    rows, d = h.shape
    qk = 2 * ML_HEADS * ML_QK_DIM
    vw = ML_HEADS * ML_V_DIM
    n_gate = 4 * ML_HEADS
    w_pad = jnp.pad(w_in, ((0, 0), (0, LANES - n_gate))).astype(BF16)
    gb = jnp.pad(gate_b, (0, LANES - n_gate)).reshape(1, LANES)
    row_tile = lambda w: pl.BlockSpec((TM, w), lambda i: (i, 0))
    return pl.pallas_call(
        _odd_proj_kernel,
        grid=(rows // TM,),
        in_specs=[row_tile(d), pl.BlockSpec((1, N_MOD, d), lambda i: (i // n_lat_t, 0, 0)),
                  _full((1, d)), _full(w_pad.shape), _full((1, LANES))],
        out_specs=[row_tile(qk), pl.BlockSpec((TM // CHUNK, vw, CHUNK), lambda i: (i, 0, 0)),
                   row_tile(vw), row_tile(LANES)],
        out_shape=[jax.ShapeDtypeStruct((rows, qk), F32), jax.ShapeDtypeStruct((rows // CHUNK, vw, CHUNK), BF16),
                   jax.ShapeDtypeStruct((rows, vw), F32), jax.ShapeDtypeStruct((rows, LANES), F32)],
        compiler_params=_cparams("arbitrary"),
        name="odd_proj",
    )(h, mod, g.reshape(1, d), w_pad, gb)


def _conv_kernel(x_ref, prev_ref, next_ref, w_ref, q_ref, k_ref, *, t_lat, rows):
    x = x_ref[...]
    tm = x.shape[0]
    row = lax.broadcasted_iota(jnp.int32, x.shape, 0)
    token = row + pl.program_id(0) * tm
    xm = jnp.where(row == 0, prev_ref[SUBLANES - 1:SUBLANES, :], pltpu.roll(x, 1, 0))
    xp = jnp.where(row == tm - 1, next_ref[0:1, :], pltpu.roll(x, tm - 1, 0))
    xm = jnp.where((token == 0) | (token == t_lat), 0.0, xm)
    xp = jnp.where((token == t_lat - 1) | (token == rows - 1), 0.0, xp)
    y = xm * w_ref[0:1, :] + x * w_ref[1:2, :] + xp * w_ref[2:3, :]
    y = _silu(y)
    qk = ML_HEADS * ML_QK_DIM
    q_ref[...] = (y[:, :qk] * (ML_QK_DIM ** -0.5)).astype(BF16)
    k_ref[...] = y[:, qk:].astype(BF16)


def _conv(pqk, conv_w, t_lat):
    rows, w = pqk.shape
    n_tiles = rows // TM
    tm = TM * max(t for t in range(1, CONV_MAX_TILES + 1) if n_tiles % t == 0)
    per = tm // SUBLANES
    last_blk = rows // SUBLANES - 1
    return pl.pallas_call(
        functools.partial(_conv_kernel, t_lat=t_lat, rows=rows),
        grid=(rows // tm,),
        in_specs=[
            pl.BlockSpec((tm, w), lambda i: (i, 0)),
            pl.BlockSpec((SUBLANES, w), lambda i: (jnp.maximum(i * per - 1, 0), 0)),
            pl.BlockSpec((SUBLANES, w), lambda i: (jnp.minimum((i + 1) * per, last_blk), 0)),
            _full(conv_w.shape),
        ],
        out_specs=[pl.BlockSpec((tm, w // 2), lambda i: (i, 0))] * 2,
        out_shape=[jax.ShapeDtypeStruct((rows, w // 2), BF16)] * 2,
        compiler_params=_cparams("arbitrary"),
        name="short_conv",
    )(pqk, pqk, pqk, conv_w)


def _mlstm_prepare(d, g):
    ln = CHUNK
    r = lax.broadcasted_iota(jnp.int32, (ln, ln), 0)
    c = lax.broadcasted_iota(jnp.int32, (ln, ln), 1)
    tri = (c <= r) if d == 0 else (c >= r)
    tri_b = jnp.where(tri, 1.0, 0.0).astype(BF16)
    g_t = g.T
    nt_dims = (((1,), (1,)), ((), ()))
    bc_col = sum(jnp.dot(tri_b, piece, preferred_element_type=F32) for piece in _split3(g))
    bc_row = sum(lax.dot_general(piece, tri_b, nt_dims, preferred_element_type=F32) for piece in _split3(g_t))
    il0 = 2 * ML_HEADS * d
    fl0 = il0 + ML_HEADS
    u = g_t[il0:il0 + ML_HEADS, :] - bc_row[fl0:fl0 + ML_HEADS, :]
    lane = lax.broadcasted_iota(jnp.int32, u.shape, 1)
    sh = 1
    while sh < ln:
        if d == 0:
            shifted = jnp.where(lane >= sh, pltpu.roll(u, sh, 1), -jnp.inf)
        else:
            shifted = jnp.where(lane < ln - sh, pltpu.roll(u, ln - sh, 1), -jnp.inf)
        u = jnp.maximum(u, shifted)
        sh *= 2
    vis = (c >= r) if d == 0 else (c <= r)
    return dict(vis=vis, g=g, g_t=g_t, bc_col=bc_col, bc_row=bc_row, run_max=u,
                lane_lo=c < ML_QK_DIM, last=ln - 1 if d == 0 else 0)


def _mlstm_head(d, h, pre, q_pair, k_pair, vt_ref, o_ref, c_ref, m_ref):
    half = h % 2
    nt_dims = (((1,), (1,)), ((), ()))
    il = 2 * ML_HEADS * d + h
    fl = il + ML_HEADS
    keep_lane = pre["lane_lo"] if half == 0 else ~pre["lane_lo"]
    brow = pre["bc_row"][fl:fl + 1, :]
    irow = pre["g_t"][il:il + 1, :]
    src_col = pre["g"][:, il:il + 1] - pre["bc_col"][:, fl:fl + 1]
    btot = pre["bc_col"][pre["last"]:pre["last"] + 1, fl:fl + 1]
    m_prev = m_ref[h:h + 1, 0:1]

    inter = brow + m_prev
    m_row = jnp.maximum(inter, brow + pre["run_max"][h:h + 1, :])
    wts_t = jnp.exp(jnp.where(pre["vis"], (brow - m_row) + src_col, -jnp.inf))
    a = jnp.exp(inter - m_row)

    qm = jnp.where(keep_lane, q_pair, jnp.zeros_like(q_pair))
    s_t = lax.dot_general(k_pair, qm, nt_dims, preferred_element_type=F32) * wts_t
    vt = vt_ref[h * ML_V_DIM:(h + 1) * ML_V_DIM, :]
    vt_aug = jnp.concatenate([vt, jnp.ones((ML_AUX_ROWS, vt.shape[1]), BF16)], axis=0)
    c_aug = c_ref[h]
    num_den = (jnp.dot(vt_aug, s_t.astype(BF16), preferred_element_type=F32)
               + a * lax.dot_general(c_aug.astype(BF16), qm, nt_dims, preferred_element_type=F32))
    den = num_den[ML_V_DIM:ML_V_DIM + 1]
    h_t = num_den[:ML_V_DIM] / jnp.maximum(jnp.abs(den), jnp.exp(-m_row))
    o_ref[:, h * LANES:(h + 1) * LANES] = h_t.T

    w_row = btot - brow + irow
    m_loc = jnp.max(w_row, axis=1, keepdims=True)
    e = jnp.exp(w_row - m_loc)
    ev_t = (vt_aug.astype(F32) * e).astype(BF16)
    km = jnp.where(keep_lane, k_pair, jnp.zeros_like(k_pair))
    c_loc = jnp.dot(ev_t, km, preferred_element_type=F32)
    m_new = jnp.maximum(btot + m_prev, m_loc)
    decay = jnp.exp(btot + m_prev - m_new)
    gain = jnp.exp(m_loc - m_new)
    c_ref[h] = decay * c_aug + gain * c_loc
    m_ref[h:h + 1, :] = jnp.broadcast_to(m_new, (1, LANES))


def _mlstm_kernel(qf_ref, kf_ref, vf_ref, gf_ref, qb_ref, kb_ref, vb_ref, gb_ref, of_ref, ob_ref,
                  ct_ref, m_ref):
    @pl.when(pl.program_id(0) == 0)
    def _():
        ct_ref[...] = jnp.zeros(ct_ref.shape, F32)
        m_ref[...] = jnp.zeros(m_ref.shape, F32)

    dirs = ((qf_ref, kf_ref, vf_ref, gf_ref, of_ref), (qb_ref, kb_ref, vb_ref, gb_ref, ob_ref))

    def chunk_of(d, step):
        return step if d == 0 else ML_CHUNKS_PER_STEP - 1 - step

    def chunk_rows(d, step):
        return pl.ds(chunk_of(d, step) * CHUNK, CHUNK)

    pre = [[_mlstm_prepare(d, dirs[d][3][chunk_rows(d, step), :]) for d in range(2)]
           for step in range(ML_CHUNKS_PER_STEP)]
    for step in range(ML_CHUNKS_PER_STEP):
        for pair in range(ML_HEADS // 2):
            loaded = []
            for d in range(2):
                rows = chunk_rows(d, step)
                loaded.append((dirs[d][0][rows, pair * LANES:(pair + 1) * LANES],
                               dirs[d][1][rows, pair * LANES:(pair + 1) * LANES]))
            for half in range(2):
                for d in range(2):
                    q_pair, k_pair = loaded[d]
                    _mlstm_head(d, 2 * pair + half, pre[step][d], q_pair, k_pair,
                                dirs[d][2].at[chunk_of(d, step)], dirs[d][4].at[chunk_rows(d, step)],
                                ct_ref.at[d], m_ref.at[d])


def _mlstm(q, k, vt, gates, n_lat_ch):
    rows = q.shape[0]
    step_rows = ML_CHUNKS_PER_STEP * CHUNK
    assert rows % step_rows == 0 and (n_lat_ch * CHUNK) % step_rows == 0
    nblk = rows // step_rows
    n_lat_blk = n_lat_ch * CHUNK // step_rows
    fwd = lambda i: (i + n_lat_blk) % nblk
    bwd = lambda i: nblk - 1 - i
    vw = vt.shape[1]

    def blocks(order):
        blk = lambda w: pl.BlockSpec((step_rows, w), lambda i: (order(i), 0))
        vt_blk = pl.BlockSpec((ML_CHUNKS_PER_STEP, vw, CHUNK), lambda i: (order(i), 0, 0))
        return [blk(q.shape[1]), blk(k.shape[1]), vt_blk, blk(LANES)]

    out_blk = lambda order: pl.BlockSpec((step_rows, vw), lambda i: (order(i), 0))
    return pl.pallas_call(
        _mlstm_kernel,
        grid=(nblk,),
        in_specs=blocks(fwd) + blocks(bwd),
        out_specs=[out_blk(fwd), out_blk(bwd)],
        out_shape=[jax.ShapeDtypeStruct((rows, vw), F32)] * 2,
        scratch_shapes=[
            pltpu.VMEM((2, ML_HEADS, ML_V_DIM + ML_AUX_ROWS, LANES), F32),
            pltpu.VMEM((2, ML_HEADS, LANES), F32),
        ],
        compiler_params=_cparams("arbitrary"),
        name="mlstm",
    )(q, k, vt, gates, q, k, vt, gates)


def kernel(x, c, ctx, c_ctx, mod_w, mod_b, norm_g, ffn_w13, ffn_w2, ab_w_in, ab_gate_norm_g, ab_spatial_w, ab_spatial_b, ab_q_norm_g, ab_k_norm_g, ab_w_out, ml_w_in, ml_conv_w, ml_gate_b, ml_out_norm_g, ml_w_out, final_norm_g):
    b, t_lat, d = x.shape
    ctx_len = ctx.shape[1]
    depth = mod_w.shape[0]
    assert b == 1 and t_lat % TM == 0 and ctx_len % TM == 0 and t_lat % ATT_TK == 0
    n_lat_t = t_lat // TM
    n_all_t = (t_lat + ctx_len) // TM

    mods = _mods(c, c_ctx, mod_w, mod_b)
    w13_all = ffn_w13.astype(BF16)
    w2_all = ffn_w2.astype(BF16)
    h = x[0]
    for l in range(depth):
        keep_ctx = l < depth - 1
        mod = mods[l]
        h = _ffn(h, mod, norm_g[l, 0], w13_all, w2_all, l, 0, n_all_t, n_lat_t, 0,
                 ctx_rows=ctx[0] if l == 0 else None)
        n_out_t = n_all_t if keep_ctx else n_lat_t
        if l % 2 == 0:
            e = l // 2
            gated, q, kd0, kd1, vt = _even_proj(
                h, mod, norm_g[l, 1], ab_w_in[e], ab_gate_norm_g[e], ab_spatial_w[e], ab_spatial_b[e],
                ab_q_norm_g[e], ab_k_norm_g[e], t_lat, n_lat_t)
            attn = _attention(q, kd0, kd1, vt, ab_q_norm_g[e], ab_k_norm_g[e], t_lat, n_lat_t)
            mixer, mixer_args = "even", (gated, attn, ab_w_out[e].astype(BF16))
        else:
            o = l // 2
            pqk, vt, og, gates = _odd_proj(h, mod, norm_g[l, 1], ml_w_in[o], ml_gate_b[o], n_lat_t)
            q, k = _conv(pqk, ml_conv_w[o], t_lat)
            hf, hb = _mlstm(q, k, vt, gates, t_lat // CHUNK)
            mixer, mixer_args = "odd", (hf, hb, og, ml_out_norm_g[o].reshape(1, -1), ml_w_out[o].astype(BF16))
        final_g = final_norm_g if l == depth - 1 else None
        h = _ffn(h, mod, norm_g[l, 2], w13_all, w2_all, l, 1, n_out_t, n_lat_t, 6, final_g,
                 mixer=mixer, mixer_args=mixer_args)
    return h[:t_lat][None]
```

```python
import functools

import jax
import jax.numpy as jnp
from jax import lax
from jax.experimental import pallas as pl
from jax.experimental.pallas import tpu as pltpu

F32 = jnp.float32
BF16 = jnp.bfloat16

LANES = 128
SUBLANES = 8
MXU_TILE = 256
VMEM_LIMIT = 56 * 1024 * 1024

GRID_W = 64
CHUNK = 128
EPS = 1e-6
ROPE_THETA = 10000.0
LOG2_E = 1.4426950408889634
N_MOD = 9
A_GROUPS = 8
A_GROUP_DIM = 64
ATT_HEADS = 8
ATT_KV_HEADS = 2
ATT_HEAD_DIM = 64
ML_HEADS = 8
ML_QK_DIM = 64
ML_V_DIM = 128
CONV_WIDTH = 3

TM = 256
CONV_MAX_TILES = 5
ATT_TK = 4096
ATT_LOOKAHEAD = 3
ATT_LOOKAHEAD_BOUNDED = 2
ML_AUX_ROWS = 16
ML_CHUNKS_PER_STEP = 2
ATT_SCORE_BOUND = 60.0
V_ROWS = ATT_HEAD_DIM + 16


def _cparams(*sem):
    return pltpu.CompilerParams(dimension_semantics=sem, vmem_limit_bytes=VMEM_LIMIT)


def _full(shape):
    n = len(shape)
    return pl.BlockSpec(shape, lambda *_: (0,) * n)


def _rms_mod(x, g, shift, scale):
    ms = jnp.mean(x * x, axis=-1, keepdims=True)
    y = x * lax.rsqrt(ms + EPS) * g
    return y * (1 + scale) + shift


def _silu(x):
    return x * jax.nn.sigmoid(x)


def _split3(x):
    x1 = x.astype(BF16)
    r = x - x1.astype(F32)
    x2 = r.astype(BF16)
    x3 = (r - x2.astype(F32)).astype(BF16)
    return x1, x2, x3


def _group_mean_sq(x, bmat):
    w = bmat.shape[0]
    sq = x * x
    hi = sq.astype(BF16)
    lo = (sq - hi.astype(F32)).astype(BF16)
    slabs = [jnp.dot(hi[:, j:j + w], bmat, preferred_element_type=F32)
             + jnp.dot(lo[:, j:j + w], bmat, preferred_element_type=F32)
             for j in range(0, x.shape[1], w)]
    return slabs[0] if len(slabs) == 1 else jnp.concatenate(slabs, axis=1)


def _mod_kernel(c_ref, w_ref, b_ref, o_ref):
    w = w_ref[0]
    o_ref[0] = jnp.zeros(o_ref.shape[1:], F32)
    for r in range(c_ref.shape[0]):
        s = _silu(c_ref[r])
        cols = [jnp.sum(w[:, j * LANES:(j + 1) * LANES] * s, axis=0, keepdims=True)
                for j in range(w.shape[1] // LANES)]
        o_ref[0, r:r + 1, :] = jnp.concatenate(cols, axis=1) + b_ref[0]


def _mods(c, c_ctx, mod_w, mod_b):
    depth, d, nd = mod_w.shape
    crep = jnp.broadcast_to(jnp.stack([c[0], c_ctx])[:, :, None], (2, d, LANES))
    out = pl.pallas_call(
        _mod_kernel,
        grid=(depth, nd // d),
        in_specs=[
            pl.BlockSpec((2, d, LANES), lambda l, j: (0, 0, 0)),
            pl.BlockSpec((1, d, d), lambda l, j: (l, 0, j)),
            pl.BlockSpec((1, 1, d), lambda l, j: (l, 0, j)),
        ],
        out_specs=pl.BlockSpec((1, SUBLANES, d), lambda l, j: (l, 0, j)),
        out_shape=jax.ShapeDtypeStruct((depth, SUBLANES, nd), F32),
        compiler_params=_cparams("arbitrary", "arbitrary"),
        name="modulation",
    )(crep, mod_w, mod_b.reshape(depth, 1, nd))
    return out[:, :2].reshape(depth, 2, N_MOD, d)


def _even_mixer_out(a_ref, b_ref, w_ref):
    aw = a_ref.shape[1]
    return (jnp.dot(a_ref[...], w_ref[:aw], preferred_element_type=F32)
            + jnp.dot(b_ref[...], w_ref[aw:], preferred_element_type=F32))


def _odd_mixer_out(hf_ref, hb_ref, og_ref, ng_ref, w_ref):
    hs = hf_ref[...] + hb_ref[...]
    parts = []
    for hh in range(ML_HEADS):
        blk = hs[:, hh * ML_V_DIM:(hh + 1) * ML_V_DIM]
        ms = jnp.mean(blk * blk, axis=-1, keepdims=True)
        parts.append(blk * lax.rsqrt(ms + EPS))
    hn = jnp.concatenate(parts, axis=1) * ng_ref[...]
    gated = (hn * jax.nn.sigmoid(og_ref[...])).astype(BF16)
    return jnp.dot(gated, w_ref[...], preferred_element_type=F32)


_MIXER_OUT = {"even": (_even_mixer_out, 3), "odd": (_odd_mixer_out, 5)}


def _ffn_kernel(h_ref, mod_ref, g_ref, w13_ref, w2_ref, *rest, mod_base, final, n_lat_t, split_input, mixer):
    o_ref = rest[-1]
    x = h_ref[...]
    m = mod_ref[0]
    if split_input:
        x = jnp.where(pl.program_id(0) < n_lat_t, x, rest[0][...])
        rest = rest[1:]
    if mixer is not None:
        fn, n_ops = _MIXER_OUT[mixer]
        x = x + m[5:6] * fn(*rest[:n_ops])
        rest = rest[n_ops:]
    z = _rms_mod(x, g_ref[...], m[mod_base:mod_base + 1], m[mod_base + 1:mod_base + 2]).astype(BF16)
    f = w2_ref.shape[0]
    gu = jnp.dot(z, w13_ref[...], preferred_element_type=F32)
    a = (_silu(gu[:, :f]) * gu[:, f:]).astype(BF16)
    y = jnp.dot(a, w2_ref[...], preferred_element_type=F32)
    out = x + (0.5 * m[mod_base + 2:mod_base + 3]) * y
    if final:
        ms = jnp.mean(out * out, axis=-1, keepdims=True)
        out = out * lax.rsqrt(ms + EPS) * rest[0][...]
    o_ref[...] = out


def _ffn(h, mod, g, w13_all, w2_all, layer, which, n_tiles, n_lat_t, mod_base, final_g=None, ctx_rows=None,
         mixer=None, mixer_args=()):
    d = h.shape[1]
    f = w2_all.shape[2]
    final = final_g is not None
    split_input = ctx_rows is not None
    last_lat = n_lat_t - 1
    row_tile = lambda w: pl.BlockSpec((TM, w), lambda i: (i, 0))
    in_specs = [
        pl.BlockSpec((TM, d), (lambda i: (jnp.minimum(i, last_lat), 0)) if split_input else (lambda i: (i, 0))),
        pl.BlockSpec((1, N_MOD, d), lambda i: (i // n_lat_t, 0, 0)),
        _full((1, d)),
        pl.BlockSpec((None, None, d, 2 * f), lambda i: (layer, which, 0, 0)),
        pl.BlockSpec((None, None, f, d), lambda i: (layer, which, 0, 0)),
    ]
    args = [h, mod, g.reshape(1, d), w13_all, w2_all]
    if split_input:
        in_specs.append(pl.BlockSpec((TM, d), lambda i: (jnp.maximum(i - n_lat_t, 0), 0)))
        args.append(ctx_rows)
    for arr in mixer_args:
        in_specs.append(row_tile(arr.shape[1]) if arr.shape[0] == h.shape[0] else _full(arr.shape))
        args.append(arr)
    if final:
        in_specs.append(_full((1, d)))
        args.append(final_g.reshape(1, d))
    return pl.pallas_call(
        functools.partial(_ffn_kernel, mod_base=mod_base, final=final, n_lat_t=n_lat_t, split_input=split_input,
                          mixer=mixer),
        grid=(n_tiles,),
        in_specs=in_specs,
        out_specs=row_tile(d),
        out_shape=jax.ShapeDtypeStruct((n_tiles * TM, d), F32),
        compiler_params=_cparams("arbitrary"),
        name="ffn",
    )(*args)


def _even_proj_kernel(h_ref, mod_ref, g_ref, win_ref, gng_ref, qg_ref, kg_ref, spw_ref, spb_ref,
                      cos_r_ref, sin_r_ref, cos_c_ref, sin_c_ref, b256_ref, b128_ref,
                      gated_ref, q_ref, kd0_ref, kd1_ref, vt_ref, *, n_lat_t):
    x = h_ref[...]
    m = mod_ref[0]
    z = _rms_mod(x, g_ref[...], m[3:4], m[4:5]).astype(BF16)
    p = jnp.dot(z, win_ref[...], preferred_element_type=F32)
    aw = A_GROUPS * A_GROUP_DIM
    qw = ATT_HEADS * ATT_HEAD_DIM
    u = jax.nn.gelu(p[:, :aw])
    v = jax.nn.gelu(p[:, aw:2 * aw])
    b256 = b256_ref[...]
    vn = (v * lax.rsqrt(_group_mean_sq(v, b256) + EPS) * gng_ref[...]).astype(BF16)

    lane = lax.broadcasted_iota(jnp.int32, (CHUNK, LANES), 1)
    lo = lane < A_GROUP_DIM
    for c in range(TM // CHUNK):
        r0 = c * CHUNK
        for j in range(aw // LANES):
            vt = vn[r0:r0 + CHUNK, j * LANES:(j + 1) * LANES]
            s0 = jnp.dot(spw_ref[2 * j], vt, preferred_element_type=F32)
            s1 = jnp.dot(spw_ref[2 * j + 1], vt, preferred_element_type=F32)
            s = jnp.where(lo, s0, s1) + spb_ref[:, j * LANES:(j + 1) * LANES]
            gated_ref[r0:r0 + CHUNK, j * LANES:(j + 1) * LANES] = (
                u[r0:r0 + CHUNK, j * LANES:(j + 1) * LANES] * s).astype(BF16)

    o = 2 * aw
    pq = p[:, o:o + qw]
    pk = p[:, o + qw:o + qw + LANES]
    va = p[:, o + qw + LANES:o + qw + 2 * LANES]
    qn = pq * lax.rsqrt(_group_mean_sq(pq, b256) + EPS) * qg_ref[...]
    kn = pk * lax.rsqrt(_group_mean_sq(pk, b128_ref[...]) + EPS) * kg_ref[...]

    is_lat = pl.program_id(0) < n_lat_t
    lane_g = lax.broadcasted_iota(jnp.int32, (GRID_W, LANES), 1)
    row_lanes = (lane_g % ATT_HEAD_DIM) < ATT_HEAD_DIM // 2

    def tile_table(r_ref, c_ref, ctx_value):
        col_part = jnp.where(is_lat, c_ref[...], ctx_value)
        return jnp.concatenate([jnp.where(row_lanes, r_ref[gr:gr + 1, :], col_part)
                                for gr in range(TM // GRID_W)], axis=0)

    cos = tile_table(cos_r_ref, cos_c_ref, 1.0)
    sin = tile_table(sin_r_ref, sin_c_ref, 0.0)
    lane_t = lax.broadcasted_iota(jnp.int32, (TM, LANES), 1)
    first_half = (lane_t % 32) < 16
    lo_t = lane_t < ATT_HEAD_DIM

    def rope(t):
        partner = jnp.where(first_half, pltpu.roll(t, LANES - 16, 1), pltpu.roll(t, 16, 1))
        return t * cos + partner * sin

    scale = ATT_HEAD_DIM ** -0.5 * LOG2_E
    for j in range(qw // LANES):
        q_ref[:, j * LANES:(j + 1) * LANES] = (rope(qn[:, j * LANES:(j + 1) * LANES]) * scale).astype(BF16)
    kr = rope(kn)
    kswap = pltpu.roll(kr, ATT_HEAD_DIM, 1)
    kd0_ref[...] = jnp.where(lo_t, kr, kswap).astype(BF16)
    kd1_ref[...] = jnp.where(lo_t, kswap, kr).astype(BF16)
    va_t = va.T
    sub = lax.broadcasted_iota(jnp.int32, (V_ROWS - ATT_HEAD_DIM, TM), 0)
    ones_rows = jnp.where(sub == 0, 1.0, 0.0).astype(BF16)
    for g in range(ATT_KV_HEADS):
        vt_ref[g, 0, :ATT_HEAD_DIM, :] = va_t[g * ATT_HEAD_DIM:(g + 1) * ATT_HEAD_DIM].astype(BF16)
        vt_ref[g, 0, ATT_HEAD_DIM:, :] = ones_rows


def _group_matrix(width, gs):
    r = jnp.arange(width) // gs
    return jnp.where(r[:, None] == r[None, :], 1.0 / gs, 0.0).astype(BF16)


def _rope_tables(t_lat, rows):
    axis_dim = ATT_HEAD_DIM // 2
    n_freq = axis_dim // 2
    inv_freq = ROPE_THETA ** (-jnp.arange(0, axis_dim, 2, dtype=F32) / axis_dim)
    per_tile = TM // GRID_W
    n_lat_t = t_lat // TM
    n_ctx_t = (rows - t_lat) // TM
    lane = jnp.arange(LANES)
    sign = jnp.where((lane % axis_dim) < n_freq, -1.0, 1.0)
    ang_r = jnp.arange(t_lat // GRID_W, dtype=F32)[:, None] * inv_freq[lane % n_freq][None, :]
    ang_r = jnp.pad(ang_r.reshape(n_lat_t, per_tile, LANES), ((0, n_ctx_t), (0, SUBLANES - per_tile), (0, 0)))
    ang_r = ang_r.reshape((n_lat_t + n_ctx_t) * SUBLANES, LANES)
    ang_c = jnp.arange(GRID_W, dtype=F32)[:, None] * inv_freq[lane % n_freq][None, :]
    return jnp.cos(ang_r), jnp.sin(ang_r) * sign, jnp.cos(ang_c), jnp.sin(ang_c) * sign


def _even_proj(h, mod, g, w_in, gate_norm_g, sp_w, sp_b, q_norm_g, k_norm_g, t_lat, n_lat_t):
    rows, d = h.shape
    aw = A_GROUPS * A_GROUP_DIM
    qw = ATT_HEADS * ATT_HEAD_DIM
    cos_r, sin_r, cos_c, sin_c = _rope_tables(t_lat, rows)
    spb = jnp.repeat(sp_b.T, A_GROUP_DIM, axis=1)
    row_tile = lambda w: pl.BlockSpec((TM, w), lambda i: (i, 0))
    out_shape = [jax.ShapeDtypeStruct((rows, aw), BF16), jax.ShapeDtypeStruct((rows, qw), BF16)]
    out_shape += [jax.ShapeDtypeStruct((rows, LANES), BF16)] * 2
    out_shape += [jax.ShapeDtypeStruct((ATT_KV_HEADS, rows // TM, V_ROWS, TM), BF16)]
    vt_spec = pl.BlockSpec((ATT_KV_HEADS, 1, V_ROWS, TM), lambda i: (0, i, 0, 0))
    return pl.pallas_call(
        functools.partial(_even_proj_kernel, n_lat_t=n_lat_t),
        grid=(rows // TM,),
        in_specs=[
            row_tile(d),
            pl.BlockSpec((1, N_MOD, d), lambda i: (i // n_lat_t, 0, 0)),
            _full((1, d)),
            _full(w_in.shape),
            _full((1, aw)),
            _full((1, qw)),
            _full((1, LANES)),
            _full(sp_w.shape),
            _full((CHUNK, aw)),
            pl.BlockSpec((SUBLANES, LANES), lambda i: (i, 0)),
            pl.BlockSpec((SUBLANES, LANES), lambda i: (i, 0)),
            _full((GRID_W, LANES)),
            _full((GRID_W, LANES)),
            _full((MXU_TILE, MXU_TILE)),
            _full((LANES, LANES)),
        ],
        out_specs=[row_tile(aw), row_tile(qw), row_tile(LANES), row_tile(LANES), vt_spec],
        out_shape=out_shape,
        compiler_params=_cparams("arbitrary"),
        name="even_proj",
    )(h, mod, g.reshape(1, d), w_in.astype(BF16), gate_norm_g.reshape(1, aw),
      jnp.tile(q_norm_g, ATT_HEADS).reshape(1, qw), jnp.tile(k_norm_g, ATT_KV_HEADS).reshape(1, LANES),
      sp_w.astype(BF16), spb, cos_r, sin_r, cos_c, sin_c,
      _group_matrix(MXU_TILE, A_GROUP_DIM), _group_matrix(LANES, ATT_HEAD_DIM))


def _attn_kernel(bounded_ref, q_ref, kd0_ref, kd1_ref, vt_ref, o_ref, qs_ref, m_ref, acc_ref,
                 *, n_lat_t, t_lat, ctx_len):
    i = pl.program_id(0)
    tq = q_ref.shape[0]
    lane = lax.broadcasted_iota(jnp.int32, (tq, LANES), 1)
    lo = lane < ATT_HEAD_DIM
    for h in range(ATT_HEADS):
        t = q_ref[:, (h // 2) * LANES:(h // 2 + 1) * LANES]
        qs_ref[h * tq:(h + 1) * tq] = jnp.where(lo if h % 2 == 0 else ~lo, t, jnp.zeros_like(t))

    kd_refs = (kd0_ref, kd1_ref)
    n_streams = ATT_HEADS // 2
    blocks_per_chunk = ATT_TK // TM
    nt_dims = (((1,), (1,)), ((), ()))

    def group_of(s):
        return s // (n_streams // ATT_KV_HEADS)

    def scores(s, k_start, n_blk):
        kc = kd_refs[group_of(s)][pl.ds(k_start, n_blk * TM), :]
        qs = qs_ref[2 * s * tq:(2 * s + 2) * tq]
        return lax.dot_general(kc, qs, nt_dims, preferred_element_type=F32)

    def p_times_v(s, pt, n_blk, blk0):
        pv = None
        for j in range(n_blk):
            part = jnp.dot(vt_ref[group_of(s), blk0 + j], pt[j * TM:(j + 1) * TM], preferred_element_type=F32)
            pv = part if pv is None else pv + part
        return pv

    def softmax_pv_online(s, st, n_blk, blk0):
        m_old = m_ref[s]
        m_new = jnp.maximum(m_old, jnp.max(st, axis=0, keepdims=True))
        alpha = jnp.exp2(m_old - m_new)
        pt = jnp.exp2(st - m_new).astype(BF16)
        m_ref[s] = m_new
        acc_ref[s] = alpha * acc_ref[s] + p_times_v(s, pt, n_blk, blk0)

    def softmax_pv_bounded(s, st, n_blk, blk0):
        pt = jnp.exp2(st).astype(BF16)
        acc_ref[s] += p_times_v(s, pt, n_blk, blk0)

    def sweep(softmax_pv, lookahead):
        def step(k_start, n_blk, blk0):
            pending = [scores(s, k_start, n_blk) for s in range(min(lookahead, n_streams))]
            for s in range(n_streams):
                st = pending.pop(0)
                if s + lookahead < n_streams:
                    pending.append(scores(s + lookahead, k_start, n_blk))
                softmax_pv(s, st, n_blk, blk0)

        def body(c, carry):
            step(pl.multiple_of(c * ATT_TK, ATT_TK), blocks_per_chunk, c * blocks_per_chunk)
            return carry

        n_full = jnp.where(i < n_lat_t, t_lat // ATT_TK, 0)
        lax.fori_loop(0, n_full, body, 0)
        step(t_lat, ctx_len // TM, t_lat // TM)

    acc_ref[...] = jnp.zeros(acc_ref.shape, F32)
    bounded = bounded_ref[0] == 1

    @pl.when(bounded)
    def _():
        sweep(softmax_pv_bounded, ATT_LOOKAHEAD_BOUNDED)

    @pl.when(jnp.logical_not(bounded))
    def _():
        m_ref[...] = jnp.full(m_ref.shape, -jnp.inf, F32)
        sweep(softmax_pv_online, ATT_LOOKAHEAD)

    for s in range(n_streams):
        acc = acc_ref[s]
        o_t = acc[:ATT_HEAD_DIM] * (1.0 / acc[ATT_HEAD_DIM:ATT_HEAD_DIM + 1])
        pair = jnp.concatenate([o_t[:, :tq], o_t[:, tq:]], axis=0)
        o_ref[:, s * LANES:(s + 1) * LANES] = pair.T.astype(BF16)


def _attention(q, kd0, kd1, vt, q_norm_g, k_norm_g, t_lat, n_lat_t):
    rows, qw = q.shape
    ctx_len = rows - t_lat
    score_bound = (ATT_HEAD_DIM ** 0.5 * LOG2_E) * jnp.max(jnp.abs(q_norm_g)) * jnp.max(jnp.abs(k_norm_g))
    bounded = (score_bound <= ATT_SCORE_BOUND).astype(jnp.int32).reshape(1)
    return pl.pallas_call(
        functools.partial(_attn_kernel, n_lat_t=n_lat_t, t_lat=t_lat, ctx_len=ctx_len),
        grid=(rows // TM,),
        in_specs=[pl.BlockSpec(memory_space=pltpu.SMEM),
                  pl.BlockSpec((TM, qw), lambda i: (i, 0)), _full((rows, LANES)), _full((rows, LANES)),
                  _full(vt.shape)],
        out_specs=pl.BlockSpec((TM, qw), lambda i: (i, 0)),
        out_shape=jax.ShapeDtypeStruct((rows, qw), BF16),
        scratch_shapes=[
            pltpu.VMEM((ATT_HEADS * TM, LANES), BF16),
            pltpu.VMEM((ATT_HEADS // 2, 1, 2 * TM), F32),
            pltpu.VMEM((ATT_HEADS // 2, V_ROWS, 2 * TM), F32),
        ],
        compiler_params=_cparams("arbitrary"),
        name="gqa_attention",
    )(bounded, q, kd0, kd1, vt)


def _odd_proj_kernel(h_ref, mod_ref, g_ref, win_ref, gb_ref, pqk_ref, vt_ref, o_ref, gates_ref):
    x = h_ref[...]
    m = mod_ref[0]
    z = _rms_mod(x, g_ref[...], m[3:4], m[4:5]).astype(BF16)
    p = jnp.dot(z, win_ref[...], preferred_element_type=F32)
    qk = 2 * ML_HEADS * ML_QK_DIM
    vw = ML_HEADS * ML_V_DIM
    pqk_ref[...] = p[:, :qk]
    for c in range(TM // CHUNK):
        vt_ref[c] = p[c * CHUNK:(c + 1) * CHUNK, qk:qk + vw].T.astype(BF16)
    o_ref[...] = p[:, qk + vw:qk + 2 * vw]
    gts = p[:, qk + 2 * vw:] + gb_ref[...]
    lane = lax.broadcasted_iota(jnp.int32, gts.shape, 1)
    is_f = (lane < 4 * ML_HEADS) & ((lane // ML_HEADS) % 2 == 1)
    log_sig = jnp.minimum(gts, 0.0) - jnp.log1p(jnp.exp(-jnp.abs(gts)))
    gates_ref[...] = jnp.where(is_f, log_sig, gts)


def _odd_proj(h, mod, g, w_in, gate_b, n_lat_t):
    rows, d = h.shape
    qk = 2 * ML_HEADS * ML_QK_DIM
    vw = ML_HEADS * ML_V_DIM
    n_gate = 4 * ML_HEADS
    w_pad = jnp.pad(w_in, ((0, 0), (0, LANES - n_gate))).astype(BF16)
    gb = jnp.pad(gate_b, (0, LANES - n_gate)).reshape(1, LANES)
    row_tile = lambda w: pl.BlockSpec((TM, w), lambda i: (i, 0))
    return pl.pallas_call(
        _odd_proj_kernel,
        grid=(rows // TM,),
        in_specs=[row_tile(d), pl.BlockSpec((1, N_MOD, d), lambda i: (i // n_lat_t, 0, 0)),
                  _full((1, d)), _full(w_pad.shape), _full((1, LANES))],
        out_specs=[row_tile(qk), pl.BlockSpec((TM // CHUNK, vw, CHUNK), lambda i: (i, 0, 0)),
                   row_tile(vw), row_tile(LANES)],
        out_shape=[jax.ShapeDtypeStruct((rows, qk), F32), jax.ShapeDtypeStruct((rows // CHUNK, vw, CHUNK), BF16),
                   jax.ShapeDtypeStruct((rows, vw), F32), jax.ShapeDtypeStruct((rows, LANES), F32)],
        compiler_params=_cparams("arbitrary"),
        name="odd_proj",
    )(h, mod, g.reshape(1, d), w_pad, gb)


def _conv_kernel(x_ref, prev_ref, next_ref, w_ref, q_ref, k_ref, *, t_lat, rows):
    x = x_ref[...]
    tm = x.shape[0]
    row = lax.broadcasted_iota(jnp.int32, x.shape, 0)
    token = row + pl.program_id(0) * tm
    xm = jnp.where(row == 0, prev_ref[SUBLANES - 1:SUBLANES, :], pltpu.roll(x, 1, 0))
    xp = jnp.where(row == tm - 1, next_ref[0:1, :], pltpu.roll(x, tm - 1, 0))
    xm = jnp.where((token == 0) | (token == t_lat), 0.0, xm)
    xp = jnp.where((token == t_lat - 1) | (token == rows - 1), 0.0, xp)
    y = xm * w_ref[0:1, :] + x * w_ref[1:2, :] + xp * w_ref[2:3, :]
    y = _silu(y)
    qk = ML_HEADS * ML_QK_DIM
    q_ref[...] = (y[:, :qk] * (ML_QK_DIM ** -0.5)).astype(BF16)
    k_ref[...] = y[:, qk:].astype(BF16)


def _conv(pqk, conv_w, t_lat):
    rows, w = pqk.shape
    n_tiles = rows // TM
    tm = TM * max(t for t in range(1, CONV_MAX_TILES + 1) if n_tiles % t == 0)
    per = tm // SUBLANES
    last_blk = rows // SUBLANES - 1
    return pl.pallas_call(
        functools.partial(_conv_kernel, t_lat=t_lat, rows=rows),
        grid=(rows // tm,),
        in_specs=[
            pl.BlockSpec((tm, w), lambda i: (i, 0)),
            pl.BlockSpec((SUBLANES, w), lambda i: (jnp.maximum(i * per - 1, 0), 0)),
            pl.BlockSpec((SUBLANES, w), lambda i: (jnp.minimum((i + 1) * per, last_blk), 0)),
            _full(conv_w.shape),
        ],
        out_specs=[pl.BlockSpec((tm, w // 2), lambda i: (i, 0))] * 2,
        out_shape=[jax.ShapeDtypeStruct((rows, w // 2), BF16)] * 2,
        compiler_params=_cparams("arbitrary"),
        name="short_conv",
    )(pqk, pqk, pqk, conv_w)


def _mlstm_prepare(d, g):
    ln = CHUNK
    r = lax.broadcasted_iota(jnp.int32, (ln, ln), 0)
    c = lax.broadcasted_iota(jnp.int32, (ln, ln), 1)
    tri = (c <= r) if d == 0 else (c >= r)
    tri_b = jnp.where(tri, 1.0, 0.0).astype(BF16)
    g_t = g.T
    nt_dims = (((1,), (1,)), ((), ()))
    bc_col = sum(jnp.dot(tri_b, piece, preferred_element_type=F32) for piece in _split3(g))
    bc_row = sum(lax.dot_general(piece, tri_b, nt_dims, preferred_element_type=F32) for piece in _split3(g_t))
    il0 = 2 * ML_HEADS * d
    fl0 = il0 + ML_HEADS
    u = g_t[il0:il0 + ML_HEADS, :] - bc_row[fl0:fl0 + ML_HEADS, :]
    lane = lax.broadcasted_iota(jnp.int32, u.shape, 1)
    sh = 1
    while sh < ln:
        if d == 0:
            shifted = jnp.where(lane >= sh, pltpu.roll(u, sh, 1), -jnp.inf)
        else:
            shifted = jnp.where(lane < ln - sh, pltpu.roll(u, ln - sh, 1), -jnp.inf)
        u = jnp.maximum(u, shifted)
        sh *= 2
    vis = (c >= r) if d == 0 else (c <= r)
    return dict(vis=vis, g=g, g_t=g_t, bc_col=bc_col, bc_row=bc_row, run_max=u,
                lane_lo=c < ML_QK_DIM, last=ln - 1 if d == 0 else 0)


def _mlstm_head(d, h, pre, q_pair, k_pair, vt_ref, o_ref, c_ref, m_ref):
    half = h % 2
    nt_dims = (((1,), (1,)), ((), ()))
    il = 2 * ML_HEADS * d + h
    fl = il + ML_HEADS
    keep_lane = pre["lane_lo"] if half == 0 else ~pre["lane_lo"]
    brow = pre["bc_row"][fl:fl + 1, :]
    irow = pre["g_t"][il:il + 1, :]
    src_col = pre["g"][:, il:il + 1] - pre["bc_col"][:, fl:fl + 1]
    btot = pre["bc_col"][pre["last"]:pre["last"] + 1, fl:fl + 1]
    m_prev = m_ref[h:h + 1, 0:1]

    inter = brow + m_prev
    m_row = jnp.maximum(inter, brow + pre["run_max"][h:h + 1, :])
    wts_t = jnp.exp(jnp.where(pre["vis"], (brow - m_row) + src_col, -jnp.inf))
    a = jnp.exp(inter - m_row)

    qm = jnp.where(keep_lane, q_pair, jnp.zeros_like(q_pair))
    s_t = lax.dot_general(k_pair, qm, nt_dims, preferred_element_type=F32) * wts_t
    vt = vt_ref[h * ML_V_DIM:(h + 1) * ML_V_DIM, :]
    vt_aug = jnp.concatenate([vt, jnp.ones((ML_AUX_ROWS, vt.shape[1]), BF16)], axis=0)
    c_aug = c_ref[h]
    num_den = (jnp.dot(vt_aug, s_t.astype(BF16), preferred_element_type=F32)
               + a * lax.dot_general(c_aug.astype(BF16), qm, nt_dims, preferred_element_type=F32))
    den = num_den[ML_V_DIM:ML_V_DIM + 1]
    h_t = num_den[:ML_V_DIM] / jnp.maximum(jnp.abs(den), jnp.exp(-m_row))
    o_ref[:, h * LANES:(h + 1) * LANES] = h_t.T

    w_row = btot - brow + irow
    m_loc = jnp.max(w_row, axis=1, keepdims=True)
    e = jnp.exp(w_row - m_loc)
    ev_t = (vt_aug.astype(F32) * e).astype(BF16)
    km = jnp.where(keep_lane, k_pair, jnp.zeros_like(k_pair))
    c_loc = jnp.dot(ev_t, km, preferred_element_type=F32)
    m_new = jnp.maximum(btot + m_prev, m_loc)
    decay = jnp.exp(btot + m_prev - m_new)
    gain = jnp.exp(m_loc - m_new)
    c_ref[h] = decay * c_aug + gain * c_loc
    m_ref[h:h + 1, :] = jnp.broadcast_to(m_new, (1, LANES))


def _mlstm_kernel(qf_ref, kf_ref, vf_ref, gf_ref, qb_ref, kb_ref, vb_ref, gb_ref, of_ref, ob_ref,
                  ct_ref, m_ref):
    @pl.when(pl.program_id(0) == 0)
    def _():
        ct_ref[...] = jnp.zeros(ct_ref.shape, F32)
        m_ref[...] = jnp.zeros(m_ref.shape, F32)

    dirs = ((qf_ref, kf_ref, vf_ref, gf_ref, of_ref), (qb_ref, kb_ref, vb_ref, gb_ref, ob_ref))

    def chunk_of(d, step):
        return step if d == 0 else ML_CHUNKS_PER_STEP - 1 - step

    def chunk_rows(d, step):
        return pl.ds(chunk_of(d, step) * CHUNK, CHUNK)

    pre = [[_mlstm_prepare(d, dirs[d][3][chunk_rows(d, step), :]) for d in range(2)]
           for step in range(ML_CHUNKS_PER_STEP)]
    for step in range(ML_CHUNKS_PER_STEP):
        for pair in range(ML_HEADS // 2):
            loaded = []
            for d in range(2):
                rows = chunk_rows(d, step)
                loaded.append((dirs[d][0][rows, pair * LANES:(pair + 1) * LANES],
                               dirs[d][1][rows, pair * LANES:(pair + 1) * LANES]))
            for half in range(2):
                for d in range(2):
                    q_pair, k_pair = loaded[d]
                    _mlstm_head(d, 2 * pair + half, pre[step][d], q_pair, k_pair,
                                dirs[d][2].at[chunk_of(d, step)], dirs[d][4].at[chunk_rows(d, step)],
                                ct_ref.at[d], m_ref.at[d])


def _mlstm(q, k, vt, gates, n_lat_ch):
    rows = q.shape[0]
    step_rows = ML_CHUNKS_PER_STEP * CHUNK
    assert rows % step_rows == 0 and (n_lat_ch * CHUNK) % step_rows == 0
    nblk = rows // step_rows
    n_lat_blk = n_lat_ch * CHUNK // step_rows
    fwd = lambda i: (i + n_lat_blk) % nblk
    bwd = lambda i: nblk - 1 - i
    vw = vt.shape[1]

    def blocks(order):
        blk = lambda w: pl.BlockSpec((step_rows, w), lambda i: (order(i), 0))
        vt_blk = pl.BlockSpec((ML_CHUNKS_PER_STEP, vw, CHUNK), lambda i: (order(i), 0, 0))
        return [blk(q.shape[1]), blk(k.shape[1]), vt_blk, blk(LANES)]

    out_blk = lambda order: pl.BlockSpec((step_rows, vw), lambda i: (order(i), 0))
    return pl.pallas_call(
        _mlstm_kernel,
        grid=(nblk,),
        in_specs=blocks(fwd) + blocks(bwd),
        out_specs=[out_blk(fwd), out_blk(bwd)],
        out_shape=[jax.ShapeDtypeStruct((rows, vw), F32)] * 2,
        scratch_shapes=[
            pltpu.VMEM((2, ML_HEADS, ML_V_DIM + ML_AUX_ROWS, LANES), F32),
            pltpu.VMEM((2, ML_HEADS, LANES), F32),
        ],
        compiler_params=_cparams("arbitrary"),
        name="mlstm",
    )(q, k, vt, gates, q, k, vt, gates)


def kernel(x, c, ctx, c_ctx, mod_w, mod_b, norm_g, ffn_w13, ffn_w2, ab_w_in, ab_gate_norm_g, ab_spatial_w, ab_spatial_b, ab_q_norm_g, ab_k_norm_g, ab_w_out, ml_w_in, ml_conv_w, ml_gate_b, ml_out_norm_g, ml_w_out, final_norm_g):
    b, t_lat, d = x.shape
    ctx_len = ctx.shape[1]
    depth = mod_w.shape[0]
    assert b == 1 and t_lat % TM == 0 and ctx_len % TM == 0 and t_lat % ATT_TK == 0
    n_lat_t = t_lat // TM
    n_all_t = (t_lat + ctx_len) // TM

    mods = _mods(c, c_ctx, mod_w, mod_b)
    w13_all = ffn_w13.astype(BF16)
    w2_all = ffn_w2.astype(BF16)
    h = x[0]
    for l in range(depth):
        keep_ctx = l < depth - 1
        mod = mods[l]
        h = _ffn(h, mod, norm_g[l, 0], w13_all, w2_all, l, 0, n_all_t, n_lat_t, 0,
                 ctx_rows=ctx[0] if l == 0 else None)
        n_out_t = n_all_t if keep_ctx else n_lat_t
        if l % 2 == 0:
            e = l // 2
            gated, q, kd0, kd1, vt = _even_proj(
                h, mod, norm_g[l, 1], ab_w_in[e], ab_gate_norm_g[e], ab_spatial_w[e], ab_spatial_b[e],
                ab_q_norm_g[e], ab_k_norm_g[e], t_lat, n_lat_t)
            attn = _attention(q, kd0, kd1, vt, ab_q_norm_g[e], ab_k_norm_g[e], t_lat, n_lat_t)
            mixer, mixer_args = "even", (gated, attn, ab_w_out[e].astype(BF16))
        else:
            o = l // 2
            pqk, vt, og, gates = _odd_proj(h, mod, norm_g[l, 1], ml_w_in[o], ml_gate_b[o], n_lat_t)
            q, k = _conv(pqk, ml_conv_w[o], t_lat)
            hf, hb = _mlstm(q, k, vt, gates, t_lat // CHUNK)
            mixer, mixer_args = "odd", (hf, hb, og, ml_out_norm_g[o].reshape(1, -1), ml_w_out[o].astype(BF16))
        final_g = final_norm_g if l == depth - 1 else None
        h = _ffn(h, mod, norm_g[l, 2], w13_all, w2_all, l, 1, n_out_t, n_lat_t, 6, final_g,
                 mixer=mixer, mixer_args=mixer_args)
    return h[:t_lat][None]
```

```python
import functools

import jax
import jax.numpy as jnp
from jax import lax
from jax.experimental import pallas as pl
from jax.experimental.pallas import tpu as pltpu

F32 = jnp.float32
BF16 = jnp.bfloat16

LANES = 128
SUBLANES = 8
MXU_TILE = 256
VMEM_LIMIT = 56 * 1024 * 1024

GRID_W = 64
CHUNK = 128
EPS = 1e-6
ROPE_THETA = 10000.0
LOG2_E = 1.4426950408889634
N_MOD = 9
A_GROUPS = 8
A_GROUP_DIM = 64
ATT_HEADS = 8
ATT_KV_HEADS = 2
ATT_HEAD_DIM = 64
ML_HEADS = 8
ML_QK_DIM = 64
ML_V_DIM = 128
CONV_WIDTH = 3

TM = 256
CONV_MAX_TILES = 5
ATT_TK = 4096
ATT_LOOKAHEAD = 3
ATT_LOOKAHEAD_BOUNDED = 2
ML_AUX_ROWS = 16
ML_CHUNKS_PER_STEP = 2
ATT_SCORE_BOUND = 60.0
V_ROWS = ATT_HEAD_DIM + 16


def _cparams(*sem):
    return pltpu.CompilerParams(dimension_semantics=sem, vmem_limit_bytes=VMEM_LIMIT)


def _full(shape):
    n = len(shape)
    return pl.BlockSpec(shape, lambda *_: (0,) * n)


def _rms_mod(x, g, shift, scale):
    ms = jnp.mean(x * x, axis=-1, keepdims=True)
    y = x * lax.rsqrt(ms + EPS) * g
    return y * (1 + scale) + shift


def _silu(x):
    return x * jax.nn.sigmoid(x)


def _split3(x):
    x1 = x.astype(BF16)
    r = x - x1.astype(F32)
    x2 = r.astype(BF16)
    x3 = (r - x2.astype(F32)).astype(BF16)
    return x1, x2, x3


def _group_mean_sq(x, bmat):
    w = bmat.shape[0]
    sq = x * x
    hi = sq.astype(BF16)
    lo = (sq - hi.astype(F32)).astype(BF16)
    slabs = [jnp.dot(hi[:, j:j + w], bmat, preferred_element_type=F32)
             + jnp.dot(lo[:, j:j + w], bmat, preferred_element_type=F32)
             for j in range(0, x.shape[1], w)]
    return slabs[0] if len(slabs) == 1 else jnp.concatenate(slabs, axis=1)


def _mod_kernel(c_ref, w_ref, b_ref, o_ref):
    w = w_ref[0]
    o_ref[0] = jnp.zeros(o_ref.shape[1:], F32)
    for r in range(c_ref.shape[0]):
        s = _silu(c_ref[r])
        cols = [jnp.sum(w[:, j * LANES:(j + 1) * LANES] * s, axis=0, keepdims=True)
                for j in range(w.shape[1] // LANES)]
        o_ref[0, r:r + 1, :] = jnp.concatenate(cols, axis=1) + b_ref[0]


def _mods(c, c_ctx, mod_w, mod_b):
    depth, d, nd = mod_w.shape
    crep = jnp.broadcast_to(jnp.stack([c[0], c_ctx])[:, :, None], (2, d, LANES))
    out = pl.pallas_call(
        _mod_kernel,
        grid=(depth, nd // d),
        in_specs=[
            pl.BlockSpec((2, d, LANES), lambda l, j: (0, 0, 0)),
            pl.BlockSpec((1, d, d), lambda l, j: (l, 0, j)),
            pl.BlockSpec((1, 1, d), lambda l, j: (l, 0, j)),
        ],
        out_specs=pl.BlockSpec((1, SUBLANES, d), lambda l, j: (l, 0, j)),
        out_shape=jax.ShapeDtypeStruct((depth, SUBLANES, nd), F32),
        compiler_params=_cparams("arbitrary", "arbitrary"),
        name="modulation",
    )(crep, mod_w, mod_b.reshape(depth, 1, nd))
    return out[:, :2].reshape(depth, 2, N_MOD, d)


def _even_mixer_out(a_ref, b_ref, w_ref):
    aw = a_ref.shape[1]
    return (jnp.dot(a_ref[...], w_ref[:aw], preferred_element_type=F32)
            + jnp.dot(b_ref[...], w_ref[aw:], preferred_element_type=F32))


def _odd_mixer_out(hf_ref, hb_ref, og_ref, ng_ref, w_ref):
    hs = hf_ref[...] + hb_ref[...]
    parts = []
    for hh in range(ML_HEADS):
        blk = hs[:, hh * ML_V_DIM:(hh + 1) * ML_V_DIM]
        ms = jnp.mean(blk * blk, axis=-1, keepdims=True)
        parts.append(blk * lax.rsqrt(ms + EPS))
    hn = jnp.concatenate(parts, axis=1) * ng_ref[...]
    gated = (hn * jax.nn.sigmoid(og_ref[...])).astype(BF16)
    return jnp.dot(gated, w_ref[...], preferred_element_type=F32)


_MIXER_OUT = {"even": (_even_mixer_out, 3), "odd": (_odd_mixer_out, 5)}


def _ffn_kernel(h_ref, mod_ref, g_ref, w13_ref, w2_ref, *rest, mod_base, final, n_lat_t, split_input, mixer):
    o_ref = rest[-1]
    x = h_ref[...]
    m = mod_ref[0]
    if split_input:
        x = jnp.where(pl.program_id(0) < n_lat_t, x, rest[0][...])
        rest = rest[1:]
    if mixer is not None:
        fn, n_ops = _MIXER_OUT[mixer]
        x = x + m[5:6] * fn(*rest[:n_ops])
        rest = rest[n_ops:]
    z = _rms_mod(x, g_ref[...], m[mod_base:mod_base + 1], m[mod_base + 1:mod_base + 2]).astype(BF16)
    f = w2_ref.shape[0]
    gu = jnp.dot(z, w13_ref[...], preferred_element_type=F32)
    a = (_silu(gu[:, :f]) * gu[:, f:]).astype(BF16)
    y = jnp.dot(a, w2_ref[...], preferred_element_type=F32)
    out = x + (0.5 * m[mod_base + 2:mod_base + 3]) * y
    if final:
        ms = jnp.mean(out * out, axis=-1, keepdims=True)
        out = out * lax.rsqrt(ms + EPS) * rest[0][...]
    o_ref[...] = out


def _ffn(h, mod, g, w13_all, w2_all, layer, which, n_tiles, n_lat_t, mod_base, final_g=None, ctx_rows=None,
         mixer=None, mixer_args=()):
    d = h.shape[1]
    f = w2_all.shape[2]
    final = final_g is not None
    split_input = ctx_rows is not None
    last_lat = n_lat_t - 1
    row_tile = lambda w: pl.BlockSpec((TM, w), lambda i: (i, 0))
    in_specs = [
        pl.BlockSpec((TM, d), (lambda i: (jnp.minimum(i, last_lat), 0)) if split_input else (lambda i: (i, 0))),
        pl.BlockSpec((1, N_MOD, d), lambda i: (i // n_lat_t, 0, 0)),
        _full((1, d)),
        pl.BlockSpec((None, None, d, 2 * f), lambda i: (layer, which, 0, 0)),
        pl.BlockSpec((None, None, f, d), lambda i: (layer, which, 0, 0)),
    ]
    args = [h, mod, g.reshape(1, d), w13_all, w2_all]
    if split_input:
        in_specs.append(pl.BlockSpec((TM, d), lambda i: (jnp.maximum(i - n_lat_t, 0), 0)))
        args.append(ctx_rows)
    for arr in mixer_args:
        in_specs.append(row_tile(arr.shape[1]) if arr.shape[0] == h.shape[0] else _full(arr.shape))
        args.append(arr)
    if final:
        in_specs.append(_full((1, d)))
        args.append(final_g.reshape(1, d))
    return pl.pallas_call(
        functools.partial(_ffn_kernel, mod_base=mod_base, final=final, n_lat_t=n_lat_t, split_input=split_input,
                          mixer=mixer),
        grid=(n_tiles,),
        in_specs=in_specs,
        out_specs=row_tile(d),
        out_shape=jax.ShapeDtypeStruct((n_tiles * TM, d), F32),
        compiler_params=_cparams("arbitrary"),
        name="ffn",
    )(*args)


def _even_proj_kernel(h_ref, mod_ref, g_ref, win_ref, gng_ref, qg_ref, kg_ref, spw_ref, spb_ref,
                      cos_r_ref, sin_r_ref, cos_c_ref, sin_c_ref, b256_ref, b128_ref,
                      gated_ref, q_ref, kd0_ref, kd1_ref, vt_ref, *, n_lat_t):
    x = h_ref[...]
    m = mod_ref[0]
    z = _rms_mod(x, g_ref[...], m[3:4], m[4:5]).astype(BF16)
    p = jnp.dot(z, win_ref[...], preferred_element_type=F32)
    aw = A_GROUPS * A_GROUP_DIM
    qw = ATT_HEADS * ATT_HEAD_DIM
    u = jax.nn.gelu(p[:, :aw])
    v = jax.nn.gelu(p[:, aw:2 * aw])
    b256 = b256_ref[...]
    vn = (v * lax.rsqrt(_group_mean_sq(v, b256) + EPS) * gng_ref[...]).astype(BF16)

    lane = lax.broadcasted_iota(jnp.int32, (CHUNK, LANES), 1)
    lo = lane < A_GROUP_DIM
    n_chunks = TM // CHUNK
    for j in range(aw // LANES):
        vt = jnp.concatenate([vn[c * CHUNK:(c + 1) * CHUNK, j * LANES:(j + 1) * LANES]
                              for c in range(n_chunks)], axis=1)
        s0 = jnp.dot(spw_ref[2 * j], vt, preferred_element_type=F32)
        s1 = jnp.dot(spw_ref[2 * j + 1], vt, preferred_element_type=F32)
        for c in range(n_chunks):
            s = (jnp.where(lo, s0[:, c * LANES:(c + 1) * LANES], s1[:, c * LANES:(c + 1) * LANES])
                 + spb_ref[:, j * LANES:(j + 1) * LANES])
            gated_ref[c * CHUNK:(c + 1) * CHUNK, j * LANES:(j + 1) * LANES] = (
                u[c * CHUNK:(c + 1) * CHUNK, j * LANES:(j + 1) * LANES] * s).astype(BF16)

    o = 2 * aw
    pq = p[:, o:o + qw]
    pk = p[:, o + qw:o + qw + LANES]
    va = p[:, o + qw + LANES:o + qw + 2 * LANES]
    qn = pq * lax.rsqrt(_group_mean_sq(pq, b256) + EPS) * qg_ref[...]
    kn = pk * lax.rsqrt(_group_mean_sq(pk, b128_ref[...]) + EPS) * kg_ref[...]

    is_lat = pl.program_id(0) < n_lat_t
    lane_g = lax.broadcasted_iota(jnp.int32, (GRID_W, LANES), 1)
    row_lanes = (lane_g % ATT_HEAD_DIM) < ATT_HEAD_DIM // 2

    def tile_table(r_ref, c_ref, ctx_value):
        col_part = jnp.where(is_lat, c_ref[...], ctx_value)
        return jnp.concatenate([jnp.where(row_lanes, r_ref[gr:gr + 1, :], col_part)
                                for gr in range(TM // GRID_W)], axis=0)

    cos = tile_table(cos_r_ref, cos_c_ref, 1.0)
    sin = tile_table(sin_r_ref, sin_c_ref, 0.0)
    lane_t = lax.broadcasted_iota(jnp.int32, (TM, LANES), 1)
    pair_dim = ATT_HEAD_DIM // 2
    half = pair_dim // 2
    first_half = (lane_t % pair_dim) < half
    lo_t = lane_t < ATT_HEAD_DIM

    def rope(t):
        partner = jnp.where(first_half, pltpu.roll(t, LANES - half, 1), pltpu.roll(t, half, 1))
        return t * cos + partner * sin

    scale = ATT_HEAD_DIM ** -0.5 * LOG2_E
    for j in range(qw // LANES):
        q_ref[:, j * LANES:(j + 1) * LANES] = (rope(qn[:, j * LANES:(j + 1) * LANES]) * scale).astype(BF16)
    kr = rope(kn)
    kswap = pltpu.roll(kr, ATT_HEAD_DIM, 1)
    kd0_ref[...] = jnp.where(lo_t, kr, kswap).astype(BF16)
    kd1_ref[...] = jnp.where(lo_t, kswap, kr).astype(BF16)
    va_t = va.T
    sub = lax.broadcasted_iota(jnp.int32, (V_ROWS - ATT_HEAD_DIM, TM), 0)
    ones_rows = jnp.where(sub == 0, 1.0, 0.0).astype(BF16)
    for g in range(ATT_KV_HEADS):
        vt_ref[g, 0, :ATT_HEAD_DIM, :] = va_t[g * ATT_HEAD_DIM:(g + 1) * ATT_HEAD_DIM].astype(BF16)
        vt_ref[g, 0, ATT_HEAD_DIM:, :] = ones_rows


def _group_matrix(width, gs):
    r = jnp.arange(width) // gs
    return jnp.where(r[:, None] == r[None, :], 1.0 / gs, 0.0).astype(BF16)


def _rope_tables(t_lat, rows):
    axis_dim = ATT_HEAD_DIM // 2
    n_freq = axis_dim // 2
    inv_freq = ROPE_THETA ** (-jnp.arange(0, axis_dim, 2, dtype=F32) / axis_dim)
    per_tile = TM // GRID_W
    n_lat_t = t_lat // TM
    n_ctx_t = (rows - t_lat) // TM
    lane = jnp.arange(LANES)
    sign = jnp.where((lane % axis_dim) < n_freq, -1.0, 1.0)
    ang_r = jnp.arange(t_lat // GRID_W, dtype=F32)[:, None] * inv_freq[lane % n_freq][None, :]
    ang_r = jnp.pad(ang_r.reshape(n_lat_t, per_tile, LANES), ((0, n_ctx_t), (0, SUBLANES - per_tile), (0, 0)))
    ang_r = ang_r.reshape((n_lat_t + n_ctx_t) * SUBLANES, LANES)
    ang_c = jnp.arange(GRID_W, dtype=F32)[:, None] * inv_freq[lane % n_freq][None, :]
    return jnp.cos(ang_r), jnp.sin(ang_r) * sign, jnp.cos(ang_c), jnp.sin(ang_c) * sign


def _even_proj(h, mod, g, w_in, gate_norm_g, sp_w, sp_b, q_norm_g, k_norm_g, t_lat, n_lat_t):
    rows, d = h.shape
    aw = A_GROUPS * A_GROUP_DIM
    qw = ATT_HEADS * ATT_HEAD_DIM
    cos_r, sin_r, cos_c, sin_c = _rope_tables(t_lat, rows)
    spb = jnp.repeat(sp_b.T, A_GROUP_DIM, axis=1)
    row_tile = lambda w: pl.BlockSpec((TM, w), lambda i: (i, 0))
    out_shape = [jax.ShapeDtypeStruct((rows, aw), BF16), jax.ShapeDtypeStruct((rows, qw), BF16)]
    out_shape += [jax.ShapeDtypeStruct((rows, LANES), BF16)] * 2
    out_shape += [jax.ShapeDtypeStruct((ATT_KV_HEADS, rows // TM, V_ROWS, TM), BF16)]
    vt_spec = pl.BlockSpec((ATT_KV_HEADS, 1, V_ROWS, TM), lambda i: (0, i, 0, 0))
    return pl.pallas_call(
        functools.partial(_even_proj_kernel, n_lat_t=n_lat_t),
        grid=(rows // TM,),
        in_specs=[
            row_tile(d),
            pl.BlockSpec((1, N_MOD, d), lambda i: (i // n_lat_t, 0, 0)),
            _full((1, d)),
            _full(w_in.shape),
            _full((1, aw)),
            _full((1, qw)),
            _full((1, LANES)),
            _full(sp_w.shape),
            _full((CHUNK, aw)),
            pl.BlockSpec((SUBLANES, LANES), lambda i: (i, 0)),
            pl.BlockSpec((SUBLANES, LANES), lambda i: (i, 0)),
            _full((GRID_W, LANES)),
            _full((GRID_W, LANES)),
            _full((MXU_TILE, MXU_TILE)),
            _full((LANES, LANES)),
        ],
        out_specs=[row_tile(aw), row_tile(qw), row_tile(LANES), row_tile(LANES), vt_spec],
        out_shape=out_shape,
        compiler_params=_cparams("arbitrary"),
        name="even_proj",
    )(h, mod, g.reshape(1, d), w_in.astype(BF16), gate_norm_g.reshape(1, aw),
      jnp.tile(q_norm_g, ATT_HEADS).reshape(1, qw), jnp.tile(k_norm_g, ATT_KV_HEADS).reshape(1, LANES),
      sp_w.astype(BF16), spb, cos_r, sin_r, cos_c, sin_c,
      _group_matrix(MXU_TILE, A_GROUP_DIM), _group_matrix(LANES, ATT_HEAD_DIM))


def _attn_kernel(bounded_ref, q_ref, kd0_ref, kd1_ref, vt_ref, o_ref, qs_ref, m_ref, acc_ref,
                 *, n_lat_t, t_lat, ctx_len):
    i = pl.program_id(0)
    tq = q_ref.shape[0]
    lane = lax.broadcasted_iota(jnp.int32, (tq, LANES), 1)
    lo = lane < ATT_HEAD_DIM
    for h in range(ATT_HEADS):
        t = q_ref[:, (h // 2) * LANES:(h // 2 + 1) * LANES]
        qs_ref[h * tq:(h + 1) * tq] = jnp.where(lo if h % 2 == 0 else ~lo, t, jnp.zeros_like(t))

    kd_refs = (kd0_ref, kd1_ref)
    n_streams = ATT_HEADS // 2
    blocks_per_chunk = ATT_TK // TM
    nt_dims = (((1,), (1,)), ((), ()))

    def group_of(s):
        return s // (n_streams // ATT_KV_HEADS)

    def scores(s, k_start, n_blk):
        kc = kd_refs[group_of(s)][pl.ds(k_start, n_blk * TM), :]
        qs = qs_ref[2 * s * tq:(2 * s + 2) * tq]
        return lax.dot_general(kc, qs, nt_dims, preferred_element_type=F32)

    def p_times_v(s, pt, n_blk, blk0):
        pv = None
        for j in range(n_blk):
            part = jnp.dot(vt_ref[group_of(s), blk0 + j], pt[j * TM:(j + 1) * TM], preferred_element_type=F32)
            pv = part if pv is None else pv + part
        return pv

    def softmax_pv_online(s, st, n_blk, blk0):
        m_old = m_ref[s]
        m_new = jnp.maximum(m_old, jnp.max(st, axis=0, keepdims=True))
        alpha = jnp.exp2(m_old - m_new)
        pt = jnp.exp2(st - m_new).astype(BF16)
        m_ref[s] = m_new
        acc_ref[s] = alpha * acc_ref[s] + p_times_v(s, pt, n_blk, blk0)

    def softmax_pv_bounded(s, st, n_blk, blk0):
        pt = jnp.exp2(st).astype(BF16)
        acc_ref[s] += p_times_v(s, pt, n_blk, blk0)

    def sweep(softmax_pv, lookahead):
        def step(k_start, n_blk, blk0):
            pending = [scores(s, k_start, n_blk) for s in range(min(lookahead, n_streams))]
            for s in range(n_streams):
                st = pending.pop(0)
                if s + lookahead < n_streams:
                    pending.append(scores(s + lookahead, k_start, n_blk))
                softmax_pv(s, st, n_blk, blk0)

        def body(c, carry):
            step(pl.multiple_of(c * ATT_TK, ATT_TK), blocks_per_chunk, c * blocks_per_chunk)
            return carry

        n_full = jnp.where(i < n_lat_t, t_lat // ATT_TK, 0)
        lax.fori_loop(0, n_full, body, 0)
        step(t_lat, ctx_len // TM, t_lat // TM)

    acc_ref[...] = jnp.zeros(acc_ref.shape, F32)
    bounded = bounded_ref[0] == 1

    @pl.when(bounded)
    def _():
        sweep(softmax_pv_bounded, ATT_LOOKAHEAD_BOUNDED)

    @pl.when(jnp.logical_not(bounded))
    def _():
        m_ref[...] = jnp.full(m_ref.shape, -jnp.inf, F32)
        sweep(softmax_pv_online, ATT_LOOKAHEAD)

    for s in range(n_streams):
        acc = acc_ref[s]
        o_t = acc[:ATT_HEAD_DIM] * (1.0 / acc[ATT_HEAD_DIM:ATT_HEAD_DIM + 1])
        pair = jnp.concatenate([o_t[:, :tq], o_t[:, tq:]], axis=0)
        o_ref[:, s * LANES:(s + 1) * LANES] = pair.T.astype(BF16)


def _attention(q, kd0, kd1, vt, q_norm_g, k_norm_g, t_lat, n_lat_t):
    rows, qw = q.shape
    ctx_len = rows - t_lat
    score_bound = (ATT_HEAD_DIM ** 0.5 * LOG2_E) * jnp.max(jnp.abs(q_norm_g)) * jnp.max(jnp.abs(k_norm_g))
    bounded = (score_bound <= ATT_SCORE_BOUND).astype(jnp.int32).reshape(1)
    return pl.pallas_call(
        functools.partial(_attn_kernel, n_lat_t=n_lat_t, t_lat=t_lat, ctx_len=ctx_len),
        grid=(rows // TM,),
        in_specs=[pl.BlockSpec(memory_space=pltpu.SMEM),
                  pl.BlockSpec((TM, qw), lambda i: (i, 0)), _full((rows, LANES)), _full((rows, LANES)),
                  _full(vt.shape)],
        out_specs=pl.BlockSpec((TM, qw), lambda i: (i, 0)),
        out_shape=jax.ShapeDtypeStruct((rows, qw), BF16),
        scratch_shapes=[
            pltpu.VMEM((ATT_HEADS * TM, LANES), BF16),
            pltpu.VMEM((ATT_HEADS // 2, 1, 2 * TM), F32),
            pltpu.VMEM((ATT_HEADS // 2, V_ROWS, 2 * TM), F32),
        ],
        compiler_params=_cparams("arbitrary"),
        name="gqa_attention",
    )(bounded, q, kd0, kd1, vt)


def _odd_proj_kernel(h_ref, mod_ref, g_ref, win_ref, gb_ref, pqk_ref, vt_ref, o_ref, gates_ref):
    x = h_ref[...]
    m = mod_ref[0]
    z = _rms_mod(x, g_ref[...], m[3:4], m[4:5]).astype(BF16)
    p = jnp.dot(z, win_ref[...], preferred_element_type=F32)
    qk = 2 * ML_HEADS * ML_QK_DIM
    vw = ML_HEADS * ML_V_DIM
    pqk_ref[...] = p[:, :qk]
    for c in range(TM // CHUNK):
        vt_ref[c] = p[c * CHUNK:(c + 1) * CHUNK, qk:qk + vw].T.astype(BF16)
    o_ref[...] = p[:, qk + vw:qk + 2 * vw]
    gts = p[:, qk + 2 * vw:] + gb_ref[...]
    lane = lax.broadcasted_iota(jnp.int32, gts.shape, 1)
    is_f = (lane < 4 * ML_HEADS) & ((lane // ML_HEADS) % 2 == 1)
    log_sig = jnp.minimum(gts, 0.0) - jnp.log1p(jnp.exp(-jnp.abs(gts)))
    gates_ref[...] = jnp.where(is_f, log_sig, gts)


def _odd_proj(h, mod, g, w_in, gate_b, n_lat_t):
    rows, d = h.shape
    qk = 2 * ML_HEADS * ML_QK_DIM
    vw = ML_HEADS * ML_V_DIM
    n_gate = 4 * ML_HEADS
    w_pad = jnp.pad(w_in, ((0, 0), (0, LANES - n_gate))).astype(BF16)
    gb = jnp.pad(gate_b, (0, LANES - n_gate)).reshape(1, LANES)
    row_tile = lambda w: pl.BlockSpec((TM, w), lambda i: (i, 0))
    return pl.pallas_call(
        _odd_proj_kernel,
        grid=(rows // TM,),
        in_specs=[row_tile(d), pl.BlockSpec((1, N_MOD, d), lambda i: (i // n_lat_t, 0, 0)),
                  _full((1, d)), _full(w_pad.shape), _full((1, LANES))],
        out_specs=[row_tile(qk), pl.BlockSpec((TM // CHUNK, vw, CHUNK), lambda i: (i, 0, 0)),
                   row_tile(vw), row_tile(LANES)],
        out_shape=[jax.ShapeDtypeStruct((rows, qk), F32), jax.ShapeDtypeStruct((rows // CHUNK, vw, CHUNK), BF16),
                   jax.ShapeDtypeStruct((rows, vw), F32), jax.ShapeDtypeStruct((rows, LANES), F32)],
        compiler_params=_cparams("arbitrary"),
        name="odd_proj",
    )(h, mod, g.reshape(1, d), w_pad, gb)


def _conv_kernel(x_ref, prev_ref, next_ref, w_ref, q_ref, k_ref, *, t_lat, rows):
    x = x_ref[...]
    tm = x.shape[0]
    row = lax.broadcasted_iota(jnp.int32, x.shape, 0)
    token = row + pl.program_id(0) * tm
    xm = jnp.where(row == 0, prev_ref[SUBLANES - 1:SUBLANES, :], pltpu.roll(x, 1, 0))
    xp = jnp.where(row == tm - 1, next_ref[0:1, :], pltpu.roll(x, tm - 1, 0))
    xm = jnp.where((token == 0) | (token == t_lat), 0.0, xm)
    xp = jnp.where((token == t_lat - 1) | (token == rows - 1), 0.0, xp)
    y = xm * w_ref[0:1, :] + x * w_ref[1:2, :] + xp * w_ref[2:3, :]
    y = _silu(y)
    qk = ML_HEADS * ML_QK_DIM
    q_ref[...] = (y[:, :qk] * (ML_QK_DIM ** -0.5)).astype(BF16)
    k_ref[...] = y[:, qk:].astype(BF16)


def _conv(pqk, conv_w, t_lat):
    rows, w = pqk.shape
    n_tiles = rows // TM
    tm = TM * max(t for t in range(1, CONV_MAX_TILES + 1) if n_tiles % t == 0)
    per = tm // SUBLANES
    last_blk = rows // SUBLANES - 1
    return pl.pallas_call(
        functools.partial(_conv_kernel, t_lat=t_lat, rows=rows),
        grid=(rows // tm,),
        in_specs=[
            pl.BlockSpec((tm, w), lambda i: (i, 0)),
            pl.BlockSpec((SUBLANES, w), lambda i: (jnp.maximum(i * per - 1, 0), 0)),
            pl.BlockSpec((SUBLANES, w), lambda i: (jnp.minimum((i + 1) * per, last_blk), 0)),
            _full(conv_w.shape),
        ],
        out_specs=[pl.BlockSpec((tm, w // 2), lambda i: (i, 0))] * 2,
        out_shape=[jax.ShapeDtypeStruct((rows, w // 2), BF16)] * 2,
        compiler_params=_cparams("arbitrary"),
        name="short_conv",
    )(pqk, pqk, pqk, conv_w)


def _mlstm_prepare(d, g):
    ln = CHUNK
    r = lax.broadcasted_iota(jnp.int32, (ln, ln), 0)
    c = lax.broadcasted_iota(jnp.int32, (ln, ln), 1)
    tri = (c <= r) if d == 0 else (c >= r)
    tri_b = jnp.where(tri, 1.0, 0.0).astype(BF16)
    g_t = g.T
    nt_dims = (((1,), (1,)), ((), ()))
    bc_col = sum(jnp.dot(tri_b, piece, preferred_element_type=F32) for piece in _split3(g))
    bc_row = sum(lax.dot_general(piece, tri_b, nt_dims, preferred_element_type=F32) for piece in _split3(g_t))
    il0 = 2 * ML_HEADS * d
    fl0 = il0 + ML_HEADS
    u = g_t[il0:il0 + ML_HEADS, :] - bc_row[fl0:fl0 + ML_HEADS, :]
    lane = lax.broadcasted_iota(jnp.int32, u.shape, 1)
    sh = 1
    while sh < ln:
        if d == 0:
            shifted = jnp.where(lane >= sh, pltpu.roll(u, sh, 1), -jnp.inf)
        else:
            shifted = jnp.where(lane < ln - sh, pltpu.roll(u, ln - sh, 1), -jnp.inf)
        u = jnp.maximum(u, shifted)
        sh *= 2
    vis = (c >= r) if d == 0 else (c <= r)
    return dict(vis=vis, g=g, g_t=g_t, bc_col=bc_col, bc_row=bc_row, run_max=u,
                lane_lo=c < ML_QK_DIM, last=ln - 1 if d == 0 else 0)


def _mlstm_head(d, h, pre, q_pair, k_pair, vt_ref, o_ref, c_ref, m_ref):
    half = h % 2
    nt_dims = (((1,), (1,)), ((), ()))
    il = 2 * ML_HEADS * d + h
    fl = il + ML_HEADS
    keep_lane = pre["lane_lo"] if half == 0 else ~pre["lane_lo"]
    brow = pre["bc_row"][fl:fl + 1, :]
    irow = pre["g_t"][il:il + 1, :]
    src_col = pre["g"][:, il:il + 1] - pre["bc_col"][:, fl:fl + 1]
    btot = pre["bc_col"][pre["last"]:pre["last"] + 1, fl:fl + 1]
    m_prev = m_ref[h:h + 1, 0:1]

    inter = brow + m_prev
    m_row = jnp.maximum(inter, brow + pre["run_max"][h:h + 1, :])
    wts_t = jnp.exp(jnp.where(pre["vis"], (brow - m_row) + src_col, -jnp.inf))
    a = jnp.exp(inter - m_row)

    qm = jnp.where(keep_lane, q_pair, jnp.zeros_like(q_pair))
    s_t = lax.dot_general(k_pair, qm, nt_dims, preferred_element_type=F32) * wts_t
    vt = vt_ref[h * ML_V_DIM:(h + 1) * ML_V_DIM, :]
    vt_aug = jnp.concatenate([vt, jnp.ones((ML_AUX_ROWS, vt.shape[1]), BF16)], axis=0)
    c_aug = c_ref[h]
    num_den = (jnp.dot(vt_aug, s_t.astype(BF16), preferred_element_type=F32)
               + a * lax.dot_general(c_aug.astype(BF16), qm, nt_dims, preferred_element_type=F32))
    den = num_den[ML_V_DIM:ML_V_DIM + 1]
    h_t = num_den[:ML_V_DIM] / jnp.maximum(jnp.abs(den), jnp.exp(-m_row))
    o_ref[:, h * LANES:(h + 1) * LANES] = h_t.T

    w_row = btot - brow + irow
    m_loc = jnp.max(w_row, axis=1, keepdims=True)
    e = jnp.exp(w_row - m_loc)
    ev_t = (vt_aug.astype(F32) * e).astype(BF16)
    km = jnp.where(keep_lane, k_pair, jnp.zeros_like(k_pair))
    c_loc = jnp.dot(ev_t, km, preferred_element_type=F32)
    m_new = jnp.maximum(btot + m_prev, m_loc)
    decay = jnp.exp(btot + m_prev - m_new)
    gain = jnp.exp(m_loc - m_new)
    c_ref[h] = decay * c_aug + gain * c_loc
    m_ref[h:h + 1, :] = jnp.broadcast_to(m_new, (1, LANES))


def _mlstm_kernel(qf_ref, kf_ref, vf_ref, gf_ref, qb_ref, kb_ref, vb_ref, gb_ref, of_ref, ob_ref,
                  ct_ref, m_ref):
    @pl.when(pl.program_id(0) == 0)
    def _():
        ct_ref[...] = jnp.zeros(ct_ref.shape, F32)
        m_ref[...] = jnp.zeros(m_ref.shape, F32)

    dirs = ((qf_ref, kf_ref, vf_ref, gf_ref, of_ref), (qb_ref, kb_ref, vb_ref, gb_ref, ob_ref))

    def chunk_of(d, step):
        return step if d == 0 else ML_CHUNKS_PER_STEP - 1 - step

    def chunk_rows(d, step):
        return pl.ds(chunk_of(d, step) * CHUNK, CHUNK)

    pre = [[_mlstm_prepare(d, dirs[d][3][chunk_rows(d, step), :]) for d in range(2)]
           for step in range(ML_CHUNKS_PER_STEP)]
    for step in range(ML_CHUNKS_PER_STEP):
        for pair in range(ML_HEADS // 2):
            loaded = []
            for d in range(2):
                rows = chunk_rows(d, step)
                loaded.append((dirs[d][0][rows, pair * LANES:(pair + 1) * LANES],
                               dirs[d][1][rows, pair * LANES:(pair + 1) * LANES]))
            for half in range(2):
                for d in range(2):
                    q_pair, k_pair = loaded[d]
                    _mlstm_head(d, 2 * pair + half, pre[step][d], q_pair, k_pair,
                                dirs[d][2].at[chunk_of(d, step)], dirs[d][4].at[chunk_rows(d, step)],
                                ct_ref.at[d], m_ref.at[d])


def _mlstm(q, k, vt, gates, n_lat_ch):
    rows = q.shape[0]
    step_rows = ML_CHUNKS_PER_STEP * CHUNK
    assert rows % step_rows == 0 and (n_lat_ch * CHUNK) % step_rows == 0
    nblk = rows // step_rows
    n_lat_blk = n_lat_ch * CHUNK // step_rows
    fwd = lambda i: (i + n_lat_blk) % nblk
    bwd = lambda i: nblk - 1 - i
    vw = vt.shape[1]

    def blocks(order):
        blk = lambda w: pl.BlockSpec((step_rows, w), lambda i: (order(i), 0))
        vt_blk = pl.BlockSpec((ML_CHUNKS_PER_STEP, vw, CHUNK), lambda i: (order(i), 0, 0))
        return [blk(q.shape[1]), blk(k.shape[1]), vt_blk, blk(LANES)]

    out_blk = lambda order: pl.BlockSpec((step_rows, vw), lambda i: (order(i), 0))
    return pl.pallas_call(
        _mlstm_kernel,
        grid=(nblk,),
        in_specs=blocks(fwd) + blocks(bwd),
        out_specs=[out_blk(fwd), out_blk(bwd)],
        out_shape=[jax.ShapeDtypeStruct((rows, vw), F32)] * 2,
        scratch_shapes=[
            pltpu.VMEM((2, ML_HEADS, ML_V_DIM + ML_AUX_ROWS, LANES), F32),
            pltpu.VMEM((2, ML_HEADS, LANES), F32),
        ],
        compiler_params=_cparams("arbitrary"),
        name="mlstm",
    )(q, k, vt, gates, q, k, vt, gates)


def kernel(x, c, ctx, c_ctx, mod_w, mod_b, norm_g, ffn_w13, ffn_w2, ab_w_in, ab_gate_norm_g, ab_spatial_w, ab_spatial_b, ab_q_norm_g, ab_k_norm_g, ab_w_out, ml_w_in, ml_conv_w, ml_gate_b, ml_out_norm_g, ml_w_out, final_norm_g):
    b, t_lat, d = x.shape
    ctx_len = ctx.shape[1]
    depth = mod_w.shape[0]
    assert b == 1 and t_lat % TM == 0 and ctx_len % TM == 0 and t_lat % ATT_TK == 0
    n_lat_t = t_lat // TM
    n_all_t = (t_lat + ctx_len) // TM

    mods = _mods(c, c_ctx, mod_w, mod_b)
    w13_all = ffn_w13.astype(BF16)
    w2_all = ffn_w2.astype(BF16)
    h = x[0]
    for l in range(depth):
        keep_ctx = l < depth - 1
        mod = mods[l]
        h = _ffn(h, mod, norm_g[l, 0], w13_all, w2_all, l, 0, n_all_t, n_lat_t, 0,
                 ctx_rows=ctx[0] if l == 0 else None)
        n_out_t = n_all_t if keep_ctx else n_lat_t
        if l % 2 == 0:
            e = l // 2
            gated, q, kd0, kd1, vt = _even_proj(
                h, mod, norm_g[l, 1], ab_w_in[e], ab_gate_norm_g[e], ab_spatial_w[e], ab_spatial_b[e],
                ab_q_norm_g[e], ab_k_norm_g[e], t_lat, n_lat_t)
            attn = _attention(q, kd0, kd1, vt, ab_q_norm_g[e], ab_k_norm_g[e], t_lat, n_lat_t)
            mixer, mixer_args = "even", (gated, attn, ab_w_out[e].astype(BF16))
        else:
            o = l // 2
            pqk, vt, og, gates = _odd_proj(h, mod, norm_g[l, 1], ml_w_in[o], ml_gate_b[o], n_lat_t)
            q, k = _conv(pqk, ml_conv_w[o], t_lat)
            hf, hb = _mlstm(q, k, vt, gates, t_lat // CHUNK)
            mixer, mixer_args = "odd", (hf, hb, og, ml_out_norm_g[o].reshape(1, -1), ml_w_out[o].astype(BF16))
        final_g = final_norm_g if l == depth - 1 else None
        h = _ffn(h, mod, norm_g[l, 2], w13_all, w2_all, l, 1, n_out_t, n_lat_t, 6, final_g,
                 mixer=mixer, mixer_args=mixer_args)
    return h[:t_lat][None]
```

```python
import functools

import jax
import jax.numpy as jnp
from jax import lax
from jax.experimental import pallas as pl
from jax.experimental.pallas import tpu as pltpu

F32 = jnp.float32
BF16 = jnp.bfloat16

LANES = 128
SUBLANES = 8
MXU_TILE = 256
VMEM_LIMIT = 56 * 1024 * 1024

GRID_W = 64
CHUNK = 128
EPS = 1e-6
ROPE_THETA = 10000.0
LOG2_E = 1.4426950408889634
N_MOD = 9
A_GROUPS = 8
A_GROUP_DIM = 64
ATT_HEADS = 8
ATT_KV_HEADS = 2
ATT_HEAD_DIM = 64
ML_HEADS = 8
ML_QK_DIM = 64
ML_V_DIM = 128
CONV_WIDTH = 3

TM = 256
CONV_MAX_TILES = 5
ATT_TK = 2048
ATT_LOOKAHEAD = 3
ATT_LOOKAHEAD_BOUNDED = 2
ML_AUX_ROWS = 16
ML_CHUNKS_PER_STEP = 2
ATT_SCORE_BOUND = 60.0
V_ROWS = ATT_HEAD_DIM + 16


def _cparams(*sem):
    return pltpu.CompilerParams(dimension_semantics=sem, vmem_limit_bytes=VMEM_LIMIT)


def _full(shape):
    n = len(shape)
    return pl.BlockSpec(shape, lambda *_: (0,) * n)


def _rms_mod(x, g, shift, scale):
    ms = jnp.mean(x * x, axis=-1, keepdims=True)
    y = x * lax.rsqrt(ms + EPS) * g
    return y * (1 + scale) + shift


def _silu(x):
    return x * jax.nn.sigmoid(x)


def _split3(x):
    x1 = x.astype(BF16)
    r = x - x1.astype(F32)
    x2 = r.astype(BF16)
    x3 = (r - x2.astype(F32)).astype(BF16)
    return x1, x2, x3


def _group_mean_sq(x, bmat):
    w = bmat.shape[0]
    sq = x * x
    hi = sq.astype(BF16)
    lo = (sq - hi.astype(F32)).astype(BF16)
    slabs = [jnp.dot(hi[:, j:j + w], bmat, preferred_element_type=F32)
             + jnp.dot(lo[:, j:j + w], bmat, preferred_element_type=F32)
             for j in range(0, x.shape[1], w)]
    return slabs[0] if len(slabs) == 1 else jnp.concatenate(slabs, axis=1)


def _mod_kernel(c_ref, w_ref, b_ref, o_ref):
    w = w_ref[0]
    o_ref[0] = jnp.zeros(o_ref.shape[1:], F32)
    for r in range(c_ref.shape[0]):
        s = _silu(c_ref[r])
        cols = [jnp.sum(w[:, j * LANES:(j + 1) * LANES] * s, axis=0, keepdims=True)
                for j in range(w.shape[1] // LANES)]
        o_ref[0, r:r + 1, :] = jnp.concatenate(cols, axis=1) + b_ref[0]


def _mods(c, c_ctx, mod_w, mod_b):
    depth, d, nd = mod_w.shape
    crep = jnp.broadcast_to(jnp.stack([c[0], c_ctx])[:, :, None], (2, d, LANES))
    out = pl.pallas_call(
        _mod_kernel,
        grid=(depth, nd // d),
        in_specs=[
            pl.BlockSpec((2, d, LANES), lambda l, j: (0, 0, 0)),
            pl.BlockSpec((1, d, d), lambda l, j: (l, 0, j)),
            pl.BlockSpec((1, 1, d), lambda l, j: (l, 0, j)),
        ],
        out_specs=pl.BlockSpec((1, SUBLANES, d), lambda l, j: (l, 0, j)),
        out_shape=jax.ShapeDtypeStruct((depth, SUBLANES, nd), F32),
        compiler_params=_cparams("arbitrary", "arbitrary"),
        name="modulation",
    )(crep, mod_w, mod_b.reshape(depth, 1, nd))
    return out[:, :2].reshape(depth, 2, N_MOD, d)


def _even_mixer_out(a_ref, b_ref, w_ref):
    aw = a_ref.shape[1]
    return (jnp.dot(a_ref[...], w_ref[:aw], preferred_element_type=F32)
            + jnp.dot(b_ref[...], w_ref[aw:], preferred_element_type=F32))


def _odd_mixer_out(hf_ref, hb_ref, og_ref, ng_ref, w_ref):
    hs = hf_ref[...] + hb_ref[...]
    parts = []
    for hh in range(ML_HEADS):
        blk = hs[:, hh * ML_V_DIM:(hh + 1) * ML_V_DIM]
        ms = jnp.mean(blk * blk, axis=-1, keepdims=True)
        parts.append(blk * lax.rsqrt(ms + EPS))
    hn = jnp.concatenate(parts, axis=1) * ng_ref[...]
    gated = (hn * jax.nn.sigmoid(og_ref[...])).astype(BF16)
    return jnp.dot(gated, w_ref[...], preferred_element_type=F32)


_MIXER_OUT = {"even": (_even_mixer_out, 3), "odd": (_odd_mixer_out, 5)}


def _ffn_kernel(h_ref, mod_ref, g_ref, w13_ref, w2_ref, *rest, mod_base, final, n_lat_t, split_input, mixer):
    o_ref = rest[-1]
    x = h_ref[...]
    m = mod_ref[0]
    if split_input:
        x = jnp.where(pl.program_id(0) < n_lat_t, x, rest[0][...])
        rest = rest[1:]
    if mixer is not None:
        fn, n_ops = _MIXER_OUT[mixer]
        x = x + m[5:6] * fn(*rest[:n_ops])
        rest = rest[n_ops:]
    z = _rms_mod(x, g_ref[...], m[mod_base:mod_base + 1], m[mod_base + 1:mod_base + 2]).astype(BF16)
    f = w2_ref.shape[0]
    gu = jnp.dot(z, w13_ref[...], preferred_element_type=F32)
    a = (_silu(gu[:, :f]) * gu[:, f:]).astype(BF16)
    y = jnp.dot(a, w2_ref[...], preferred_element_type=F32)
    out = x + (0.5 * m[mod_base + 2:mod_base + 3]) * y
    if final:
        ms = jnp.mean(out * out, axis=-1, keepdims=True)
        out = out * lax.rsqrt(ms + EPS) * rest[0][...]
    o_ref[...] = out


def _ffn(h, mod, g, w13_all, w2_all, layer, which, n_tiles, n_lat_t, mod_base, final_g=None, ctx_rows=None,
         mixer=None, mixer_args=()):
    d = h.shape[1]
    f = w2_all.shape[2]
    final = final_g is not None
    split_input = ctx_rows is not None
    last_lat = n_lat_t - 1
    row_tile = lambda w: pl.BlockSpec((TM, w), lambda i: (i, 0))
    in_specs = [
        pl.BlockSpec((TM, d), (lambda i: (jnp.minimum(i, last_lat), 0)) if split_input else (lambda i: (i, 0))),
        pl.BlockSpec((1, N_MOD, d), lambda i: (i // n_lat_t, 0, 0)),
        _full((1, d)),
        pl.BlockSpec((None, None, d, 2 * f), lambda i: (layer, which, 0, 0)),
        pl.BlockSpec((None, None, f, d), lambda i: (layer, which, 0, 0)),
    ]
    args = [h, mod, g.reshape(1, d), w13_all, w2_all]
    if split_input:
        in_specs.append(pl.BlockSpec((TM, d), lambda i: (jnp.maximum(i - n_lat_t, 0), 0)))
        args.append(ctx_rows)
    for arr in mixer_args:
        in_specs.append(row_tile(arr.shape[1]) if arr.shape[0] == h.shape[0] else _full(arr.shape))
        args.append(arr)
    if final:
        in_specs.append(_full((1, d)))
        args.append(final_g.reshape(1, d))
    return pl.pallas_call(
        functools.partial(_ffn_kernel, mod_base=mod_base, final=final, n_lat_t=n_lat_t, split_input=split_input,
                          mixer=mixer),
        grid=(n_tiles,),
        in_specs=in_specs,
        out_specs=row_tile(d),
        out_shape=jax.ShapeDtypeStruct((n_tiles * TM, d), F32),
        compiler_params=_cparams("arbitrary"),
        name="ffn",
    )(*args)


def _even_proj_kernel(h_ref, mod_ref, g_ref, win_ref, gng_ref, qg_ref, kg_ref, spw_ref, spb_ref,
                      cos_r_ref, sin_r_ref, cos_c_ref, sin_c_ref, b256_ref, b128_ref,
                      gated_ref, q_ref, kd0_ref, kd1_ref, vt_ref, *, n_lat_t):
    x = h_ref[...]
    m = mod_ref[0]
    z = _rms_mod(x, g_ref[...], m[3:4], m[4:5]).astype(BF16)
    p = jnp.dot(z, win_ref[...], preferred_element_type=F32)
    aw = A_GROUPS * A_GROUP_DIM
    qw = ATT_HEADS * ATT_HEAD_DIM
    u = jax.nn.gelu(p[:, :aw])
    v = jax.nn.gelu(p[:, aw:2 * aw])
    b256 = b256_ref[...]
    vn = (v * lax.rsqrt(_group_mean_sq(v, b256) + EPS) * gng_ref[...]).astype(BF16)

    lane = lax.broadcasted_iota(jnp.int32, (CHUNK, LANES), 1)
    lo = lane < A_GROUP_DIM
    n_chunks = TM // CHUNK
    for j in range(aw // LANES):
        vt = jnp.concatenate([vn[c * CHUNK:(c + 1) * CHUNK, j * LANES:(j + 1) * LANES]
                              for c in range(n_chunks)], axis=1)
        s0 = jnp.dot(spw_ref[2 * j], vt, preferred_element_type=F32)
        s1 = jnp.dot(spw_ref[2 * j + 1], vt, preferred_element_type=F32)
        for c in range(n_chunks):
            s = (jnp.where(lo, s0[:, c * LANES:(c + 1) * LANES], s1[:, c * LANES:(c + 1) * LANES])
                 + spb_ref[:, j * LANES:(j + 1) * LANES])
            gated_ref[c * CHUNK:(c + 1) * CHUNK, j * LANES:(j + 1) * LANES] = (
                u[c * CHUNK:(c + 1) * CHUNK, j * LANES:(j + 1) * LANES] * s).astype(BF16)

    o = 2 * aw
    pq = p[:, o:o + qw]
    pk = p[:, o + qw:o + qw + LANES]
    va = p[:, o + qw + LANES:o + qw + 2 * LANES]
    qn = pq * lax.rsqrt(_group_mean_sq(pq, b256) + EPS) * qg_ref[...]
    kn = pk * lax.rsqrt(_group_mean_sq(pk, b128_ref[...]) + EPS) * kg_ref[...]

    is_lat = pl.program_id(0) < n_lat_t
    lane_g = lax.broadcasted_iota(jnp.int32, (GRID_W, LANES), 1)
    row_lanes = (lane_g % ATT_HEAD_DIM) < ATT_HEAD_DIM // 2

    def tile_table(r_ref, c_ref, ctx_value):
        col_part = jnp.where(is_lat, c_ref[...], ctx_value)
        return jnp.concatenate([jnp.where(row_lanes, r_ref[gr:gr + 1, :], col_part)
                                for gr in range(TM // GRID_W)], axis=0)

    cos = tile_table(cos_r_ref, cos_c_ref, 1.0)
    sin = tile_table(sin_r_ref, sin_c_ref, 0.0)
    lane_t = lax.broadcasted_iota(jnp.int32, (TM, LANES), 1)
    pair_dim = ATT_HEAD_DIM // 2
    half = pair_dim // 2
    first_half = (lane_t % pair_dim) < half
    lo_t = lane_t < ATT_HEAD_DIM

    def rope(t):
        partner = jnp.where(first_half, pltpu.roll(t, LANES - half, 1), pltpu.roll(t, half, 1))
        return t * cos + partner * sin

    scale = ATT_HEAD_DIM ** -0.5 * LOG2_E
    for j in range(qw // LANES):
        q_ref[:, j * LANES:(j + 1) * LANES] = (rope(qn[:, j * LANES:(j + 1) * LANES]) * scale).astype(BF16)
    kr = rope(kn)
    kswap = pltpu.roll(kr, ATT_HEAD_DIM, 1)
    kd0_ref[...] = jnp.where(lo_t, kr, kswap).astype(BF16)
    kd1_ref[...] = jnp.where(lo_t, kswap, kr).astype(BF16)
    va_t = va.T
    sub = lax.broadcasted_iota(jnp.int32, (V_ROWS - ATT_HEAD_DIM, TM), 0)
    ones_rows = jnp.where(sub == 0, 1.0, 0.0).astype(BF16)
    for g in range(ATT_KV_HEADS):
        vt_ref[g, 0, :ATT_HEAD_DIM, :] = va_t[g * ATT_HEAD_DIM:(g + 1) * ATT_HEAD_DIM].astype(BF16)
        vt_ref[g, 0, ATT_HEAD_DIM:, :] = ones_rows


def _group_matrix(width, gs):
    r = jnp.arange(width) // gs
    return jnp.where(r[:, None] == r[None, :], 1.0 / gs, 0.0).astype(BF16)


def _rope_tables(t_lat, rows):
    axis_dim = ATT_HEAD_DIM // 2
    n_freq = axis_dim // 2
    inv_freq = ROPE_THETA ** (-jnp.arange(0, axis_dim, 2, dtype=F32) / axis_dim)
    per_tile = TM // GRID_W
    n_lat_t = t_lat // TM
    n_ctx_t = (rows - t_lat) // TM
    lane = jnp.arange(LANES)
    sign = jnp.where((lane % axis_dim) < n_freq, -1.0, 1.0)
    ang_r = jnp.arange(t_lat // GRID_W, dtype=F32)[:, None] * inv_freq[lane % n_freq][None, :]
    ang_r = jnp.pad(ang_r.reshape(n_lat_t, per_tile, LANES), ((0, n_ctx_t), (0, SUBLANES - per_tile), (0, 0)))
    ang_r = ang_r.reshape((n_lat_t + n_ctx_t) * SUBLANES, LANES)
    ang_c = jnp.arange(GRID_W, dtype=F32)[:, None] * inv_freq[lane % n_freq][None, :]
    return jnp.cos(ang_r), jnp.sin(ang_r) * sign, jnp.cos(ang_c), jnp.sin(ang_c) * sign


def _even_proj(h, mod, g, w_in, gate_norm_g, sp_w, sp_b, q_norm_g, k_norm_g, t_lat, n_lat_t):
    rows, d = h.shape
    aw = A_GROUPS * A_GROUP_DIM
    qw = ATT_HEADS * ATT_HEAD_DIM
    cos_r, sin_r, cos_c, sin_c = _rope_tables(t_lat, rows)
    spb = jnp.repeat(sp_b.T, A_GROUP_DIM, axis=1)
    row_tile = lambda w: pl.BlockSpec((TM, w), lambda i: (i, 0))
    out_shape = [jax.ShapeDtypeStruct((rows, aw), BF16), jax.ShapeDtypeStruct((rows, qw), BF16)]
    out_shape += [jax.ShapeDtypeStruct((rows, LANES), BF16)] * 2
    out_shape += [jax.ShapeDtypeStruct((ATT_KV_HEADS, rows // TM, V_ROWS, TM), BF16)]
    vt_spec = pl.BlockSpec((ATT_KV_HEADS, 1, V_ROWS, TM), lambda i: (0, i, 0, 0))
    return pl.pallas_call(
        functools.partial(_even_proj_kernel, n_lat_t=n_lat_t),
        grid=(rows // TM,),
        in_specs=[
            row_tile(d),
            pl.BlockSpec((1, N_MOD, d), lambda i: (i // n_lat_t, 0, 0)),
            _full((1, d)),
            _full(w_in.shape),
            _full((1, aw)),
            _full((1, qw)),
            _full((1, LANES)),
            _full(sp_w.shape),
            _full((CHUNK, aw)),
            pl.BlockSpec((SUBLANES, LANES), lambda i: (i, 0)),
            pl.BlockSpec((SUBLANES, LANES), lambda i: (i, 0)),
            _full((GRID_W, LANES)),
            _full((GRID_W, LANES)),
            _full((MXU_TILE, MXU_TILE)),
            _full((LANES, LANES)),
        ],
        out_specs=[row_tile(aw), row_tile(qw), row_tile(LANES), row_tile(LANES), vt_spec],
        out_shape=out_shape,
        compiler_params=_cparams("arbitrary"),
        name="even_proj",
    )(h, mod, g.reshape(1, d), w_in.astype(BF16), gate_norm_g.reshape(1, aw),
      jnp.tile(q_norm_g, ATT_HEADS).reshape(1, qw), jnp.tile(k_norm_g, ATT_KV_HEADS).reshape(1, LANES),
      sp_w.astype(BF16), spb, cos_r, sin_r, cos_c, sin_c,
      _group_matrix(MXU_TILE, A_GROUP_DIM), _group_matrix(LANES, ATT_HEAD_DIM))


def _attn_kernel(bounded_ref, q_ref, kd0_ref, kd1_ref, vt_ref, o_ref, qs_ref, m_ref, acc_ref,
                 *, n_lat_t, t_lat, ctx_len):
    i = pl.program_id(0)
    tq = q_ref.shape[0]
    lane = lax.broadcasted_iota(jnp.int32, (tq, LANES), 1)
    lo = lane < ATT_HEAD_DIM
    for h in range(ATT_HEADS):
        t = q_ref[:, (h // 2) * LANES:(h // 2 + 1) * LANES]
        qs_ref[h * tq:(h + 1) * tq] = jnp.where(lo if h % 2 == 0 else ~lo, t, jnp.zeros_like(t))

    kd_refs = (kd0_ref, kd1_ref)
    n_streams = ATT_HEADS // 2
    blocks_per_chunk = ATT_TK // TM
    nt_dims = (((1,), (1,)), ((), ()))

    def group_of(s):
        return s // (n_streams // ATT_KV_HEADS)

    def scores(s, k_start, n_blk):
        kc = kd_refs[group_of(s)][pl.ds(k_start, n_blk * TM), :]
        qs = qs_ref[2 * s * tq:(2 * s + 2) * tq]
        return lax.dot_general(kc, qs, nt_dims, preferred_element_type=F32)

    def p_times_v(s, pt, n_blk, blk0):
        pv = None
        for j in range(n_blk):
            part = jnp.dot(vt_ref[group_of(s), blk0 + j], pt[j * TM:(j + 1) * TM], preferred_element_type=F32)
            pv = part if pv is None else pv + part
        return pv

    def softmax_pv_online(s, st, n_blk, blk0):
        m_old = m_ref[s]
        m_new = jnp.maximum(m_old, jnp.max(st, axis=0, keepdims=True))
        alpha = jnp.exp2(m_old - m_new)
        pt = jnp.exp2(st - m_new).astype(BF16)
        m_ref[s] = m_new
        acc_ref[s] = alpha * acc_ref[s] + p_times_v(s, pt, n_blk, blk0)

    def softmax_pv_bounded(s, st, n_blk, blk0):
        pt = jnp.exp2(st).astype(BF16)
        acc_ref[s] += p_times_v(s, pt, n_blk, blk0)

    def sweep(softmax_pv, lookahead):
        def step(k_start, n_blk, blk0):
            pending = [scores(s, k_start, n_blk) for s in range(min(lookahead, n_streams))]
            for s in range(n_streams):
                st = pending.pop(0)
                if s + lookahead < n_streams:
                    pending.append(scores(s + lookahead, k_start, n_blk))
                softmax_pv(s, st, n_blk, blk0)

        def body(c, carry):
            step(pl.multiple_of(c * ATT_TK, ATT_TK), blocks_per_chunk, c * blocks_per_chunk)
            return carry

        n_full = jnp.where(i < n_lat_t, t_lat // ATT_TK, 0)
        lax.fori_loop(0, n_full, body, 0)
        step(t_lat, ctx_len // TM, t_lat // TM)

    acc_ref[...] = jnp.zeros(acc_ref.shape, F32)
    bounded = bounded_ref[0] == 1

    @pl.when(bounded)
    def _():
        sweep(softmax_pv_bounded, ATT_LOOKAHEAD_BOUNDED)

    @pl.when(jnp.logical_not(bounded))
    def _():
        m_ref[...] = jnp.full(m_ref.shape, -jnp.inf, F32)
        sweep(softmax_pv_online, ATT_LOOKAHEAD)

    for s in range(n_streams):
        acc = acc_ref[s]
        o_t = acc[:ATT_HEAD_DIM] * (1.0 / acc[ATT_HEAD_DIM:ATT_HEAD_DIM + 1])
        pair = jnp.concatenate([o_t[:, :tq], o_t[:, tq:]], axis=0)
        o_ref[:, s * LANES:(s + 1) * LANES] = pair.T.astype(BF16)


def _attention(q, kd0, kd1, vt, q_norm_g, k_norm_g, t_lat, n_lat_t):
    rows, qw = q.shape
    ctx_len = rows - t_lat
    score_bound = (ATT_HEAD_DIM ** 0.5 * LOG2_E) * jnp.max(jnp.abs(q_norm_g)) * jnp.max(jnp.abs(k_norm_g))
    bounded = (score_bound <= ATT_SCORE_BOUND).astype(jnp.int32).reshape(1)
    return pl.pallas_call(
        functools.partial(_attn_kernel, n_lat_t=n_lat_t, t_lat=t_lat, ctx_len=ctx_len),
        grid=(rows // TM,),
        in_specs=[pl.BlockSpec(memory_space=pltpu.SMEM),
                  pl.BlockSpec((TM, qw), lambda i: (i, 0)), _full((rows, LANES)), _full((rows, LANES)),
                  _full(vt.shape)],
        out_specs=pl.BlockSpec((TM, qw), lambda i: (i, 0)),
        out_shape=jax.ShapeDtypeStruct((rows, qw), BF16),
        scratch_shapes=[
            pltpu.VMEM((ATT_HEADS * TM, LANES), BF16),
            pltpu.VMEM((ATT_HEADS // 2, 1, 2 * TM), F32),
            pltpu.VMEM((ATT_HEADS // 2, V_ROWS, 2 * TM), F32),
        ],
        compiler_params=_cparams("arbitrary"),
        name="gqa_attention",
    )(bounded, q, kd0, kd1, vt)


def _odd_proj_kernel(h_ref, mod_ref, g_ref, win_ref, gb_ref, pqk_ref, vt_ref, o_ref, gates_ref):
    x = h_ref[...]
    m = mod_ref[0]
    z = _rms_mod(x, g_ref[...], m[3:4], m[4:5]).astype(BF16)
    p = jnp.dot(z, win_ref[...], preferred_element_type=F32)
    qk = 2 * ML_HEADS * ML_QK_DIM
    vw = ML_HEADS * ML_V_DIM
    pqk_ref[...] = p[:, :qk]
    for c in range(TM // CHUNK):
        vt_ref[c] = p[c * CHUNK:(c + 1) * CHUNK, qk:qk + vw].T.astype(BF16)
    o_ref[...] = p[:, qk + vw:qk + 2 * vw]
    gts = p[:, qk + 2 * vw:] + gb_ref[...]
    lane = lax.broadcasted_iota(jnp.int32, gts.shape, 1)
    is_f = (lane < 4 * ML_HEADS) & ((lane // ML_HEADS) % 2 == 1)
    log_sig = jnp.minimum(gts, 0.0) - jnp.log1p(jnp.exp(-jnp.abs(gts)))
    gates_ref[...] = jnp.where(is_f, log_sig, gts)


def _odd_proj(h, mod, g, w_in, gate_b, n_lat_t):
    rows, d = h.shape
    qk = 2 * ML_HEADS * ML_QK_DIM
    vw = ML_HEADS * ML_V_DIM
    n_gate = 4 * ML_HEADS
    w_pad = jnp.pad(w_in, ((0, 0), (0, LANES - n_gate))).astype(BF16)
    gb = jnp.pad(gate_b, (0, LANES - n_gate)).reshape(1, LANES)
    row_tile = lambda w: pl.BlockSpec((TM, w), lambda i: (i, 0))
    return pl.pallas_call(
        _odd_proj_kernel,
        grid=(rows // TM,),
        in_specs=[row_tile(d), pl.BlockSpec((1, N_MOD, d), lambda i: (i // n_lat_t, 0, 0)),
                  _full((1, d)), _full(w_pad.shape), _full((1, LANES))],
        out_specs=[row_tile(qk), pl.BlockSpec((TM // CHUNK, vw, CHUNK), lambda i: (i, 0, 0)),
                   row_tile(vw), row_tile(LANES)],
        out_shape=[jax.ShapeDtypeStruct((rows, qk), F32), jax.ShapeDtypeStruct((rows // CHUNK, vw, CHUNK), BF16),
                   jax.ShapeDtypeStruct((rows, vw), F32), jax.ShapeDtypeStruct((rows, LANES), F32)],
        compiler_params=_cparams("arbitrary"),
        name="odd_proj",
    )(h, mod, g.reshape(1, d), w_pad, gb)


def _conv_kernel(x_ref, prev_ref, next_ref, w_ref, q_ref, k_ref, *, t_lat, rows):
    x = x_ref[...]
    tm = x.shape[0]
    row = lax.broadcasted_iota(jnp.int32, x.shape, 0)
    token = row + pl.program_id(0) * tm
    xm = jnp.where(row == 0, prev_ref[SUBLANES - 1:SUBLANES, :], pltpu.roll(x, 1, 0))
    xp = jnp.where(row == tm - 1, next_ref[0:1, :], pltpu.roll(x, tm - 1, 0))
    xm = jnp.where((token == 0) | (token == t_lat), 0.0, xm)
    xp = jnp.where((token == t_lat - 1) | (token == rows - 1), 0.0, xp)
    y = xm * w_ref[0:1, :] + x * w_ref[1:2, :] + xp * w_ref[2:3, :]
    y = _silu(y)
    qk = ML_HEADS * ML_QK_DIM
    q_ref[...] = (y[:, :qk] * (ML_QK_DIM ** -0.5)).astype(BF16)
    k_ref[...] = y[:, qk:].astype(BF16)


def _conv(pqk, conv_w, t_lat):
    rows, w = pqk.shape
    n_tiles = rows // TM
    tm = TM * max(t for t in range(1, CONV_MAX_TILES + 1) if n_tiles % t == 0)
    per = tm // SUBLANES
    last_blk = rows // SUBLANES - 1
    return pl.pallas_call(
        functools.partial(_conv_kernel, t_lat=t_lat, rows=rows),
        grid=(rows // tm,),
        in_specs=[
            pl.BlockSpec((tm, w), lambda i: (i, 0)),
            pl.BlockSpec((SUBLANES, w), lambda i: (jnp.maximum(i * per - 1, 0), 0)),
            pl.BlockSpec((SUBLANES, w), lambda i: (jnp.minimum((i + 1) * per, last_blk), 0)),
            _full(conv_w.shape),
        ],
        out_specs=[pl.BlockSpec((tm, w // 2), lambda i: (i, 0))] * 2,
        out_shape=[jax.ShapeDtypeStruct((rows, w // 2), BF16)] * 2,
        compiler_params=_cparams("arbitrary"),
        name="short_conv",
    )(pqk, pqk, pqk, conv_w)


def _mlstm_prepare(d, g):
    ln = CHUNK
    r = lax.broadcasted_iota(jnp.int32, (ln, ln), 0)
    c = lax.broadcasted_iota(jnp.int32, (ln, ln), 1)
    tri = (c <= r) if d == 0 else (c >= r)
    tri_b = jnp.where(tri, 1.0, 0.0).astype(BF16)
    g_t = g.T
    nt_dims = (((1,), (1,)), ((), ()))
    bc_col = sum(jnp.dot(tri_b, piece, preferred_element_type=F32) for piece in _split3(g))
    bc_row = sum(lax.dot_general(piece, tri_b, nt_dims, preferred_element_type=F32) for piece in _split3(g_t))
    il0 = 2 * ML_HEADS * d
    fl0 = il0 + ML_HEADS
    u = g_t[il0:il0 + ML_HEADS, :] - bc_row[fl0:fl0 + ML_HEADS, :]
    lane = lax.broadcasted_iota(jnp.int32, u.shape, 1)
    sh = 1
    while sh < ln:
        if d == 0:
            shifted = jnp.where(lane >= sh, pltpu.roll(u, sh, 1), -jnp.inf)
        else:
            shifted = jnp.where(lane < ln - sh, pltpu.roll(u, ln - sh, 1), -jnp.inf)
        u = jnp.maximum(u, shifted)
        sh *= 2
    vis = (c >= r) if d == 0 else (c <= r)
    return dict(vis=vis, g=g, g_t=g_t, bc_col=bc_col, bc_row=bc_row, run_max=u,
                lane_lo=c < ML_QK_DIM, last=ln - 1 if d == 0 else 0)


def _mlstm_head(d, h, pre, q_pair, k_pair, vt_ref, o_ref, c_ref, m_ref):
    half = h % 2
    nt_dims = (((1,), (1,)), ((), ()))
    il = 2 * ML_HEADS * d + h
    fl = il + ML_HEADS
    keep_lane = pre["lane_lo"] if half == 0 else ~pre["lane_lo"]
    brow = pre["bc_row"][fl:fl + 1, :]
    irow = pre["g_t"][il:il + 1, :]
    src_col = pre["g"][:, il:il + 1] - pre["bc_col"][:, fl:fl + 1]
    btot = pre["bc_col"][pre["last"]:pre["last"] + 1, fl:fl + 1]
    m_prev = m_ref[h:h + 1, 0:1]

    inter = brow + m_prev
    m_row = jnp.maximum(inter, brow + pre["run_max"][h:h + 1, :])
    wts_t = jnp.exp(jnp.where(pre["vis"], (brow - m_row) + src_col, -jnp.inf))
    a = jnp.exp(inter - m_row)

    qm = jnp.where(keep_lane, q_pair, jnp.zeros_like(q_pair))
    s_t = lax.dot_general(k_pair, qm, nt_dims, preferred_element_type=F32) * wts_t
    vt = vt_ref[h * ML_V_DIM:(h + 1) * ML_V_DIM, :]
    vt_aug = jnp.concatenate([vt, jnp.ones((ML_AUX_ROWS, vt.shape[1]), BF16)], axis=0)
    c_aug = c_ref[h]
    num_den = (jnp.dot(vt_aug, s_t.astype(BF16), preferred_element_type=F32)
               + a * lax.dot_general(c_aug.astype(BF16), qm, nt_dims, preferred_element_type=F32))
    den = num_den[ML_V_DIM:ML_V_DIM + 1]
    h_t = num_den[:ML_V_DIM] / jnp.maximum(jnp.abs(den), jnp.exp(-m_row))
    o_ref[:, h * LANES:(h + 1) * LANES] = h_t.T

    w_row = btot - brow + irow
    m_loc = jnp.max(w_row, axis=1, keepdims=True)
    e = jnp.exp(w_row - m_loc)
    ev_t = (vt_aug.astype(F32) * e).astype(BF16)
    km = jnp.where(keep_lane, k_pair, jnp.zeros_like(k_pair))
    c_loc = jnp.dot(ev_t, km, preferred_element_type=F32)
    m_new = jnp.maximum(btot + m_prev, m_loc)
    decay = jnp.exp(btot + m_prev - m_new)
    gain = jnp.exp(m_loc - m_new)
    c_ref[h] = decay * c_aug + gain * c_loc
    m_ref[h:h + 1, :] = jnp.broadcast_to(m_new, (1, LANES))


def _mlstm_kernel(qf_ref, kf_ref, vf_ref, gf_ref, qb_ref, kb_ref, vb_ref, gb_ref, of_ref, ob_ref,
                  ct_ref, m_ref):
    @pl.when(pl.program_id(0) == 0)
    def _():
        ct_ref[...] = jnp.zeros(ct_ref.shape, F32)
        m_ref[...] = jnp.zeros(m_ref.shape, F32)

    dirs = ((qf_ref, kf_ref, vf_ref, gf_ref, of_ref), (qb_ref, kb_ref, vb_ref, gb_ref, ob_ref))

    def chunk_of(d, step):
        return step if d == 0 else ML_CHUNKS_PER_STEP - 1 - step

    def chunk_rows(d, step):
        return pl.ds(chunk_of(d, step) * CHUNK, CHUNK)

    pre = [[_mlstm_prepare(d, dirs[d][3][chunk_rows(d, step), :]) for d in range(2)]
           for step in range(ML_CHUNKS_PER_STEP)]
    for step in range(ML_CHUNKS_PER_STEP):
        for pair in range(ML_HEADS // 2):
            loaded = []
            for d in range(2):
                rows = chunk_rows(d, step)
                loaded.append((dirs[d][0][rows, pair * LANES:(pair + 1) * LANES],
                               dirs[d][1][rows, pair * LANES:(pair + 1) * LANES]))
            for half in range(2):
                for d in range(2):
                    q_pair, k_pair = loaded[d]
                    _mlstm_head(d, 2 * pair + half, pre[step][d], q_pair, k_pair,
                                dirs[d][2].at[chunk_of(d, step)], dirs[d][4].at[chunk_rows(d, step)],
                                ct_ref.at[d], m_ref.at[d])


def _mlstm(q, k, vt, gates, n_lat_ch):
    rows = q.shape[0]
    step_rows = ML_CHUNKS_PER_STEP * CHUNK
    assert rows % step_rows == 0 and (n_lat_ch * CHUNK) % step_rows == 0
    nblk = rows // step_rows
    n_lat_blk = n_lat_ch * CHUNK // step_rows
    fwd = lambda i: (i + n_lat_blk) % nblk
    bwd = lambda i: nblk - 1 - i
    vw = vt.shape[1]

    def blocks(order):
        blk = lambda w: pl.BlockSpec((step_rows, w), lambda i: (order(i), 0))
        vt_blk = pl.BlockSpec((ML_CHUNKS_PER_STEP, vw, CHUNK), lambda i: (order(i), 0, 0))
        return [blk(q.shape[1]), blk(k.shape[1]), vt_blk, blk(LANES)]

    out_blk = lambda order: pl.BlockSpec((step_rows, vw), lambda i: (order(i), 0))
    return pl.pallas_call(
        _mlstm_kernel,
        grid=(nblk,),
        in_specs=blocks(fwd) + blocks(bwd),
        out_specs=[out_blk(fwd), out_blk(bwd)],
        out_shape=[jax.ShapeDtypeStruct((rows, vw), F32)] * 2,
        scratch_shapes=[
            pltpu.VMEM((2, ML_HEADS, ML_V_DIM + ML_AUX_ROWS, LANES), F32),
            pltpu.VMEM((2, ML_HEADS, LANES), F32),
        ],
        compiler_params=_cparams("arbitrary"),
        name="mlstm",
    )(q, k, vt, gates, q, k, vt, gates)


def kernel(x, c, ctx, c_ctx, mod_w, mod_b, norm_g, ffn_w13, ffn_w2, ab_w_in, ab_gate_norm_g, ab_spatial_w, ab_spatial_b, ab_q_norm_g, ab_k_norm_g, ab_w_out, ml_w_in, ml_conv_w, ml_gate_b, ml_out_norm_g, ml_w_out, final_norm_g):
    b, t_lat, d = x.shape
    ctx_len = ctx.shape[1]
    depth = mod_w.shape[0]
    assert b == 1 and t_lat % TM == 0 and ctx_len % TM == 0 and t_lat % ATT_TK == 0
    n_lat_t = t_lat // TM
    n_all_t = (t_lat + ctx_len) // TM

    mods = _mods(c, c_ctx, mod_w, mod_b)
    w13_all = ffn_w13.astype(BF16)
    w2_all = ffn_w2.astype(BF16)
    h = x[0]
    for l in range(depth):
        keep_ctx = l < depth - 1
        mod = mods[l]
        h = _ffn(h, mod, norm_g[l, 0], w13_all, w2_all, l, 0, n_all_t, n_lat_t, 0,
                 ctx_rows=ctx[0] if l == 0 else None)
        n_out_t = n_all_t if keep_ctx else n_lat_t
        if l % 2 == 0:
            e = l // 2
            gated, q, kd0, kd1, vt = _even_proj(
                h, mod, norm_g[l, 1], ab_w_in[e], ab_gate_norm_g[e], ab_spatial_w[e], ab_spatial_b[e],
                ab_q_norm_g[e], ab_k_norm_g[e], t_lat, n_lat_t)
            attn = _attention(q, kd0, kd1, vt, ab_q_norm_g[e], ab_k_norm_g[e], t_lat, n_lat_t)
            mixer, mixer_args = "even", (gated, attn, ab_w_out[e].astype(BF16))
        else:
            o = l // 2
            pqk, vt, og, gates = _odd_proj(h, mod, norm_g[l, 1], ml_w_in[o], ml_gate_b[o], n_lat_t)
            q, k = _conv(pqk, ml_conv_w[o], t_lat)
            hf, hb = _mlstm(q, k, vt, gates, t_lat // CHUNK)
            mixer, mixer_args = "odd", (hf, hb, og, ml_out_norm_g[o].reshape(1, -1), ml_w_out[o].astype(BF16))
        final_g = final_norm_g if l == depth - 1 else None
        h = _ffn(h, mod, norm_g[l, 2], w13_all, w2_all, l, 1, n_out_t, n_lat_t, 6, final_g,
                 mixer=mixer, mixer_args=mixer_args)
    return h[:t_lat][None]
```
